```python
import math
import jax
import jax.numpy as jnp
from jax import lax
import numpy as np

D_MODEL = 1024
BATCH = 8
SEQ = 2048
DEPTH = 4
DEC_BATCH = 32
DEC_SEQ = 4
PAST_LEN = 8192
PAGE_SIZE = 128

N_MIXERS = 3
ALPHA = (2 * DEPTH) ** 0.25
BETA = (8 * DEPTH) ** -0.25
LN_EPS = 1e-5
RMS_EPS = 1e-5
CONV_W = 4
NEG_INF = -1e30

D_RNN = (4 * D_MODEL // 3) // 64 * 64
LRU_BLOCKS = 16
LRU_BS = D_RNN // LRU_BLOCKS
LRU_C = 8.0

NSA_HEADS = 16
NSA_KV_HEADS = 4
NSA_HD = D_MODEL // NSA_HEADS
NSA_GROUP = NSA_HEADS // NSA_KV_HEADS
CMP_BLOCK = 32
SEL_BLOCK = 64
SEL_TOPK = 16
WINDOW = 512
Q_BLOCK = 128
FORCE_SCORE = 1e4
ROPE_THETA = 10000.0
NSA_PROJ = NSA_HEADS * NSA_HD + 6 * NSA_KV_HEADS * NSA_HD + 3 * NSA_HEADS

SSD_INNER = 2 * D_MODEL
SSD_HEAD_DIM = 64
SSD_HEADS = SSD_INNER // SSD_HEAD_DIM
SSD_GROUPS = 4
SSD_STATE = 128
SSD_CHUNK = 128
SSD_CONV_DIM = SSD_INNER + 2 * SSD_GROUPS * SSD_STATE
SSD_PROJ = SSD_INNER + SSD_CONV_DIM + SSD_HEADS

D_FF = 2816
N_EXPERTS = 8
TOP_K = 2
D_FF_EXPERT = 3584

N_LRU_LAYERS = (DEPTH + 2) // 3
N_NSA_LAYERS = (DEPTH + 1) // 3
N_SSD_LAYERS = DEPTH // 3
N_DENSE_LAYERS = (DEPTH + 1) // 2
N_MOE_LAYERS = DEPTH // 2

kernel_name = 'hybrid_lru_nsa_ssd_moe_decoder_step'


def layer_norm(x, g, b):
    xf = x.astype(jnp.float32)
    mu = jnp.mean(xf, -1, keepdims=True)
    var = jnp.mean(jnp.square(xf - mu), -1, keepdims=True)
    y = (xf - mu) * lax.rsqrt(var + LN_EPS) * g.astype(jnp.float32) + b.astype(jnp.float32)
    return y.astype(x.dtype)


def causal_conv(x, buf, w, b):
    s = x.shape[1]
    xp = jnp.concatenate([buf.astype(x.dtype), x], axis=1)
    y = xp[:, 0:s] * w[0]
    for k in range(1, CONV_W):
        y = y + xp[:, k:k + s] * w[k]
    return y + b, xp[:, s:]


def rope(x, pos):
    hd = x.shape[-1]
    inv = ROPE_THETA ** (-jnp.arange(0, hd, 2, dtype=jnp.float32) / hd)
    ang = pos[:, None] * inv[None, :]
    cos = jnp.cos(ang)[:, None, :]
    sin = jnp.sin(ang)[:, None, :]
    xf = x.astype(jnp.float32)
    x1, x2 = xf[..., :hd // 2], xf[..., hd // 2:]
    return jnp.concatenate([x1 * cos - x2 * sin, x2 * cos + x1 * sin], -1).astype(x.dtype)


def _linear_combine(left, right):
    a1, b1 = left
    a2, b2 = right
    return a1 * a2, a2 * b1 + b2


def rglru_mixer(x, conv_buf, h0, w_in, conv_w, conv_b, w_a, b_a, w_x, b_x, lam, w_out):
    f32 = jnp.float32
    bsz, s, _ = x.shape
    u = x @ w_in
    gate_br, xb = u[..., :D_RNN], u[..., D_RNN:]
    xc, new_buf = causal_conv(xb, conv_buf, conv_w, conv_b)
    xblk = xc.reshape(bsz, s, LRU_BLOCKS, LRU_BS)
    r = jax.nn.sigmoid(jnp.einsum('bsnk,nkj->bsnj', xblk, w_a).reshape(bsz, s, D_RNN).astype(f32) + b_a.astype(f32))
    gi = jax.nn.sigmoid(jnp.einsum('bsnk,nkj->bsnj', xblk, w_x).reshape(bsz, s, D_RNN).astype(f32) + b_x.astype(f32))
    log_a = -LRU_C * jax.nn.softplus(-lam.astype(f32)) * r
    a = jnp.exp(log_a)
    bt = jnp.sqrt(-jnp.expm1(2.0 * log_a)) * gi * xc.astype(f32)
    a_cum, b_cum = lax.associative_scan(_linear_combine, (a, bt), axis=1)
    h = a_cum * h0.astype(f32)[:, None, :] + b_cum
    y = (h * jax.nn.gelu(gate_br.astype(f32))).astype(x.dtype) @ w_out
    return y, new_buf, h[:, -1].astype(x.dtype)


def nsa_sequence(q, kv, win, gates, q_pos0, kw_pos0):
    f32 = jnp.float32
    sq, t_len = q.shape[0], kv.shape[0]
    scale = NSA_HD ** -0.5
    kv = kv.astype(f32)
    nc = t_len // CMP_BLOCK
    blocks = kv[:nc * CMP_BLOCK, :2].reshape(nc, CMP_BLOCK, 2, NSA_KV_HEADS, NSA_HD).mean(1)
    kcb = rope(blocks[:, 0], jnp.arange(nc, dtype=f32) * CMP_BLOCK + (CMP_BLOCK - 1) / 2.0)
    vcb = blocks[:, 1]
    ks, vs = kv[:, 2], kv[:, 3]
    nsb = -(-t_len // SEL_BLOCK)
    ratio = SEL_BLOCK // CMP_BLOCK
    ksel = min(SEL_TOPK, nsb)
    win_p = jnp.concatenate([jnp.zeros((WINDOW,) + win.shape[1:], f32), win.astype(f32)], 0)
    qb = math.gcd(sq, Q_BLOCK)
    n_wk = WINDOW + qb - 1
    kv_idx = jnp.arange(NSA_KV_HEADS)[None, :, None]

    def block(i):
        q0 = i * qb
        qi = lax.dynamic_slice_in_dim(q, q0, qb, 0).astype(f32).reshape(qb, NSA_KV_HEADS, NSA_GROUP, NSA_HD) * scale
        gi = lax.dynamic_slice_in_dim(gates, q0, qb, 0).astype(f32).reshape(qb, NSA_KV_HEADS, NSA_GROUP, 3)
        t = q_pos0 + q0 + jnp.arange(qb)
        cmask = ((jnp.arange(nc) + 1) * CMP_BLOCK - 1)[None, :] <= t[:, None]
        s_c = jnp.einsum('qkgd,ckd->qkgc', qi, kcb)
        p_c = jax.nn.softmax(jnp.where(cmask[:, None, None, :], s_c, NEG_INF), -1)
        p_c = p_c * jnp.any(cmask, -1).astype(f32)[:, None, None, None]
        o_c = jnp.einsum('qkgc,ckd->qkgd', p_c, vcb)
        imp = jnp.pad(p_c.sum(2), ((0, 0), (0, 0), (0, nsb * ratio - nc))).reshape(qb, NSA_KV_HEADS, nsb, ratio).sum(-1)
        blk = jnp.arange(nsb)
        valid = blk[None, :] * SEL_BLOCK <= t[:, None]
        forced = (blk[None, :] == 0) | (blk[None, :] == (t // SEL_BLOCK)[:, None])
        score = jnp.where(forced[:, None, :], FORCE_SCORE, jnp.where(valid[:, None, :], imp, -1.0))
        _, sel = lax.top_k(score, ksel)
        pos = (sel[..., None] * SEL_BLOCK + jnp.arange(SEL_BLOCK)).reshape(qb, NSA_KV_HEADS, ksel * SEL_BLOCK)
        smask = pos <= t[:, None, None]
        posc = jnp.minimum(pos, t_len - 1)
        kg = ks[posc, kv_idx]
        vg = vs[posc, kv_idx]
        s_s = jnp.einsum('qkgd,qkld->qkgl', qi, kg)
        p_s = jax.nn.softmax(jnp.where(smask[:, :, None, :], s_s, NEG_INF), -1)
        o_s = jnp.einsum('qkgl,qkld->qkgd', p_s, vg)
        start = q_pos0 + q0 + 1 - kw_pos0
        wk = lax.dynamic_slice_in_dim(win_p, start, n_wk, 0)
        spos = kw_pos0 - WINDOW + start + jnp.arange(n_wk)
        wmask = (spos[None, :] <= t[:, None]) & (spos[None, :] > t[:, None] - WINDOW) & (spos[None, :] >= 0)
        s_w = jnp.einsum('qkgd,lkd->qkgl', qi, wk[:, 0])
        p_w = jax.nn.softmax(jnp.where(wmask[:, None, None, :], s_w, NEG_INF), -1)
        o_w = jnp.einsum('qkgl,lkd->qkgd', p_w, wk[:, 1])
        o = gi[..., 0:1] * o_c + gi[..., 1:2] * o_s + gi[..., 2:3] * o_w
        return o.reshape(qb, NSA_HEADS * NSA_HD)

    out = lax.map(block, jnp.arange(sq // qb))
    return out.reshape(sq, NSA_HEADS * NSA_HD)


def nsa_mixer(x, past_kv, win_buf, q_pos0, w_in, w_out):
    bsz, s, _ = x.shape
    hq = NSA_HEADS * NSA_HD
    hk = NSA_KV_HEADS * NSA_HD
    u = x @ w_in
    q = u[..., :hq].reshape(bsz, s, NSA_HEADS, NSA_HD)
    kvs = u[..., hq:hq + 6 * hk].reshape(bsz, s, 6, NSA_KV_HEADS, NSA_HD)
    gates = jax.nn.sigmoid(u[..., hq + 6 * hk:].reshape(bsz, s, NSA_HEADS, 3))
    pos = q_pos0 + jnp.arange(s, dtype=jnp.float32)
    rot = jax.vmap(lambda a: rope(a, pos))
    q = rot(q)
    k_sel = rot(kvs[:, :, 2])
    k_win = rot(kvs[:, :, 4])
    rows = jnp.stack([kvs[:, :, 0], kvs[:, :, 1], k_sel, kvs[:, :, 3]], axis=2)
    wrows = jnp.stack([k_win, kvs[:, :, 5]], axis=2)
    full = rows if past_kv is None else jnp.concatenate([past_kv.astype(rows.dtype), rows], 1)
    wfull = wrows if win_buf is None else jnp.concatenate([win_buf.astype(wrows.dtype), wrows], 1)
    tw = wfull.shape[1]
    kw_pos0 = q_pos0 + s - tw
    o = lax.map(lambda a: nsa_sequence(a[0], a[1], a[2], a[3], q_pos0, kw_pos0), (q, full, wfull, gates))
    y = o.astype(x.dtype) @ w_out
    new_win = wfull[:, tw - min(WINDOW, tw):]
    return y, rows, new_win


def ssd_scan(xh, dt, a, bm, cm, s0):
    f32 = jnp.float32
    bsz, s = xh.shape[:2]
    q_len = math.gcd(s, SSD_CHUNK)
    nch = s // q_len
    hg = SSD_HEADS // SSD_GROUPS
    x = xh.astype(f32).reshape(bsz, nch, q_len, SSD_GROUPS, hg, SSD_HEAD_DIM)
    dtc = dt.reshape(bsz, nch, q_len, SSD_GROUPS, hg)
    bc = bm.astype(f32).reshape(bsz, nch, q_len, SSD_GROUPS, SSD_STATE)
    cc = cm.astype(f32).reshape(bsz, nch, q_len, SSD_GROUPS, SSD_STATE)
    acum = jnp.cumsum(dtc * a.reshape(SSD_GROUPS, hg), axis=2)
    xdt = x * dtc[..., None]
    causal = jnp.tril(jnp.ones((q_len, q_len), bool))[None, None, :, :, None, None]
    seg = acum[:, :, :, None] - acum[:, :, None, :]
    decay_in = jnp.exp(jnp.where(causal, seg, NEG_INF))
    cb = jnp.einsum('bcqgn,bckgn->bcqkg', cc, bc)
    y_diag = jnp.einsum('bcqkgh,bckghp->bcqghp', cb[..., None] * decay_in, xdt)
    decay_out = jnp.exp(acum[:, :, -1:] - acum)
    states = jnp.einsum('bckgn,bckghp->bcghpn', bc, xdt * decay_out[..., None])
    chunk_decay = jnp.exp(acum[:, :, -1])

    def step(st, inp):
        dec, new = inp
        return dec[..., None, None] * st + new, st

    s_init = s0.astype(f32).reshape(bsz, SSD_GROUPS, hg, SSD_HEAD_DIM, SSD_STATE)
    s_fin, prev = lax.scan(step, s_init, (jnp.moveaxis(chunk_decay, 1, 0), jnp.moveaxis(states, 1, 0)))
    prev = jnp.moveaxis(prev, 0, 1)
    y_off = jnp.einsum('bcqgn,bcghpn->bcqghp', cc, prev) * jnp.exp(acum)[..., None]
    y = (y_diag + y_off).reshape(bsz, s, SSD_HEADS, SSD_HEAD_DIM)
    return y, s_fin.reshape(bsz, SSD_HEADS, SSD_HEAD_DIM, SSD_STATE)


def ssd_mixer(x, conv_buf, s0, w_in, conv_w, conv_b, dt_bias, a_log, d_skip, norm_g, w_out):
    f32 = jnp.float32
    bsz, s, _ = x.shape
    gn = SSD_GROUPS * SSD_STATE
    u = x @ w_in
    z = u[..., :SSD_INNER]
    xbc = u[..., SSD_INNER:SSD_INNER + SSD_CONV_DIM]
    dt_raw = u[..., SSD_INNER + SSD_CONV_DIM:]
    xbc, new_buf = causal_conv(xbc, conv_buf, conv_w, conv_b)
    xbc = jax.nn.silu(xbc)
    xh = xbc[..., :SSD_INNER].reshape(bsz, s, SSD_HEADS, SSD_HEAD_DIM)
    bm = xbc[..., SSD_INNER:SSD_INNER + gn].reshape(bsz, s, SSD_GROUPS, SSD_STATE)
    cm = xbc[..., SSD_INNER + gn:].reshape(bsz, s, SSD_GROUPS, SSD_STATE)
    dt = jax.nn.softplus(dt_raw.astype(f32) + dt_bias.astype(f32))
    a = -jnp.exp(a_log.astype(f32))
    y, s_fin = ssd_scan(xh, dt, a, bm, cm, s0)
    y = y + d_skip.astype(f32)[:, None] * xh.astype(f32)
    yg = (y.reshape(bsz, s, SSD_INNER) * jax.nn.silu(z.astype(f32))).reshape(bsz, s, SSD_GROUPS, SSD_INNER // SSD_GROUPS)
    yg = yg * lax.rsqrt(jnp.mean(yg * yg, -1, keepdims=True) + RMS_EPS)
    y = (yg.reshape(bsz, s, SSD_INNER) * norm_g.astype(f32)).astype(x.dtype)
    return y @ w_out, new_buf, s_fin.astype(x.dtype)


def swiglu(x, w_in, w_out):
    h = x @ w_in
    f = w_in.shape[-1] // 2
    return (jax.nn.silu(h[..., :f]) * h[..., f:]) @ w_out


def moe_ffn(x, router_w, router_b, w_in, w_out):
    f32 = jnp.float32
    shp = x.shape
    xf = x.reshape(-1, D_MODEL)
    logits = xf.astype(f32) @ router_w.astype(f32) + router_b.astype(f32)
    top_v, top_i = lax.top_k(logits, TOP_K)
    w = jax.nn.softmax(top_v, -1)
    combine = jnp.sum(jax.nn.one_hot(top_i, N_EXPERTS, dtype=f32) * w[..., None], axis=1)
    out = jnp.zeros(xf.shape, f32)
    for e in range(N_EXPERTS):
        out = out + combine[:, e:e + 1] * swiglu(xf, w_in[e], w_out[e]).astype(f32)
    return out.astype(x.dtype).reshape(shp)


def setup_inputs(seed: int = 0) -> dict:
    key = jax.random.key(seed)
    keys = iter(jax.random.split(key, 64))
    f32 = jnp.float32

    def nrm(shape, scale):
        return jax.random.normal(next(keys), shape, f32) * scale

    def unif(shape, lo, hi):
        return jax.random.uniform(next(keys), shape, f32, lo, hi)

    n_pages = PAST_LEN // PAGE_SIZE
    n_used = DEC_BATCH * n_pages
    n_pool = n_used + max(1, n_used // 4)
    page_table = jax.random.permutation(next(keys), n_pool)[:n_used].reshape(DEC_BATCH, n_pages).astype(jnp.int32)
    win_len = min(WINDOW, PAST_LEN)
    nl, nn, ns, nd, nm = N_LRU_LAYERS, N_NSA_LAYERS, N_SSD_LAYERS, N_DENSE_LAYERS, N_MOE_LAYERS
    a0 = unif((nl, D_RNN), 0.9, 0.999)
    p = a0 ** (1.0 / LRU_C)
    lru_lam = jnp.log(p) - jnp.log1p(-p)
    dt0 = jnp.exp(unif((ns, SSD_HEADS), math.log(1e-3), math.log(1e-1)))
    ssd_dt_bias = dt0 + jnp.log(-jnp.expm1(-dt0))
    ssd_a_log = jnp.log(unif((ns, SSD_HEADS), 1.0, 16.0))
    dm = D_MODEL ** -0.5
    return {
        'x_prompt': nrm((BATCH, SEQ, D_MODEL), 1.0),
        'x_sample': nrm((DEC_BATCH, DEC_SEQ, D_MODEL), 1.0),
        'state_l0_lru_conv': nrm((DEC_BATCH, CONV_W - 1, D_RNN), 1.0),
        'state_l0_lru_h': nrm((DEC_BATCH, D_RNN), 0.5),
        'cache_l1_nsa_kv': nrm((n_pool, PAGE_SIZE, 4, NSA_KV_HEADS, NSA_HD), 1.0),
        'cache_l1_nsa_win': nrm((DEC_BATCH, win_len, 2, NSA_KV_HEADS, NSA_HD), 1.0),
        'page_table': page_table,
        'state_l2_ssd_conv': nrm((DEC_BATCH, CONV_W - 1, SSD_CONV_DIM), 1.0),
        'state_l2_ssd_ssm': nrm((DEC_BATCH, SSD_HEADS, SSD_HEAD_DIM, SSD_STATE), 0.1),
        'state_l3_lru_conv': nrm((DEC_BATCH, CONV_W - 1, D_RNN), 1.0),
        'state_l3_lru_h': nrm((DEC_BATCH, D_RNN), 0.5),
        'ln_g': 1.0 + nrm((DEPTH, 2, D_MODEL), 0.02),
        'ln_b': nrm((DEPTH, 2, D_MODEL), 0.02),
        'lru_w_in': nrm((nl, D_MODEL, 2 * D_RNN), dm),
        'lru_conv_w': nrm((nl, CONV_W, D_RNN), CONV_W ** -0.5),
        'lru_conv_b': nrm((nl, D_RNN), 0.01),
        'lru_w_a': nrm((nl, LRU_BLOCKS, LRU_BS, LRU_BS), LRU_BS ** -0.5),
        'lru_b_a': nrm((nl, D_RNN), 0.01),
        'lru_w_x': nrm((nl, LRU_BLOCKS, LRU_BS, LRU_BS), LRU_BS ** -0.5),
        'lru_b_x': nrm((nl, D_RNN), 0.01),
        'lru_lam': lru_lam,
        'lru_w_out': nrm((nl, D_RNN, D_MODEL), D_RNN ** -0.5 * BETA),
        'nsa_w_in': nrm((nn, D_MODEL, NSA_PROJ), dm),
        'nsa_w_out': nrm((nn, NSA_HEADS * NSA_HD, D_MODEL), (NSA_HEADS * NSA_HD) ** -0.5 * BETA),
        'ssd_w_in': nrm((ns, D_MODEL, SSD_PROJ), dm),
        'ssd_conv_w': nrm((ns, CONV_W, SSD_CONV_DIM), CONV_W ** -0.5),
        'ssd_conv_b': nrm((ns, SSD_CONV_DIM), 0.01),
        'ssd_dt_bias': ssd_dt_bias,
        'ssd_a_log': ssd_a_log,
        'ssd_d': 1.0 + nrm((ns, SSD_HEADS), 0.02),
        'ssd_norm_g': 1.0 + nrm((ns, SSD_INNER), 0.02),
        'ssd_w_out': nrm((ns, SSD_INNER, D_MODEL), SSD_INNER ** -0.5 * BETA),
        'ffn_w_in': nrm((nd, D_MODEL, 2 * D_FF), dm),
        'ffn_w_out': nrm((nd, D_FF, D_MODEL), D_FF ** -0.5 * BETA),
        'moe_router_w': nrm((nm, D_MODEL, N_EXPERTS), dm),
        'moe_router_b': nrm((nm, N_EXPERTS), 0.01),
        'moe_w_in': nrm((nm, N_EXPERTS, D_MODEL, 2 * D_FF_EXPERT), dm),
        'moe_w_out': nrm((nm, N_EXPERTS, D_FF_EXPERT, D_MODEL), D_FF_EXPERT ** -0.5 * BETA),
    }


def reference(x_prompt, x_sample, state_l0_lru_conv, state_l0_lru_h, cache_l1_nsa_kv, cache_l1_nsa_win,
              page_table, state_l2_ssd_conv, state_l2_ssd_ssm, state_l3_lru_conv, state_l3_lru_h,
              ln_g, ln_b, lru_w_in, lru_conv_w, lru_conv_b, lru_w_a, lru_b_a, lru_w_x, lru_b_x, lru_lam,
              lru_w_out, nsa_w_in, nsa_w_out, ssd_w_in, ssd_conv_w, ssd_conv_b, ssd_dt_bias, ssd_a_log,
              ssd_d, ssd_norm_g, ssd_w_out, ffn_w_in, ffn_w_out, moe_router_w, moe_router_b, moe_w_in,
              moe_w_out):
    lru_state = {0: (state_l0_lru_conv, state_l0_lru_h), 3: (state_l3_lru_conv, state_l3_lru_h)}
    nsa_cache = {1: (cache_l1_nsa_kv, cache_l1_nsa_win)}
    ssd_state = {2: (state_l2_ssd_conv, state_l2_ssd_ssm)}
    new = {}
    xp, xs = x_prompt, x_sample
    bp, bs = xp.shape[0], xs.shape[0]
    for i in range(DEPTH):
        kind, j = i % N_MIXERS, i // N_MIXERS
        if kind == 0:
            prm = (lru_w_in[j], lru_conv_w[j], lru_conv_b[j], lru_w_a[j], lru_b_a[j], lru_w_x[j], lru_b_x[j],
                   lru_lam[j], lru_w_out[j])
            conv_s, h_s = lru_state[i]
            mp, cp, hp = rglru_mixer(xp, jnp.zeros((bp, CONV_W - 1, D_RNN), xp.dtype),
                                     jnp.zeros((bp, D_RNN), xp.dtype), *prm)
            ms, cs, hs = rglru_mixer(xs, conv_s, h_s, *prm)
            new[i] = (cp, cs, hp, hs)
        elif kind == 1:
            pool, win = nsa_cache[i]
            n_pages = page_table.shape[1]
            past = pool[page_table].reshape((bs, n_pages * pool.shape[1]) + pool.shape[2:])
            mp, rp, wp = nsa_mixer(xp, None, None, 0, nsa_w_in[j], nsa_w_out[j])
            ms, rs, ws = nsa_mixer(xs, past, win, past.shape[1], nsa_w_in[j], nsa_w_out[j])
            new[i] = (rp, rs, wp, ws)
        else:
            prm = (ssd_w_in[j], ssd_conv_w[j], ssd_conv_b[j], ssd_dt_bias[j], ssd_a_log[j], ssd_d[j],
                   ssd_norm_g[j], ssd_w_out[j])
            conv_s, st_s = ssd_state[i]
            mp, cp, sp = ssd_mixer(xp, jnp.zeros((bp, CONV_W - 1, SSD_CONV_DIM), xp.dtype),
                                   jnp.zeros((bp, SSD_HEADS, SSD_HEAD_DIM, SSD_STATE), xp.dtype), *prm)
            ms, cs, ss = ssd_mixer(xs, conv_s, st_s, *prm)
            new[i] = (cp, cs, sp, ss)
        xp = layer_norm(ALPHA * xp + mp, ln_g[i, 0], ln_b[i, 0])
        xs = layer_norm(ALPHA * xs + ms, ln_g[i, 0], ln_b[i, 0])
        if i % 2 == 0:
            fp = swiglu(xp, ffn_w_in[i // 2], ffn_w_out[i // 2])
            fs = swiglu(xs, ffn_w_in[i // 2], ffn_w_out[i // 2])
        else:
            k = i // 2
            fp = moe_ffn(xp, moe_router_w[k], moe_router_b[k], moe_w_in[k], moe_w_out[k])
            fs = moe_ffn(xs, moe_router_w[k], moe_router_b[k], moe_w_in[k], moe_w_out[k])
        xp = layer_norm(ALPHA * xp + fp, ln_g[i, 1], ln_b[i, 1])
        xs = layer_norm(ALPHA * xs + fs, ln_g[i, 1], ln_b[i, 1])
    return (xp, xs,
            new[0][0], new[0][1], new[0][2], new[0][3],
            new[1][0], new[1][1], new[1][2], new[1][3],
            new[2][0], new[2][1], new[2][2], new[2][3],
            new[3][0], new[3][1], new[3][2], new[3][3])
```

```python
import functools
import math

import jax
import jax.numpy as jnp
from jax import lax
from jax.experimental import pallas as pl
from jax.experimental.pallas import tpu as pltpu

f32 = jnp.float32
bf16 = jnp.bfloat16

D_MODEL = 1024
DEPTH = 4
ALPHA = (2 * DEPTH) ** 0.25
LN_EPS = 1e-5
RMS_EPS = 1e-5
CONV_W = 4
NEG_INF = -1e30

D_RNN = 1344
LRU_BLOCKS = 16
LRU_BS = D_RNN // LRU_BLOCKS
LRU_C = 8.0

NSA_HEADS = 16
NSA_KV_HEADS = 4
NSA_HD = 64
NSA_GROUP = NSA_HEADS // NSA_KV_HEADS
CMP_BLOCK = 32
SEL_BLOCK = 64
SEL_TOPK = 16
WINDOW = 512
FORCE_SCORE = 1e4
ROPE_THETA = 10000.0
NSA_HQ = NSA_HEADS * NSA_HD
NSA_HK = NSA_KV_HEADS * NSA_HD

SSD_INNER = 2 * D_MODEL
SSD_HEAD_DIM = 64
SSD_HEADS = SSD_INNER // SSD_HEAD_DIM
SSD_GROUPS = 4
SSD_STATE = 128
SSD_CHUNK = 128
SSD_CONV_DIM = SSD_INNER + 2 * SSD_GROUPS * SSD_STATE

N_EXPERTS = 8
TOP_K = 2

VMEM_LIMIT_V7X = 56 * 1024 * 1024


def _cparams(sem):
    return pltpu.CompilerParams(dimension_semantics=sem, vmem_limit_bytes=VMEM_LIMIT_V7X)


def _const_spec(shape):
    nd = len(shape)
    return pl.BlockSpec(shape, lambda *_: (0,) * nd, pipeline_mode=pl.Buffered(1))


def _layer_norm(v, g, b):
    mu = jnp.mean(v, -1, keepdims=True)
    d = v - mu
    var = jnp.mean(d * d, -1, keepdims=True)
    return d * lax.rsqrt(var + LN_EPS) * g + b


def _proj_kernel(*refs, epilogues, n_aux):
    n_out = len(epilogues)
    x_ref = refs[0]
    w_refs = refs[1:1 + n_out]
    aux_refs = refs[1 + n_out:1 + n_out + n_aux]
    o_refs = refs[1 + n_out + n_aux:]
    x = x_ref[...].astype(bf16)
    aux = [a[...] for a in aux_refs]
    for w_ref, o_ref, ep in zip(w_refs, o_refs, epilogues):
        y = jnp.dot(x, w_ref[...], preferred_element_type=f32)
        if ep is not None:
            y = ep(y, *aux)
        o_ref[...] = y.astype(o_ref.dtype)


def _proj(x, ws, epilogues, aux=(), aux_period=(), tm=256, name="proj"):
    m, k = x.shape
    assert m % tm == 0
    in_specs = [pl.BlockSpec((tm, k), lambda i: (i, 0))]
    in_specs += [_const_spec(w.shape) for w in ws]
    for a, p in zip(aux, aux_period):
        if p:
            assert p % tm == 0 and a.shape[0] == p
            in_specs.append(pl.BlockSpec((tm, a.shape[1]), functools.partial(lambda i, n: (i % n, 0), n=p // tm)))
        else:
            in_specs.append(_const_spec(a.shape))
    out_shape = [jax.ShapeDtypeStruct((m, w.shape[1]), f32) for w in ws]
    out_specs = [pl.BlockSpec((tm, w.shape[1]), lambda i: (i, 0)) for w in ws]
    return pl.pallas_call(
        functools.partial(_proj_kernel, epilogues=tuple(epilogues), n_aux=len(aux)),
        grid=(m // tm,), in_specs=in_specs, out_specs=out_specs, out_shape=out_shape,
        compiler_params=_cparams(("parallel",)), name=name,
    )(x, *ws, *aux)


def _outproj_ln_kernel(a_ref, w_ref, x_ref, g_ref, b_ref, o_ref):
    y = jnp.dot(a_ref[...].astype(bf16), w_ref[...], preferred_element_type=f32)
    o_ref[...] = _layer_norm(ALPHA * x_ref[...] + y, g_ref[...], b_ref[...])


def _outproj_ln(a, w, x, g, b, tm=256, name="outproj_ln"):
    m, k = a.shape
    d = w.shape[1]
    assert m % tm == 0
    return pl.pallas_call(
        _outproj_ln_kernel, grid=(m // tm,),
        in_specs=[pl.BlockSpec((tm, k), lambda i: (i, 0)), _const_spec(w.shape),
                  pl.BlockSpec((tm, d), lambda i: (i, 0)), _const_spec((1, d)), _const_spec((1, d))],
        out_specs=pl.BlockSpec((tm, d), lambda i: (i, 0)),
        out_shape=jax.ShapeDtypeStruct((m, d), f32),
        compiler_params=_cparams(("parallel",)), name=name,
    )(a, w, x, g.reshape(1, d), b.reshape(1, d))


def _ffn_ln_kernel(x_ref, wg_ref, wu_ref, wo_ref, g_ref, b_ref, o_ref, acc_ref):
    j = pl.program_id(1)
    x = x_ref[...]
    xb = x.astype(bf16)
    hg = jnp.dot(xb, wg_ref[...].astype(bf16), preferred_element_type=f32)
    hu = jnp.dot(xb, wu_ref[...].astype(bf16), preferred_element_type=f32)
    act = (hg * jax.nn.sigmoid(hg) * hu).astype(bf16)
    part = jnp.dot(act, wo_ref[...].astype(bf16), preferred_element_type=f32)

    @pl.when(j == 0)
    def _():
        acc_ref[...] = part

    @pl.when(j > 0)
    def _():
        acc_ref[...] += part

    @pl.when(j == pl.num_programs(1) - 1)
    def _():
        o_ref[...] = _layer_norm(ALPHA * x + acc_ref[...], g_ref[...], b_ref[...])


def _ffn_ln(x, w_in, w_out, g, b, tm, tf, name="ffn_ln"):
    m, d = x.shape
    f = w_out.shape[0]
    assert m % tm == 0 and f % tf == 0
    nf = f // tf
    return pl.pallas_call(
        _ffn_ln_kernel, grid=(m // tm, nf),
        in_specs=[pl.BlockSpec((tm, d), lambda i, j: (i, 0)),
                  pl.BlockSpec((d, tf), lambda i, j: (0, j)),
                  pl.BlockSpec((d, tf), lambda i, j: (0, j + nf)),
                  pl.BlockSpec((tf, d), lambda i, j: (j, 0)),
                  _const_spec((1, d)), _const_spec((1, d))],
        out_specs=pl.BlockSpec((tm, d), lambda i, j: (i, 0)),
        out_shape=jax.ShapeDtypeStruct((m, d), f32),
        scratch_shapes=[pltpu.VMEM((tm, d), f32)],
        compiler_params=_cparams(("parallel", "arbitrary")), name=name,
    )(x, w_in, w_in, w_out, g.reshape(1, d), b.reshape(1, d))


def _router_kernel(x_ref, w_ref, b_ref, c_ref):
    logits = jnp.dot(x_ref[...], w_ref[...], precision=lax.Precision.HIGHEST,
                     preferred_element_type=f32) + b_ref[...]
    e_idx = lax.broadcasted_iota(jnp.int32, logits.shape, 1)
    v1 = jnp.max(logits, -1, keepdims=True)
    i1 = jnp.min(jnp.where(logits == v1, e_idx, N_EXPERTS), -1, keepdims=True)
    rest = jnp.where(e_idx == i1, -jnp.inf, logits)
    v2 = jnp.max(rest, -1, keepdims=True)
    i2 = jnp.min(jnp.where(rest == v2, e_idx, N_EXPERTS), -1, keepdims=True)
    e2 = jnp.exp(v2 - v1)
    den = 1.0 + e2
    c_ref[...] = jnp.where(e_idx == i1, 1.0 / den, 0.0) + jnp.where(e_idx == i2, e2 / den, 0.0)


def _router(x, w, b, tm=512):
    m, d = x.shape
    assert m % tm == 0
    return pl.pallas_call(
        _router_kernel, grid=(m // tm,),
        in_specs=[pl.BlockSpec((tm, d), lambda i: (i, 0)), _const_spec(w.shape),
                  _const_spec((1, N_EXPERTS))],
        out_specs=pl.BlockSpec((tm, N_EXPERTS), lambda i: (i, 0)),
        out_shape=jax.ShapeDtypeStruct((m, N_EXPERTS), f32),
        compiler_params=_cparams(("parallel",)), name="router",
    )(x, w, b.reshape(1, N_EXPERTS))


def _moe_ln_kernel(x_ref, c_ref, wg_ref, wu_ref, wo_ref, g_ref, b_ref, o_ref, acc_ref):
    e = pl.program_id(1)
    j = pl.program_id(2)
    x = x_ref[...]
    xb = x.astype(bf16)
    c = c_ref[...]
    ce = jnp.sum(jnp.where(lax.broadcasted_iota(jnp.int32, c.shape, 1) == e, c, 0.0), -1, keepdims=True)
    hg = jnp.dot(xb, wg_ref[...].astype(bf16), preferred_element_type=f32)
    hu = jnp.dot(xb, wu_ref[...].astype(bf16), preferred_element_type=f32)
    act = (hg * jax.nn.sigmoid(hg) * hu).astype(bf16)
    part = ce * jnp.dot(act, wo_ref[...].astype(bf16), preferred_element_type=f32)
    first = jnp.logical_and(e == 0, j == 0)

    @pl.when(first)
    def _():
        acc_ref[...] = part

    @pl.when(jnp.logical_not(first))
    def _():
        acc_ref[...] += part

    @pl.when(jnp.logical_and(e == pl.num_programs(1) - 1, j == pl.num_programs(2) - 1))
    def _():
        o_ref[...] = _layer_norm(ALPHA * x + acc_ref[...], g_ref[...], b_ref[...])


def _moe_ln(x, comb, w_in, w_out, g, b, tm, tf):
    m, d = x.shape
    ne, f, _ = w_out.shape
    assert m % tm == 0 and f % tf == 0
    nf = f // tf
    return pl.pallas_call(
        _moe_ln_kernel, grid=(m // tm, ne, nf),
        in_specs=[pl.BlockSpec((tm, d), lambda i, e, j: (i, 0)),
                  pl.BlockSpec((tm, ne), lambda i, e, j: (i, 0)),
                  pl.BlockSpec((None, d, tf), lambda i, e, j: (e, 0, j)),
                  pl.BlockSpec((None, d, tf), lambda i, e, j: (e, 0, j + nf)),
                  pl.BlockSpec((None, tf, d), lambda i, e, j: (e, j, 0)),
                  _const_spec((1, d)), _const_spec((1, d))],
        out_specs=pl.BlockSpec((tm, d), lambda i, e, j: (i, 0)),
        out_shape=jax.ShapeDtypeStruct((m, d), f32),
        scratch_shapes=[pltpu.VMEM((tm, d), f32)],
        compiler_params=_cparams(("parallel", "arbitrary", "arbitrary")), name="moe_ln",
    )(x, comb, w_in, w_in, w_out, g.reshape(1, d), b.reshape(1, d))


def _softplus(x):
    return jnp.maximum(x, 0.0) + jnp.log(1.0 + jnp.exp(-jnp.abs(x)))


def _lru_gates(xc, wa, ba, wx, bx, sp):
    xcb = xc.astype(bf16)
    r = jax.nn.sigmoid(jnp.dot(xcb, wa, preferred_element_type=f32) + ba)
    gi = jax.nn.sigmoid(jnp.dot(xcb, wx, preferred_element_type=f32) + bx)
    log_a = sp * r
    th = jnp.tanh(log_a)
    one_minus_a2 = -2.0 * th / (1.0 - th)
    return jnp.exp(log_a), jnp.sqrt(one_minus_a2) * gi * xc


def _lru_core_kernel(xb_ref, g_ref, cw_ref, cb_ref, wa_ref, ba_ref, wx_ref, bx_ref, lam_ref,
                     y_ref, hlast_ref, ext_ref, a_ref, b_ref, h_ref, *, ts, nb):
    i = pl.program_id(0)
    hist = 8

    @pl.when(i == 0)
    def _():
        ext_ref[:, 0:hist, :] = jnp.zeros((nb, hist, ext_ref.shape[2]), f32)
        h_ref[...] = jnp.zeros_like(h_ref)

    ext_ref[:, hist:hist + ts, :] = xb_ref[...]
    sp = -LRU_C * _softplus(-lam_ref[...])
    cw = cw_ref[...]
    for b in range(nb):
        xc = cb_ref[...]
        for k in range(CONV_W):
            off = hist - (CONV_W - 1) + k
            xc = xc + cw[k:k + 1, :] * ext_ref[b, off:off + ts, :]
        a, bt = _lru_gates(xc, wa_ref[...], ba_ref[...], wx_ref[...], bx_ref[...], sp)
        a_ref[b] = a
        b_ref[b] = bt
    ext_ref[:, 0:hist, :] = ext_ref[:, ts:ts + hist, :]

    def step(t, h):
        h = a_ref[:, t, :] * h + b_ref[:, t, :]
        b_ref[:, t, :] = h
        return h

    h = lax.fori_loop(0, ts, step, h_ref[...], unroll=8)
    h_ref[...] = h
    y_ref[...] = b_ref[...] * g_ref[...]

    @pl.when(i == pl.num_programs(0) - 1)
    def _():
        hlast_ref[...] = h


def _lru_core(xb, g, cw, cb, wa, ba, wx, bx, lam, ts=64):
    nb, s, c = xb.shape
    assert s % ts == 0 and ts % 8 == 0
    blk = pl.BlockSpec((nb, ts, c), lambda i: (0, i, 0))
    row = _const_spec((1, c))
    return pl.pallas_call(
        functools.partial(_lru_core_kernel, ts=ts, nb=nb), grid=(s // ts,),
        in_specs=[blk, blk, _const_spec((CONV_W, c)), row, _const_spec((c, c)), row,
                  _const_spec((c, c)), row, row],
        out_specs=[blk, pl.BlockSpec((nb, c), lambda i: (0, 0))],
        out_shape=[jax.ShapeDtypeStruct((nb, s, c), f32), jax.ShapeDtypeStruct((nb, c), f32)],
        scratch_shapes=[pltpu.VMEM((nb, ts + 8, c), f32), pltpu.VMEM((nb, ts, c), f32),
                        pltpu.VMEM((nb, ts, c), f32), pltpu.VMEM((nb, c), f32)],
        compiler_params=_cparams(("arbitrary",)), name="lru_core",
    )(xb, g, cw, cb.reshape(1, c), wa, ba.reshape(1, c), wx, bx.reshape(1, c), lam.reshape(1, c))


def _lru_short_kernel(xb_ref, g_ref, cs_ref, h0_ref, cw_ref, cb_ref, wa_ref, ba_ref, wx_ref, bx_ref,
                      lam_ref, y_ref, hlast_ref, *, s):
    rows = [cs_ref[k] for k in range(CONV_W - 1)] + [xb_ref[t] for t in range(s)]
    sp = -LRU_C * _softplus(-lam_ref[...])
    cw = cw_ref[...]
    h = h0_ref[...]
    for t in range(s):
        xc = cb_ref[...]
        for k in range(CONV_W):
            xc = xc + cw[k:k + 1, :] * rows[t + k]
        a, bt = _lru_gates(xc, wa_ref[...], ba_ref[...], wx_ref[...], bx_ref[...], sp)
        h = a * h + bt
        y_ref[t] = h * g_ref[t]
    hlast_ref[...] = h


def _lru_short(xb, g, cs, h0, cw, cb, wa, ba, wx, bx, lam):
    s, nb, c = xb.shape
    return pl.pallas_call(
        functools.partial(_lru_short_kernel, s=s),
        out_shape=[jax.ShapeDtypeStruct((s, nb, c), f32), jax.ShapeDtypeStruct((nb, c), f32)],
        compiler_params=pltpu.CompilerParams(vmem_limit_bytes=VMEM_LIMIT_V7X), name="lru_short",
    )(xb, g, cs, h0, cw, cb.reshape(1, c), wa, ba.reshape(1, c), wx, bx.reshape(1, c), lam.reshape(1, c))


def _block_diag(w):
    n, k, _ = w.shape
    eye = jnp.eye(n, dtype=w.dtype)
    return (eye[:, None, :, None] * w[:, :, None, :]).reshape(n * k, n * k)


def _gelu_ep(y):
    return jax.nn.gelu(y)


def _lru_layer(xp, xs, nbp, nbs, conv_s, h_s, ln_g, ln_b, w_in, cw, cb, w_a, b_a, w_x, b_x, lam, w_out):
    c = D_RNN
    w_gate = w_in[:, :c].astype(bf16)
    w_xb = w_in[:, c:].astype(bf16)
    wa = _block_diag(w_a).astype(bf16)
    wx = _block_diag(w_x).astype(bf16)
    wo = w_out.astype(bf16)
    sp_len = xp.shape[0] // nbp
    ss_len = xs.shape[0] // nbs

    gp, xbp = _proj(xp, [w_gate, w_xb], [_gelu_ep, None], tm=256, name="lru_in")
    gs, xbs = _proj(xs, [w_gate, w_xb], [_gelu_ep, None], tm=xs.shape[0], name="lru_in_s")
    xbp3 = xbp.reshape(nbp, sp_len, c)
    yp, hp = _lru_core(xbp3, gp.reshape(nbp, sp_len, c), cw, cb, wa, b_a, wx, b_x, lam)
    xbs3 = xbs.reshape(nbs, ss_len, c)
    ys_t, hs = _lru_short(xbs3.transpose(1, 0, 2), gs.reshape(nbs, ss_len, c).transpose(1, 0, 2),
                          conv_s.transpose(1, 0, 2), h_s, cw, cb, wa, b_a, wx, b_x, lam)
    ys = ys_t.transpose(1, 0, 2).reshape(nbs * ss_len, c)
    xp = _outproj_ln(yp.reshape(nbp * sp_len, c), wo, xp, ln_g, ln_b, name="lru_out")
    xs = _outproj_ln(ys, wo, xs, ln_g, ln_b, tm=xs.shape[0], name="lru_out_s")
    conv_p = xbp3[:, sp_len - (CONV_W - 1):]
    conv_sn = jnp.concatenate([conv_s, xbs3], axis=1)[:, ss_len:]
    return xp, xs, (conv_p, conv_sn, hp, hs)


SSD_GN = SSD_GROUPS * SSD_STATE
SSD_HPG = SSD_HEADS // SSD_GROUPS
SSD_DT_PAD = 128


def _ssd_core_kernel(*refs, q, valid, zero_init):
    if zero_init:
        (xbc_ref, dt_ref, zs_ref, cw_ref, cb_ref, alog_ref, dsk_ref, ng_ref,
         y_ref, sfin_ref, ext_ref, st_ref, yacc_ref) = refs
    else:
        (xbc_ref, dt_ref, zs_ref, hist_ref, s0_ref, cw_ref, cb_ref, alog_ref, dsk_ref, ng_ref,
         y_ref, sfin_ref, ext_ref, st_ref, yacc_ref) = refs
    ci = pl.program_id(1)
    hist = 8

    @pl.when(ci == 0)
    def _():
        if zero_init:
            ext_ref[0:hist, :] = jnp.zeros((hist, ext_ref.shape[1]), f32)
            st_ref[...] = jnp.zeros_like(st_ref)
        else:
            ext_ref[0:hist, :] = hist_ref[...]
            st_ref[...] = s0_ref[...]

    ext_ref[hist:hist + q, :] = xbc_ref[...]
    cw = cw_ref[...]
    xc = cb_ref[...]
    for k in range(CONV_W):
        off = hist - (CONV_W - 1) + k
        xc = xc + cw[k:k + 1, :] * ext_ref[off:off + q, :]
    xc = xc * jax.nn.sigmoid(xc)
    if q >= hist:
        ext_ref[0:hist, :] = ext_ref[q:q + hist, :]

    row = lax.broadcasted_iota(jnp.int32, (q, q), 0)
    col = lax.broadcasted_iota(jnp.int32, (q, q), 1)
    causal = col <= row
    dt = dt_ref[...]
    if valid < q:
        dt = jnp.where(lax.broadcasted_iota(jnp.int32, dt.shape, 0) < valid, dt, 0.0)
    a_neg = -jnp.exp(alog_ref[...])
    tril = jnp.where(causal, 1.0, 0.0)
    acum = jnp.dot(tril, dt * a_neg, precision=lax.Precision.HIGHEST, preferred_element_type=f32)
    acum_t = acum.T
    last = acum[q - 1:q, :]
    dec_out = jnp.exp(last - acum)
    dec_chunk_t = jnp.exp(acum_t[:, q - 1:q])
    e_acum = jnp.exp(acum)

    for g in range(SSD_GROUPS):
        bm = xc[:, SSD_INNER + g * SSD_STATE:SSD_INNER + (g + 1) * SSD_STATE].astype(bf16)
        cm = xc[:, SSD_INNER + SSD_GN + g * SSD_STATE:SSD_INNER + SSD_GN + (g + 1) * SSD_STATE].astype(bf16)
        cb = lax.dot_general(cm, bm, (((1,), (1,)), ((), ())), preferred_element_type=f32)
        for hh in range(SSD_HPG):
            h = g * SSD_HPG + hh
            xh = xc[:, h * SSD_HEAD_DIM:(h + 1) * SSD_HEAD_DIM]
            xdt = xh * dt[:, h:h + 1]
            seg = acum[:, h:h + 1] - acum_t[h:h + 1, :]
            decay = jnp.exp(jnp.where(causal, seg, NEG_INF))
            y_diag = jnp.dot((cb * decay).astype(bf16), xdt.astype(bf16), preferred_element_type=f32)
            s_prev = st_ref[h]
            y_off = lax.dot_general(cm, s_prev.astype(bf16), (((1,), (1,)), ((), ())),
                                    preferred_element_type=f32) * e_acum[:, h:h + 1]
            s_new = lax.dot_general((xdt * dec_out[:, h:h + 1]).astype(bf16), bm,
                                    (((0,), (0,)), ((), ())), preferred_element_type=f32)
            st_ref[h] = dec_chunk_t[h:h + 1, :] * s_prev + s_new
            yacc_ref[:, h * SSD_HEAD_DIM:(h + 1) * SSD_HEAD_DIM] = y_diag + y_off

    y = yacc_ref[...] + dsk_ref[...] * xc[:, :SSD_INNER]
    yg = y * zs_ref[...]
    gw = SSD_INNER // SSD_GROUPS
    for g in range(SSD_GROUPS):
        v = yg[:, g * gw:(g + 1) * gw]
        v = v * lax.rsqrt(jnp.mean(v * v, -1, keepdims=True) + RMS_EPS)
        y_ref[:, g * gw:(g + 1) * gw] = v * ng_ref[:, g * gw:(g + 1) * gw]

    @pl.when(ci == pl.num_programs(1) - 1)
    def _():
        sfin_ref[...] = st_ref[...]


def _ssd_core(xbc, dt, zs, hist, s0, cw, cb, a_log, d_skip, norm_g, q, valid):
    nb, s, cd = xbc.shape
    assert s % q == 0
    zero_init = s0 is None
    blk = lambda w: pl.BlockSpec((None, q, w), lambda b, c: (b, c, 0))
    st_spec = pl.BlockSpec((None, SSD_HEADS, SSD_HEAD_DIM, SSD_STATE), lambda b, c: (b, 0, 0, 0))
    in_specs = [blk(cd), blk(SSD_DT_PAD), blk(SSD_INNER)]
    args = [xbc, dt, zs]
    if not zero_init:
        in_specs += [pl.BlockSpec((None, 8, cd), lambda b, c: (b, 0, 0)), st_spec]
        args += [hist, s0]
    in_specs += [_const_spec((CONV_W, cd)), _const_spec((1, cd)), _const_spec((1, SSD_DT_PAD)),
                 _const_spec((1, SSD_INNER)), _const_spec((1, SSD_INNER))]
    pad_h = SSD_DT_PAD - SSD_HEADS
    args += [cw, cb.reshape(1, cd), jnp.pad(a_log, (0, pad_h)).reshape(1, SSD_DT_PAD),
             jnp.repeat(d_skip, SSD_HEAD_DIM).reshape(1, SSD_INNER), norm_g.reshape(1, SSD_INNER)]
    return pl.pallas_call(
        functools.partial(_ssd_core_kernel, q=q, valid=valid, zero_init=zero_init),
        grid=(nb, s // q), in_specs=in_specs,
        out_specs=[blk(SSD_INNER), st_spec],
        out_shape=[jax.ShapeDtypeStruct((nb, s, SSD_INNER), f32),
                   jax.ShapeDtypeStruct((nb, SSD_HEADS, SSD_HEAD_DIM, SSD_STATE), f32)],
        scratch_shapes=[pltpu.VMEM((q + 8, cd), f32), pltpu.VMEM((SSD_HEADS, SSD_HEAD_DIM, SSD_STATE), f32),
                        pltpu.VMEM((q, SSD_INNER), f32)],
        compiler_params=_cparams(("parallel", "arbitrary")), name="ssd_core",
    )(*args)


def _silu_ep(y, *_):
    return y * jax.nn.sigmoid(y)


def _dt_ep(y, bias):
    return _softplus(y + bias)


def _pass_ep(y, *_):
    return y


def _ssd_layer(xp, xs, nbp, nbs, conv_s, st_s, ln_g, ln_b, w_in, cw, cb, dt_bias, a_log, d_skip, norm_g, w_out):
    cd = SSD_CONV_DIM
    pad_h = SSD_DT_PAD - SSD_HEADS
    w_z = w_in[:, :SSD_INNER].astype(bf16)
    w_xbc = w_in[:, SSD_INNER:SSD_INNER + cd].astype(bf16)
    w_dt = jnp.pad(w_in[:, SSD_INNER + cd:], ((0, 0), (0, pad_h))).astype(bf16)
    bias = jnp.pad(dt_bias, (0, pad_h)).reshape(1, SSD_DT_PAD)
    wo = w_out.astype(bf16)
    sp_len = xp.shape[0] // nbp
    ss_len = xs.shape[0] // nbs
    eps = [_silu_ep, _pass_ep, _dt_ep]

    zsp, xbcp, dtp = _proj(xp, [w_z, w_xbc, w_dt], eps, aux=(bias,), aux_period=(0,), tm=256, name="ssd_in")
    zss, xbcs, dts = _proj(xs, [w_z, w_xbc, w_dt], eps, aux=(bias,), aux_period=(0,), tm=xs.shape[0], name="ssd_in_s")
    xbcp3 = xbcp.reshape(nbp, sp_len, cd)
    yp, sp_fin = _ssd_core(xbcp3, dtp.reshape(nbp, sp_len, SSD_DT_PAD), zsp.reshape(nbp, sp_len, SSD_INNER),
                           None, None, cw, cb, a_log, d_skip, norm_g, q=math.gcd(sp_len, SSD_CHUNK), valid=SSD_CHUNK)
    qs = 8
    assert ss_len <= qs
    tpad = lambda a: jnp.pad(a, ((0, 0), (0, qs - ss_len), (0, 0)))
    xbcs3 = xbcs.reshape(nbs, ss_len, cd)
    hist = jnp.pad(conv_s, ((0, 0), (8 - (CONV_W - 1), 0), (0, 0)))
    ys, ss_fin = _ssd_core(tpad(xbcs3), tpad(dts.reshape(nbs, ss_len, SSD_DT_PAD)),
                           tpad(zss.reshape(nbs, ss_len, SSD_INNER)), hist, st_s,
                           cw, cb, a_log, d_skip, norm_g, q=qs, valid=ss_len)
    ys = ys[:, :ss_len].reshape(nbs * ss_len, SSD_INNER)
    xp = _outproj_ln(yp.reshape(nbp * sp_len, SSD_INNER), wo, xp, ln_g, ln_b, name="ssd_out")
    xs = _outproj_ln(ys, wo, xs, ln_g, ln_b, tm=xs.shape[0], name="ssd_out_s")
    conv_p = xbcp3[:, sp_len - (CONV_W - 1):]
    conv_sn = jnp.concatenate([conv_s, xbcs3], axis=1)[:, ss_len:]
    return xp, xs, (conv_p, conv_sn, sp_fin, ss_fin)


NSA_TILE = 128
NSA_RATIO = SEL_BLOCK // CMP_BLOCK
NSA_SCALE = NSA_HD ** -0.5


def _rope_tables(pos):
    inv = ROPE_THETA ** (-jnp.arange(0, NSA_HD, 2, dtype=f32) / NSA_HD)
    ang = pos[:, None] * inv[None, :]
    cos, sin = jnp.cos(ang), jnp.sin(ang)
    return jnp.tile(cos, (1, 4)), jnp.tile(jnp.concatenate([-sin, sin], axis=1), (1, 2))


def _rope_cols(y, cos, sgn_sin, blocks):
    lane = lax.broadcasted_iota(jnp.int32, (y.shape[0], 128), 1)
    first_half = (lane % NSA_HD) < NSA_HD // 2
    out = []
    for c in range(y.shape[1] // 128):
        blk = y[:, 128 * c:128 * (c + 1)]
        if c in blocks:
            partner = jnp.where(first_half, pltpu.roll(blk, 128 - NSA_HD // 2, 1), pltpu.roll(blk, NSA_HD // 2, 1))
            blk = blk * cos + partner * sgn_sin
        out.append(blk)
    return jnp.concatenate(out, axis=1)


def _q_ep(y, cos, sin):
    return _rope_cols(y, cos, sin, range(NSA_HQ // 128)) * NSA_SCALE


def _rows_ep(y, cos, sin):
    return _rope_cols(y, cos, sin, (4, 5))


def _wrows_ep(y, cos, sin):
    return _rope_cols(y, cos, sin, (0, 1))


def _sigmoid_ep(y, *_):
    return jax.nn.sigmoid(y)


def _nsa_cmp_kernel(pt_ref, pa_ref, pb_ref, cos_ref, sin_ref, o_ref):
    del pt_ref
    per_page = NSA_TILE // CMP_BLOCK
    for n, ref in enumerate((pa_ref, pb_ref)):
        m = ref[...].reshape(per_page, CMP_BLOCK, ref.shape[1]).sum(1) * (1.0 / CMP_BLOCK)
        lo, hi = n * per_page, (n + 1) * per_page
        o_ref[lo:hi, :] = _rope_cols(m, cos_ref[lo:hi, :], sin_ref[lo:hi, :], (0, 1))


def _nsa_cmp(pool, page_table, cos, sin):
    nb, n_pages = page_table.shape
    assert n_pages % 2 == 0
    w = 2 * NSA_HK
    per_step = 2 * NSA_TILE // CMP_BLOCK
    grid_spec = pltpu.PrefetchScalarGridSpec(
        num_scalar_prefetch=1, grid=(nb, n_pages // 2),
        in_specs=[pl.BlockSpec((None, NSA_TILE, w), lambda b, p, pt: (pt[b, 2 * p], 0, 0)),
                  pl.BlockSpec((None, NSA_TILE, w), lambda b, p, pt: (pt[b, 2 * p + 1], 0, 0)),
                  pl.BlockSpec((per_step, 128), lambda b, p, pt: (p, 0)),
                  pl.BlockSpec((per_step, 128), lambda b, p, pt: (p, 0))],
        out_specs=pl.BlockSpec((None, per_step, w), lambda b, p, pt: (b, p, 0)))
    return pl.pallas_call(
        _nsa_cmp_kernel, grid_spec=grid_spec,
        out_shape=jax.ShapeDtypeStruct((nb, n_pages * NSA_TILE // CMP_BLOCK, w), f32),
        compiler_params=_cparams(("parallel", "arbitrary")), name="nsa_cmp",
    )(page_table, pool, pool, cos, sin)


def _even_odd(cmp):
    nb, nc, w = cmp.shape
    return cmp.reshape(nb, nc // 2, 2, w).transpose(0, 2, 1, 3).reshape(nb, nc, w)


def _topk_mask_rows(score_t, k):
    n = score_t.shape[0]
    assert n % 8 == 0
    pieces = [score_t[8 * v:8 * v + 8] for v in range(n // 8)]
    ridx = lax.broadcasted_iota(jnp.int32, pieces[0].shape, 0)
    cnts = [jnp.zeros(pieces[0].shape, f32) for _ in pieces]
    for i in range(n):
        si = score_t[i:i + 1, :]
        for v, pc in enumerate(pieces):
            if 8 * v > i:
                cnts[v] = cnts[v] + jnp.where(si >= pc, 1.0, 0.0)
            elif 8 * v + 7 < i:
                cnts[v] = cnts[v] + jnp.where(si > pc, 1.0, 0.0)
            else:
                tie_wins = jnp.where(ridx + 8 * v > i, 1.0, 0.0)
                cnts[v] = cnts[v] + jnp.where(si > pc, 1.0, 0.0) + jnp.where(si == pc, tie_wins, 0.0)
    return jnp.concatenate([jnp.where(c < k, 1.0, 0.0) for c in cnts], axis=0)


def _topk_mask_lanes(score, k, n_real):
    lane = lax.broadcasted_iota(jnp.int32, score.shape, 1)
    cnt = jnp.zeros(score.shape, f32)
    for i in range(n_real):
        si = score[:, i:i + 1]
        cnt = cnt + jnp.where(si > score, 1.0, 0.0) + jnp.where(si == score, jnp.where(lane > i, 1.0, 0.0), 0.0)
    return jnp.where(cnt < k, 1.0, 0.0)


def _softmax_rows(s):
    e = jnp.exp(s - jnp.max(s, -1, keepdims=True))
    return e / jnp.sum(e, -1, keepdims=True)


def _flash_step(qk, kt, vt, ok, carry):
    m, l, acc = carry
    s = lax.dot_general(qk, kt, (((1,), (1,)), ((), ())), preferred_element_type=f32)
    s = jnp.where(ok, s, NEG_INF)
    m_new = jnp.maximum(m, jnp.max(s, -1, keepdims=True))
    alpha = jnp.exp(m - m_new)
    p = jnp.exp(s - m_new)
    l = alpha * l + jnp.sum(p, -1, keepdims=True)
    acc = alpha * acc + jnp.dot(p.astype(bf16), vt, preferred_element_type=f32)
    return m_new, l, acc


def _flash_init(rows):
    return (jnp.full((rows, 1), NEG_INF, f32), jnp.zeros((rows, 1), f32), jnp.zeros((rows, NSA_HD), f32))


def _nsa_prompt_kernel(q_ref, g_ref, cmp_ref, sel_ref, win_ref, o_ref, selm_ref, *, s_len):
    i = pl.program_id(1)
    qb = NSA_TILE
    nc = s_len // CMP_BLOCK
    nsb = s_len // SEL_BLOCK
    rows = NSA_GROUP * qb
    t_col = i * qb + lax.broadcasted_iota(jnp.int32, (qb, 1), 0)
    t_rows = jnp.concatenate([t_col] * NSA_GROUP, axis=0)
    q = q_ref[...]
    gates = g_ref[...]
    cl = lax.broadcasted_iota(jnp.int32, (1, nc), 1)
    cblk = jnp.where(cl < nc // 2, 2 * cl, 2 * (cl - nc // 2) + 1)
    c_last = (cblk + 1) * CMP_BLOCK - 1
    blk = lax.broadcasted_iota(jnp.int32, (qb, nsb), 1)
    valid = blk * SEL_BLOCK <= t_col
    forced = (blk == 0) | (blk == t_col // SEL_BLOCK)
    expand = jnp.where(lax.broadcasted_iota(jnp.int32, (nsb, s_len), 0)
                       == lax.broadcasted_iota(jnp.int32, (nsb, s_len), 1) // SEL_BLOCK, 1.0, 0.0).astype(bf16)
    kpos0 = lax.broadcasted_iota(jnp.int32, (1, NSA_TILE), 1)

    for k in range(NSA_KV_HEADS):
        qk = jnp.concatenate([q[:, NSA_HD * (NSA_GROUP * k + g):NSA_HD * (NSA_GROUP * k + g + 1)]
                              for g in range(NSA_GROUP)], axis=0).astype(bf16)
        kcol = slice(NSA_HD * k, NSA_HD * (k + 1))
        vcol = slice(NSA_HK + NSA_HD * k, NSA_HK + NSA_HD * (k + 1))

        cmask = c_last <= t_rows
        s_c = lax.dot_general(qk, cmp_ref[:, kcol].astype(bf16), (((1,), (1,)), ((), ())),
                              preferred_element_type=f32)
        p_c = _softmax_rows(jnp.where(cmask, s_c, NEG_INF))
        p_c = p_c * jnp.where(t_rows >= CMP_BLOCK - 1, 1.0, 0.0)
        o_c = jnp.dot(p_c.astype(bf16), cmp_ref[:, vcol].astype(bf16), preferred_element_type=f32)

        imp = p_c[0:qb]
        for g in range(1, NSA_GROUP):
            imp = imp + p_c[g * qb:(g + 1) * qb]
        imp = imp[:, :nc // 2] + imp[:, nc // 2:]
        score = jnp.where(forced, FORCE_SCORE, jnp.where(valid, imp, -1.0))
        sel = _topk_mask_rows(score.T, SEL_TOPK).T
        selm_ref[...] = jnp.dot(sel.astype(bf16), expand, preferred_element_type=f32)

        def sel_body(j, carry):
            off = pl.multiple_of(j * NSA_TILE, NSA_TILE)
            kt = sel_ref[pl.ds(off, NSA_TILE), kcol].astype(bf16)
            vt = sel_ref[pl.ds(off, NSA_TILE), vcol].astype(bf16)
            sm = selm_ref[:, pl.ds(off, NSA_TILE)]
            ok = (jnp.concatenate([sm] * NSA_GROUP, axis=0) > 0.5) & (off + kpos0 <= t_rows)
            return _flash_step(qk, kt, vt, ok, carry)

        _, l_s, acc_s = lax.fori_loop(0, i + 1, sel_body, _flash_init(rows))
        o_s = acc_s / l_s

        def win_body(j, carry):
            off = pl.multiple_of(j * NSA_TILE, NSA_TILE)
            kt = win_ref[pl.ds(off, NSA_TILE), kcol].astype(bf16)
            vt = win_ref[pl.ds(off, NSA_TILE), vcol].astype(bf16)
            kpos = off + kpos0
            ok = (kpos <= t_rows) & (kpos > t_rows - WINDOW)
            return _flash_step(qk, kt, vt, ok, carry)

        _, l_w, acc_w = lax.fori_loop(jnp.maximum(i - WINDOW // NSA_TILE, 0), i + 1, win_body, _flash_init(rows))
        o_w = acc_w / l_w

        for g in range(NSA_GROUP):
            h = NSA_GROUP * k + g
            r = slice(g * qb, (g + 1) * qb)
            o_ref[:, NSA_HD * h:NSA_HD * (h + 1)] = (gates[:, 3 * h:3 * h + 1] * o_c[r]
                                                     + gates[:, 3 * h + 1:3 * h + 2] * o_s[r]
                                                     + gates[:, 3 * h + 2:3 * h + 3] * o_w[r])


def _nsa_prompt(q, gates, cmp, rows, wrows):
    nb, s_len, _ = q.shape
    assert s_len % NSA_TILE == 0 and (s_len // CMP_BLOCK) % 2 == 0
    w = 2 * NSA_HK
    qspec = lambda width: pl.BlockSpec((None, NSA_TILE, width), lambda b, i: (b, i, 0))
    return pl.pallas_call(
        functools.partial(_nsa_prompt_kernel, s_len=s_len), grid=(nb, s_len // NSA_TILE),
        in_specs=[qspec(NSA_HQ), qspec(3 * NSA_HEADS),
                  pl.BlockSpec((None, s_len // CMP_BLOCK, w), lambda b, i: (b, 0, 0)),
                  pl.BlockSpec((None, s_len, w), lambda b, i: (b, 0, 1)),
                  pl.BlockSpec((None, s_len, w), lambda b, i: (b, 0, 0))],
        out_specs=qspec(NSA_HQ),
        out_shape=jax.ShapeDtypeStruct((nb, s_len, NSA_HQ), f32),
        scratch_shapes=[pltpu.VMEM((NSA_TILE, s_len), f32)],
        compiler_params=_cparams(("parallel", "arbitrary")), name="nsa_prompt",
    )(q, gates, cmp, rows, wrows)


def _nsa_decode_kernel(pt_ref, q_ref, g_ref, cmp_ref, page_ref, new_ref, win_ref, wnew_ref, o_ref,
                       sel_sc, m_sc, l_sc, acc_sc, oc_sc, *, past_len, n_new):
    del pt_ref
    p = pl.program_id(1)
    rows = NSA_GROUP * n_new
    t_len = past_len + n_new
    nc = t_len // CMP_BLOCK
    nsb = -(-t_len // SEL_BLOCK)
    lanes = cmp_ref.shape[0]
    new_pad = new_ref.shape[0]
    qi = lax.broadcasted_iota(jnp.int32, (rows, 1), 0) % n_new
    t_rows = past_len + qi

    @pl.when(p == 0)
    def _():
        cl = lax.broadcasted_iota(jnp.int32, (1, nc), 1)
        cblk = jnp.where(cl < nc // 2, 2 * cl, 2 * (cl - nc // 2) + 1)
        cmask = (cblk + 1) * CMP_BLOCK - 1 <= t_rows
        rr = lax.broadcasted_iota(jnp.int32, (rows, rows), 0) % n_new
        rc = lax.broadcasted_iota(jnp.int32, (rows, rows), 1) % n_new
        group_sum = jnp.where(rr == rc, 1.0, 0.0)
        blk = lax.broadcasted_iota(jnp.int32, (rows, lanes), 1)
        valid = blk * SEL_BLOCK <= t_rows
        forced = (blk == 0) | (blk == t_rows // SEL_BLOCK)
        for k in range(NSA_KV_HEADS):
            qk = q_ref[k].astype(bf16)
            kcb = cmp_ref[:, NSA_HD * k:NSA_HD * (k + 1)].astype(bf16)
            vcb = cmp_ref[:, NSA_HK + NSA_HD * k:NSA_HK + NSA_HD * (k + 1)].astype(bf16)
            s_c = lax.dot_general(qk, kcb, (((1,), (1,)), ((), ())), preferred_element_type=f32)
            p_c = _softmax_rows(jnp.where(cmask, s_c, NEG_INF))
            p_c = p_c * jnp.where(t_rows >= CMP_BLOCK - 1, 1.0, 0.0)
            oc_sc[k] = jnp.dot(p_c.astype(bf16), vcb, preferred_element_type=f32)
            imp = jnp.dot(group_sum, p_c, precision=lax.Precision.HIGHEST, preferred_element_type=f32)
            imp = imp[:, :nc // 2] + imp[:, nc // 2:]
            imp = jnp.concatenate([imp, jnp.zeros((rows, lanes - nc // 2), f32)], axis=1)
            score = jnp.where(forced, FORCE_SCORE, jnp.where(valid, imp, -1.0))
            score = jnp.where(blk < nsb, score, -2.0)
            sel_sc[k] = _topk_mask_lanes(score, SEL_TOPK, nsb)
            m_sc[k] = jnp.full((rows, 1), NEG_INF, f32)
            l_sc[k] = jnp.zeros((rows, 1), f32)
            acc_sc[k] = jnp.zeros((rows, NSA_HD), f32)

    eb = lax.broadcasted_iota(jnp.int32, (lanes, NSA_TILE), 0)
    ek = (NSA_TILE // SEL_BLOCK) * p + lax.broadcasted_iota(jnp.int32, (lanes, NSA_TILE), 1) // SEL_BLOCK
    expand = jnp.where(eb == ek, 1.0, 0.0).astype(bf16)
    kpos = p * NSA_TILE + lax.broadcasted_iota(jnp.int32, (1, NSA_TILE), 1)
    for k in range(NSA_KV_HEADS):
        qk = q_ref[k].astype(bf16)
        kt = page_ref[:, NSA_HD * k:NSA_HD * (k + 1)].astype(bf16)
        vt = page_ref[:, NSA_HK + NSA_HD * k:NSA_HK + NSA_HD * (k + 1)].astype(bf16)
        msk = jnp.dot(sel_sc[k].astype(bf16), expand, preferred_element_type=f32)
        ok = (msk > 0.5) & (kpos <= t_rows)
        m_sc[k], l_sc[k], acc_sc[k] = _flash_step(qk, kt, vt, ok, (m_sc[k], l_sc[k], acc_sc[k]))

    @pl.when(p == pl.num_programs(1) - 1)
    def _():
        jn = lax.broadcasted_iota(jnp.int32, (1, new_pad), 1)
        npos = past_len + jn
        wpos = past_len - WINDOW + lax.broadcasted_iota(jnp.int32, (1, WINDOW), 1)
        for k in range(NSA_KV_HEADS):
            qk = q_ref[k].astype(bf16)
            kc = slice(NSA_HD * k, NSA_HD * (k + 1))
            vc = slice(NSA_HK + NSA_HD * k, NSA_HK + NSA_HD * (k + 1))
            in_last = sel_sc[k][:, nsb - 1:nsb] > 0.5
            ok = in_last & (npos <= t_rows) & (jn < n_new)
            _, l_s, acc_s = _flash_step(qk, new_ref[:, kc].astype(bf16), new_ref[:, vc].astype(bf16), ok,
                                        (m_sc[k], l_sc[k], acc_sc[k]))
            o_s = acc_s / l_s
            ok1 = (wpos <= t_rows) & (wpos > t_rows - WINDOW)
            carry = _flash_step(qk, win_ref[:, kc].astype(bf16), win_ref[:, vc].astype(bf16), ok1, _flash_init(rows))
            ok2 = (npos <= t_rows) & (npos > t_rows - WINDOW) & (jn < n_new)
            _, l_w, acc_w = _flash_step(qk, wnew_ref[:, kc].astype(bf16), wnew_ref[:, vc].astype(bf16), ok2, carry)
            o_w = acc_w / l_w
            g = g_ref[k]
            o_ref[k] = g[:, 0:1] * oc_sc[k] + g[:, 1:2] * o_s + g[:, 2:3] * o_w


def _nsa_decode(q, gates, cmp, pool, page_table, new_rows, win, new_wrows, past_len, n_new):
    nb, n_pages = page_table.shape
    rows = NSA_GROUP * n_new
    w = 2 * NSA_HK
    assert past_len == n_pages * NSA_TILE and past_len % SEL_BLOCK == 0 and n_new <= CMP_BLOCK
    assert win.shape[1] == WINDOW and past_len >= WINDOW
    nc = (past_len + n_new) // CMP_BLOCK
    assert cmp.shape[1] == nc and nc % 2 == 0 and -(-(past_len + n_new) // SEL_BLOCK) <= nc
    per_b = lambda shape: pl.BlockSpec((None,) + shape, lambda b, p, pt: (b,) + (0,) * len(shape))
    grid_spec = pltpu.PrefetchScalarGridSpec(
        num_scalar_prefetch=1, grid=(nb, n_pages),
        in_specs=[per_b((NSA_KV_HEADS, rows, NSA_HD)), per_b((NSA_KV_HEADS, rows, 3)), per_b((nc, w)),
                  pl.BlockSpec((None, NSA_TILE, w), lambda b, p, pt: (pt[b, p], 0, 1)),
                  pl.BlockSpec((None, new_rows.shape[1], w), lambda b, p, pt: (b, 0, 1)),
                  per_b((WINDOW, w)), per_b((new_wrows.shape[1], w))],
        out_specs=per_b((NSA_KV_HEADS, rows, NSA_HD)),
        scratch_shapes=[pltpu.VMEM((NSA_KV_HEADS, rows, nc), f32), pltpu.VMEM((NSA_KV_HEADS, rows, 1), f32),
                        pltpu.VMEM((NSA_KV_HEADS, rows, 1), f32), pltpu.VMEM((NSA_KV_HEADS, rows, NSA_HD), f32),
                        pltpu.VMEM((NSA_KV_HEADS, rows, NSA_HD), f32)])
    return pl.pallas_call(
        functools.partial(_nsa_decode_kernel, past_len=past_len, n_new=n_new), grid_spec=grid_spec,
        out_shape=jax.ShapeDtypeStruct((nb, NSA_KV_HEADS, rows, NSA_HD), f32),
        compiler_params=_cparams(("parallel", "arbitrary")), name="nsa_decode",
    )(page_table, q, gates, cmp, pool, new_rows, win, new_wrows)


def _nsa_layer(xp, xs, nbp, nbs, pool, win, page_table, ln_g, ln_b, w_in, w_out):
    hq, hk = NSA_HQ, NSA_HK
    ws = [w_in[:, :hq].astype(bf16), w_in[:, hq:hq + 4 * hk].astype(bf16),
          w_in[:, hq + 4 * hk:hq + 6 * hk].astype(bf16), w_in[:, hq + 6 * hk:].astype(bf16)]
    wo = w_out.astype(bf16)
    eps = [_q_ep, _rows_ep, _wrows_ep, _sigmoid_ep]
    sp_len = xp.shape[0] // nbp
    ss_len = xs.shape[0] // nbs
    n_pages = page_table.shape[1]
    page = pool.shape[1]
    assert page == NSA_TILE
    past_len = n_pages * page
    ms = xs.shape[0]

    cos_p, sin_p = _rope_tables(jnp.arange(sp_len, dtype=f32))
    cos_s, sin_s = _rope_tables(jnp.tile(past_len + jnp.arange(ss_len, dtype=f32), nbs))
    qp, rows_p, wrows_p, gp = _proj(xp, ws, eps, aux=(cos_p, sin_p), aux_period=(sp_len, sp_len), tm=256, name="nsa_in")
    qs, rows_s, wrows_s, gs = _proj(xs, ws, eps, aux=(cos_s, sin_s), aux_period=(ms, ms), tm=ms, name="nsa_in_s")

    def cmp_tables(nc):
        return _rope_tables(jnp.arange(nc, dtype=f32) * CMP_BLOCK + (CMP_BLOCK - 1) / 2.0)

    pages_p = sp_len // page
    pt_p = jnp.arange(nbp * pages_p, dtype=jnp.int32).reshape(nbp, pages_p)
    cmp_p = _even_odd(_nsa_cmp(rows_p.reshape(nbp * pages_p, page, 4 * hk), pt_p, *cmp_tables(sp_len // CMP_BLOCK)))
    op = _nsa_prompt(qp.reshape(nbp, sp_len, hq), gp.reshape(nbp, sp_len, 3 * NSA_HEADS), cmp_p,
                     rows_p.reshape(nbp, sp_len, 4 * hk), wrows_p.reshape(nbp, sp_len, 2 * hk))

    pool3 = pool.reshape(pool.shape[0], page, 4 * hk)
    cmp_s = _even_odd(_nsa_cmp(pool3, page_table, *cmp_tables(past_len // CMP_BLOCK)))
    rows_s3 = rows_s.reshape(nbs, ss_len, 4 * hk)
    wrows_s3 = wrows_s.reshape(nbs, ss_len, 2 * hk)
    new_pad = 8
    tpad = lambda a: jnp.pad(a, ((0, 0), (0, new_pad - ss_len), (0, 0)))
    to_heads = lambda a, last: (a.reshape(nbs, ss_len, NSA_KV_HEADS, NSA_GROUP, last).transpose(0, 2, 3, 1, 4)
                                .reshape(nbs, NSA_KV_HEADS, NSA_GROUP * ss_len, last))
    win3 = win.reshape(nbs, win.shape[1], 2 * hk)
    os_ = _nsa_decode(to_heads(qs, NSA_HD), to_heads(gs, 3), cmp_s, pool3, page_table, tpad(rows_s3), win3,
                      tpad(wrows_s3), past_len, ss_len)
    os_ = (os_.reshape(nbs, NSA_KV_HEADS, NSA_GROUP, ss_len, NSA_HD).transpose(0, 3, 1, 2, 4).reshape(ms, hq))

    xp = _outproj_ln(op.reshape(nbp * sp_len, hq), wo, xp, ln_g, ln_b, name="nsa_out")
    xs = _outproj_ln(os_, wo, xs, ln_g, ln_b, tm=ms, name="nsa_out_s")
    kv_shape = (4, NSA_KV_HEADS, NSA_HD)
    win_shape = (2, NSA_KV_HEADS, NSA_HD)
    rp = rows_p.reshape((nbp, sp_len) + kv_shape)
    rs = rows_s.reshape((nbs, ss_len) + kv_shape)
    wp = wrows_p.reshape(nbp, sp_len, 2 * hk)[:, sp_len - min(WINDOW, sp_len):]
    wp = wp.reshape((nbp, wp.shape[1]) + win_shape)
    wsn = jnp.concatenate([win3, wrows_s3], axis=1)
    wsn = wsn[:, wsn.shape[1] - min(WINDOW, wsn.shape[1]):]
    wsn = wsn.reshape((nbs, wsn.shape[1]) + win_shape)
    return xp, xs, (rp, rs, wp, wsn)


FFN_TF = 256
MOE_TF = 512
ROW_TILE = 1024


def kernel(x_prompt, x_sample, state_l0_lru_conv, state_l0_lru_h, cache_l1_nsa_kv, cache_l1_nsa_win, page_table,
           state_l2_ssd_conv, state_l2_ssd_ssm, state_l3_lru_conv, state_l3_lru_h, ln_g, ln_b, lru_w_in,
           lru_conv_w, lru_conv_b, lru_w_a, lru_b_a, lru_w_x, lru_b_x, lru_lam, lru_w_out, nsa_w_in, nsa_w_out,
           ssd_w_in, ssd_conv_w, ssd_conv_b, ssd_dt_bias, ssd_a_log, ssd_d, ssd_norm_g, ssd_w_out, ffn_w_in,
           ffn_w_out, moe_router_w, moe_router_b, moe_w_in, moe_w_out):
    nbp, sp_len, d = x_prompt.shape
    nbs, ss_len, _ = x_sample.shape
    xp = x_prompt.reshape(nbp * sp_len, d)
    xs = x_sample.reshape(nbs * ss_len, d)
    ms = xs.shape[0]
    lru_state = {0: (state_l0_lru_conv, state_l0_lru_h), 3: (state_l3_lru_conv, state_l3_lru_h)}
    new = {}
    for i in range(DEPTH):
        kind, j = i % 3, i // 3
        g0, b0 = ln_g[i, 0], ln_b[i, 0]
        if kind == 0:
            conv_s, h_s = lru_state[i]
            xp, xs, new[i] = _lru_layer(xp, xs, nbp, nbs, conv_s, h_s, g0, b0, lru_w_in[j], lru_conv_w[j],
                                        lru_conv_b[j], lru_w_a[j], lru_b_a[j], lru_w_x[j], lru_b_x[j],
                                        lru_lam[j], lru_w_out[j])
        elif kind == 1:
            xp, xs, new[i] = _nsa_layer(xp, xs, nbp, nbs, cache_l1_nsa_kv, cache_l1_nsa_win, page_table, g0, b0,
                                        nsa_w_in[j], nsa_w_out[j])
        else:
            xp, xs, new[i] = _ssd_layer(xp, xs, nbp, nbs, state_l2_ssd_conv, state_l2_ssd_ssm, g0, b0,
                                        ssd_w_in[j], ssd_conv_w[j], ssd_conv_b[j], ssd_dt_bias[j], ssd_a_log[j],
                                        ssd_d[j], ssd_norm_g[j], ssd_w_out[j])
        g1, b1 = ln_g[i, 1], ln_b[i, 1]
        k = i // 2
        if i % 2 == 0:
            xp = _ffn_ln(xp, ffn_w_in[k], ffn_w_out[k], g1, b1, ROW_TILE, FFN_TF, name="ffn")
            xs = _ffn_ln(xs, ffn_w_in[k], ffn_w_out[k], g1, b1, ms, FFN_TF, name="ffn_s")
        else:
            cp = _router(xp, moe_router_w[k], moe_router_b[k], tm=512)
            cs = _router(xs, moe_router_w[k], moe_router_b[k], tm=ms)
            xp = _moe_ln(xp, cp, moe_w_in[k], moe_w_out[k], g1, b1, ROW_TILE, MOE_TF)
            xs = _moe_ln(xs, cs, moe_w_in[k], moe_w_out[k], g1, b1, ms, MOE_TF)
    out = [xp.reshape(nbp, sp_len, d), xs.reshape(nbs, ss_len, d)]
    for i in range(DEPTH):
        out.extend(new[i])
    return tuple(out)
```

```python
import functools
import math

import jax
import jax.numpy as jnp
from jax import lax
from jax.experimental import pallas as pl
from jax.experimental.pallas import tpu as pltpu

f32 = jnp.float32
bf16 = jnp.bfloat16

D_MODEL = 1024
DEPTH = 4
ALPHA = (2 * DEPTH) ** 0.25
LN_EPS = 1e-5
RMS_EPS = 1e-5
CONV_W = 4
NEG_INF = -1e30

D_RNN = 1344
LRU_BLOCKS = 16
LRU_BS = D_RNN // LRU_BLOCKS
LRU_C = 8.0

NSA_HEADS = 16
NSA_KV_HEADS = 4
NSA_HD = 64
NSA_GROUP = NSA_HEADS // NSA_KV_HEADS
CMP_BLOCK = 32
SEL_BLOCK = 64
SEL_TOPK = 16
WINDOW = 512
FORCE_SCORE = 1e4
ROPE_THETA = 10000.0
NSA_HQ = NSA_HEADS * NSA_HD
NSA_HK = NSA_KV_HEADS * NSA_HD

SSD_INNER = 2 * D_MODEL
SSD_HEAD_DIM = 64
SSD_HEADS = SSD_INNER // SSD_HEAD_DIM
SSD_GROUPS = 4
SSD_STATE = 128
SSD_CHUNK = 128
SSD_CONV_DIM = SSD_INNER + 2 * SSD_GROUPS * SSD_STATE

N_EXPERTS = 8
TOP_K = 2

VMEM_LIMIT_V7X = 56 * 1024 * 1024


def _cparams(sem):
    return pltpu.CompilerParams(dimension_semantics=sem, vmem_limit_bytes=VMEM_LIMIT_V7X)


def _const_spec(shape):
    nd = len(shape)
    return pl.BlockSpec(shape, lambda *_: (0,) * nd, pipeline_mode=pl.Buffered(1))


def _layer_norm(v, g, b):
    mu = jnp.mean(v, -1, keepdims=True)
    d = v - mu
    var = jnp.mean(d * d, -1, keepdims=True)
    return d * lax.rsqrt(var + LN_EPS) * g + b


def _proj_kernel(*refs, epilogues, n_aux):
    n_out = len(epilogues)
    x_ref = refs[0]
    w_refs = refs[1:1 + n_out]
    aux_refs = refs[1 + n_out:1 + n_out + n_aux]
    o_refs = refs[1 + n_out + n_aux:]
    x = x_ref[...].astype(bf16)
    aux = [a[...] for a in aux_refs]
    for w_ref, o_ref, ep in zip(w_refs, o_refs, epilogues):
        y = jnp.dot(x, w_ref[...], preferred_element_type=f32)
        if ep is not None:
            y = ep(y, *aux)
        o_ref[...] = y.astype(o_ref.dtype)


def _proj(x, ws, epilogues, aux=(), aux_period=(), tm=256, name="proj"):
    m, k = x.shape
    assert m % tm == 0
    in_specs = [pl.BlockSpec((tm, k), lambda i: (i, 0))]
    in_specs += [_const_spec(w.shape) for w in ws]
    for a, p in zip(aux, aux_period):
        if p:
            assert p % tm == 0 and a.shape[0] == p
            in_specs.append(pl.BlockSpec((tm, a.shape[1]), functools.partial(lambda i, n: (i % n, 0), n=p // tm)))
        else:
            in_specs.append(_const_spec(a.shape))
    out_shape = [jax.ShapeDtypeStruct((m, w.shape[1]), f32) for w in ws]
    out_specs = [pl.BlockSpec((tm, w.shape[1]), lambda i: (i, 0)) for w in ws]
    return pl.pallas_call(
        functools.partial(_proj_kernel, epilogues=tuple(epilogues), n_aux=len(aux)),
        grid=(m // tm,), in_specs=in_specs, out_specs=out_specs, out_shape=out_shape,
        compiler_params=_cparams(("parallel",)), name=name,
    )(x, *ws, *aux)


def _outproj_ln_kernel(a_ref, w_ref, x_ref, g_ref, b_ref, o_ref):
    y = jnp.dot(a_ref[...].astype(bf16), w_ref[...], preferred_element_type=f32)
    o_ref[...] = _layer_norm(ALPHA * x_ref[...] + y, g_ref[...], b_ref[...])


def _outproj_ln(a, w, x, g, b, tm=256, name="outproj_ln"):
    m, k = a.shape
    d = w.shape[1]
    assert m % tm == 0
    return pl.pallas_call(
        _outproj_ln_kernel, grid=(m // tm,),
        in_specs=[pl.BlockSpec((tm, k), lambda i: (i, 0)), _const_spec(w.shape),
                  pl.BlockSpec((tm, d), lambda i: (i, 0)), _const_spec((1, d)), _const_spec((1, d))],
        out_specs=pl.BlockSpec((tm, d), lambda i: (i, 0)),
        out_shape=jax.ShapeDtypeStruct((m, d), f32),
        compiler_params=_cparams(("parallel",)), name=name,
    )(a, w, x, g.reshape(1, d), b.reshape(1, d))


def _ffn_ln_kernel(x_ref, wg_ref, wu_ref, wo_ref, g_ref, b_ref, o_ref, acc_ref):
    j = pl.program_id(1)
    x = x_ref[...]
    xb = x.astype(bf16)
    hg = jnp.dot(xb, wg_ref[...].astype(bf16), preferred_element_type=f32)
    hu = jnp.dot(xb, wu_ref[...].astype(bf16), preferred_element_type=f32)
    act = (hg * jax.nn.sigmoid(hg) * hu).astype(bf16)
    part = jnp.dot(act, wo_ref[...].astype(bf16), preferred_element_type=f32)

    @pl.when(j == 0)
    def _():
        acc_ref[...] = part

    @pl.when(j > 0)
    def _():
        acc_ref[...] += part

    @pl.when(j == pl.num_programs(1) - 1)
    def _():
        o_ref[...] = _layer_norm(ALPHA * x + acc_ref[...], g_ref[...], b_ref[...])


def _ffn_ln(x, w_in, w_out, g, b, tm, tf, name="ffn_ln"):
    m, d = x.shape
    f = w_out.shape[0]
    assert m % tm == 0 and f % tf == 0
    nf = f // tf
    return pl.pallas_call(
        _ffn_ln_kernel, grid=(m // tm, nf),
        in_specs=[pl.BlockSpec((tm, d), lambda i, j: (i, 0)),
                  pl.BlockSpec((d, tf), lambda i, j: (0, j)),
                  pl.BlockSpec((d, tf), lambda i, j: (0, j + nf)),
                  pl.BlockSpec((tf, d), lambda i, j: (j, 0)),
                  _const_spec((1, d)), _const_spec((1, d))],
        out_specs=pl.BlockSpec((tm, d), lambda i, j: (i, 0)),
        out_shape=jax.ShapeDtypeStruct((m, d), f32),
        scratch_shapes=[pltpu.VMEM((tm, d), f32)],
        compiler_params=_cparams(("parallel", "arbitrary")), name=name,
    )(x, w_in, w_in, w_out, g.reshape(1, d), b.reshape(1, d))


def _router_kernel(x_ref, w_ref, b_ref, c_ref):
    logits = jnp.dot(x_ref[...], w_ref[...], precision=lax.Precision.HIGHEST,
                     preferred_element_type=f32) + b_ref[...]
    e_idx, i1, i2, w1, w2 = _top2(logits)
    c_ref[...] = jnp.where(e_idx == i1, w1, 0.0) + jnp.where(e_idx == i2, w2, 0.0)


def _router(x, w, b, tm=512):
    m, d = x.shape
    assert m % tm == 0
    return pl.pallas_call(
        _router_kernel, grid=(m // tm,),
        in_specs=[pl.BlockSpec((tm, d), lambda i: (i, 0)), _const_spec(w.shape),
                  _const_spec((1, N_EXPERTS))],
        out_specs=pl.BlockSpec((tm, N_EXPERTS), lambda i: (i, 0)),
        out_shape=jax.ShapeDtypeStruct((m, N_EXPERTS), f32),
        compiler_params=_cparams(("parallel",)), name="router",
    )(x, w, b.reshape(1, N_EXPERTS))


def _moe_ln_kernel(x_ref, c_ref, wg_ref, wu_ref, wo_ref, g_ref, b_ref, o_ref, acc_ref):
    e = pl.program_id(1)
    j = pl.program_id(2)
    x = x_ref[...]
    xb = x.astype(bf16)
    c = c_ref[...]
    ce = jnp.sum(jnp.where(lax.broadcasted_iota(jnp.int32, c.shape, 1) == e, c, 0.0), -1, keepdims=True)
    hg = jnp.dot(xb, wg_ref[...].astype(bf16), preferred_element_type=f32)
    hu = jnp.dot(xb, wu_ref[...].astype(bf16), preferred_element_type=f32)
    act = (hg * jax.nn.sigmoid(hg) * hu).astype(bf16)
    part = ce * jnp.dot(act, wo_ref[...].astype(bf16), preferred_element_type=f32)
    first = jnp.logical_and(e == 0, j == 0)

    @pl.when(first)
    def _():
        acc_ref[...] = part

    @pl.when(jnp.logical_not(first))
    def _():
        acc_ref[...] += part

    @pl.when(jnp.logical_and(e == pl.num_programs(1) - 1, j == pl.num_programs(2) - 1))
    def _():
        o_ref[...] = _layer_norm(ALPHA * x + acc_ref[...], g_ref[...], b_ref[...])


def _moe_ln(x, comb, w_in, w_out, g, b, tm, tf):
    m, d = x.shape
    ne, f, _ = w_out.shape
    assert m % tm == 0 and f % tf == 0
    nf = f // tf
    return pl.pallas_call(
        _moe_ln_kernel, grid=(m // tm, ne, nf),
        in_specs=[pl.BlockSpec((tm, d), lambda i, e, j: (i, 0)),
                  pl.BlockSpec((tm, ne), lambda i, e, j: (i, 0)),
                  pl.BlockSpec((None, d, tf), lambda i, e, j: (e, 0, j)),
                  pl.BlockSpec((None, d, tf), lambda i, e, j: (e, 0, j + nf)),
                  pl.BlockSpec((None, tf, d), lambda i, e, j: (e, j, 0)),
                  _const_spec((1, d)), _const_spec((1, d))],
        out_specs=pl.BlockSpec((tm, d), lambda i, e, j: (i, 0)),
        out_shape=jax.ShapeDtypeStruct((m, d), f32),
        scratch_shapes=[pltpu.VMEM((tm, d), f32)],
        compiler_params=_cparams(("parallel", "arbitrary", "arbitrary")), name="moe_ln",
    )(x, comb, w_in, w_in, w_out, g.reshape(1, d), b.reshape(1, d))


MOE_TM = 1024
MOE_ROUTE_TM = 512
MOE_DISPATCH_TM = 512
MOE_COMBINE_TM = 256
ROUTE_COLS = 8


def _top2(logits):
    e_idx = lax.broadcasted_iota(jnp.int32, logits.shape, 1)
    v1 = jnp.max(logits, -1, keepdims=True)
    i1 = jnp.min(jnp.where(logits == v1, e_idx, N_EXPERTS), -1, keepdims=True)
    rest = jnp.where(e_idx == i1, -jnp.inf, logits)
    v2 = jnp.max(rest, -1, keepdims=True)
    i2 = jnp.min(jnp.where(rest == v2, e_idx, N_EXPERTS), -1, keepdims=True)
    e2 = jnp.exp(v2 - v1)
    den = 1.0 + e2
    return e_idx, i1, i2, 1.0 / den, e2 / den


def _route_kernel(x_ref, w_ref, b_ref, r_ref, cnt_ref, carry_ref):
    i = pl.program_id(0)
    tm = x_ref.shape[0]

    @pl.when(i == 0)
    def _():
        carry_ref[...] = jnp.zeros_like(carry_ref)

    logits = jnp.dot(x_ref[...], w_ref[...], precision=lax.Precision.HIGHEST,
                     preferred_element_type=f32) + b_ref[...]
    e_idx, i1, i2, w1, w2 = _top2(logits)
    hit1 = e_idx == i1
    hit2 = e_idx == i2
    a = jnp.where(hit1, 1.0, 0.0) + jnp.where(hit2, 1.0, 0.0)
    row = lax.broadcasted_iota(jnp.int32, (tm, tm), 0)
    col = lax.broadcasted_iota(jnp.int32, (tm, tm), 1)
    before = jnp.where(col < row, 1.0, 0.0).astype(bf16)
    c = carry_ref[...] + jnp.dot(before, a.astype(bf16), preferred_element_type=f32)
    pos1 = jnp.sum(jnp.where(hit1, c, 0.0), -1, keepdims=True)
    pos2 = jnp.sum(jnp.where(hit2, c, 0.0), -1, keepdims=True)
    carry_ref[...] += jnp.sum(a, 0, keepdims=True)
    cols = (i1.astype(f32), i2.astype(f32), pos1, pos2, w1, w2)
    rec = jnp.zeros((tm, ROUTE_COLS), f32)
    for n, v in enumerate(cols):
        rec = jnp.where(e_idx == n, v, rec)
    r_ref[...] = rec

    @pl.when(i == pl.num_programs(0) - 1)
    def _():
        cnt_ref[...] = carry_ref[...]


def _route(x, w, b):
    m, d = x.shape
    tm = MOE_ROUTE_TM
    assert m % tm == 0 and ROUTE_COLS == N_EXPERTS
    return pl.pallas_call(
        _route_kernel, grid=(m // tm,),
        in_specs=[pl.BlockSpec((tm, d), lambda i: (i, 0)), _const_spec(w.shape), _const_spec((1, N_EXPERTS))],
        out_specs=[pl.BlockSpec((tm, ROUTE_COLS), lambda i: (i, 0)), pl.BlockSpec((1, N_EXPERTS), lambda i: (0, 0))],
        out_shape=[jax.ShapeDtypeStruct((m, ROUTE_COLS), f32), jax.ShapeDtypeStruct((1, N_EXPERTS), f32)],
        scratch_shapes=[pltpu.VMEM((1, N_EXPERTS), f32)],
        compiler_params=_cparams(("arbitrary",)), name="moe_route",
    )(x, w, b.reshape(1, N_EXPERTS))


def _row_copy(src, src_row, dst, dst_row, sem):
    return pltpu.make_async_copy(src.at[pl.ds(src_row, 1)], dst.at[pl.ds(dst_row, 1)], sem)


def _dispatch_kernel(d1_ref, d2_ref, x_hbm, init_hbm, o_hbm, sem):
    del init_hbm
    base = pl.program_id(0) * MOE_DISPATCH_TM

    def copies(t):
        tok = base + t
        return (_row_copy(x_hbm, tok, o_hbm, d1_ref[tok], sem), _row_copy(x_hbm, tok, o_hbm, d2_ref[tok], sem))

    def issue(t, carry):
        for cp in copies(t):
            cp.start()
        return carry

    def drain(t, carry):
        for cp in copies(t):
            cp.wait()
        return carry

    lax.fori_loop(0, MOE_DISPATCH_TM, issue, 0)
    lax.fori_loop(0, MOE_DISPATCH_TM, drain, 0)


def _dispatch(x, d1, d2, n_rows):
    m, d = x.shape
    assert m % MOE_DISPATCH_TM == 0
    any_spec = pl.BlockSpec(memory_space=pl.ANY)
    grid_spec = pltpu.PrefetchScalarGridSpec(
        num_scalar_prefetch=2, grid=(m // MOE_DISPATCH_TM,),
        in_specs=[any_spec, any_spec], out_specs=any_spec,
        scratch_shapes=[pltpu.SemaphoreType.DMA(())])
    return pl.pallas_call(
        _dispatch_kernel, grid_spec=grid_spec, out_shape=jax.ShapeDtypeStruct((n_rows, d), x.dtype),
        input_output_aliases={3: 0},
        compiler_params=_cparams(("arbitrary",)), name="moe_dispatch",
    )(d1, d2, x, jnp.zeros((n_rows, d), x.dtype))


def _moe_group_kernel(te_ref, nu_ref, x_ref, wg_ref, wu_ref, wo_ref, o_ref, acc_ref):
    del te_ref
    t = pl.program_id(0)
    j = pl.program_id(1)
    used = t < nu_ref[0]
    last = j == pl.num_programs(1) - 1

    @pl.when(used)
    def _():
        xb = x_ref[...].astype(bf16)
        hg = jnp.dot(xb, wg_ref[...].astype(bf16), preferred_element_type=f32)
        hu = jnp.dot(xb, wu_ref[...].astype(bf16), preferred_element_type=f32)
        act = (hg * jax.nn.sigmoid(hg) * hu).astype(bf16)
        part = jnp.dot(act, wo_ref[...].astype(bf16), preferred_element_type=f32)

        @pl.when(j == 0)
        def _():
            acc_ref[...] = part

        @pl.when(j > 0)
        def _():
            acc_ref[...] += part

        @pl.when(last)
        def _():
            o_ref[...] = acc_ref[...]

    @pl.when(jnp.logical_and(jnp.logical_not(used), last))
    def _():
        o_ref[...] = jnp.zeros_like(o_ref)


def _moe_group(xs, tile_expert, n_used, w_in, w_out, tf):
    r, d = xs.shape
    ne, f, _ = w_out.shape
    assert r % MOE_TM == 0 and f % tf == 0
    nf = f // tf
    n_tiles = r // MOE_TM

    def jj(t, j, nu):
        return jnp.where(t < nu[0], j, nf - 1)

    grid_spec = pltpu.PrefetchScalarGridSpec(
        num_scalar_prefetch=2, grid=(n_tiles, nf),
        in_specs=[pl.BlockSpec((MOE_TM, d), lambda t, j, te, nu: (jnp.minimum(t, nu[0] - 1), 0)),
                  pl.BlockSpec((None, d, tf), lambda t, j, te, nu: (te[t], 0, jj(t, j, nu))),
                  pl.BlockSpec((None, d, tf), lambda t, j, te, nu: (te[t], 0, jj(t, j, nu) + nf)),
                  pl.BlockSpec((None, tf, d), lambda t, j, te, nu: (te[t], jj(t, j, nu), 0))],
        out_specs=pl.BlockSpec((MOE_TM, d), lambda t, j, te, nu: (t, 0)),
        scratch_shapes=[pltpu.VMEM((MOE_TM, d), f32)])
    return pl.pallas_call(
        _moe_group_kernel, grid_spec=grid_spec, out_shape=jax.ShapeDtypeStruct((r, d), f32),
        compiler_params=_cparams(("arbitrary", "arbitrary")), name="moe_group",
    )(tile_expert, n_used, xs, w_in, w_in, w_out)


def _moe_combine_kernel(d1_ref, d2_ref, x_ref, r_ref, y_hbm, g_ref, b_ref, o_ref, ya_ref, yb_ref, sem):
    tm = x_ref.shape[0]
    base = pl.program_id(0) * tm

    def copies(t):
        tok = base + t
        return (_row_copy(y_hbm, d1_ref[tok], ya_ref, t, sem), _row_copy(y_hbm, d2_ref[tok], yb_ref, t, sem))

    def issue(t, carry):
        for cp in copies(t):
            cp.start()
        return carry

    def drain(t, carry):
        for cp in copies(t):
            cp.wait()
        return carry

    lax.fori_loop(0, tm, issue, 0)
    lax.fori_loop(0, tm, drain, 0)
    r = r_ref[...]
    mix = r[:, 4:5] * ya_ref[...] + r[:, 5:6] * yb_ref[...]
    o_ref[...] = _layer_norm(ALPHA * x_ref[...] + mix, g_ref[...], b_ref[...])


def _moe_combine(x, route, y, d1, d2, g, b):
    m, d = x.shape
    tm = MOE_COMBINE_TM
    assert m % tm == 0
    grid_spec = pltpu.PrefetchScalarGridSpec(
        num_scalar_prefetch=2, grid=(m // tm,),
        in_specs=[pl.BlockSpec((tm, d), lambda i, *_: (i, 0)),
                  pl.BlockSpec((tm, ROUTE_COLS), lambda i, *_: (i, 0)),
                  pl.BlockSpec(memory_space=pl.ANY),
                  pl.BlockSpec((1, d), lambda i, *_: (0, 0)), pl.BlockSpec((1, d), lambda i, *_: (0, 0))],
        out_specs=pl.BlockSpec((tm, d), lambda i, *_: (i, 0)),
        scratch_shapes=[pltpu.VMEM((tm, d), f32), pltpu.VMEM((tm, d), f32), pltpu.SemaphoreType.DMA(())])
    return pl.pallas_call(
        _moe_combine_kernel, grid_spec=grid_spec, out_shape=jax.ShapeDtypeStruct((m, d), f32),
        compiler_params=_cparams(("arbitrary",)), name="moe_combine",
    )(d1, d2, x, route, y, g.reshape(1, d), b.reshape(1, d))


def _moe_sparse_ln(x, router_w, router_b, w_in, w_out, g, b, tf):
    m, d = x.shape
    route, counts = _route(x, router_w, router_b)
    counts = counts[0].astype(jnp.int32)
    padded = (counts + MOE_TM - 1) // MOE_TM * MOE_TM
    ends = jnp.cumsum(padded)
    offs = ends - padded
    n_tiles = -(-TOP_K * m // MOE_TM) + N_EXPERTS
    n_used = (ends[-1] // MOE_TM).reshape(1)
    tile_start = jnp.minimum(jnp.arange(n_tiles, dtype=jnp.int32), n_used[0] - 1) * MOE_TM
    tile_expert = jnp.minimum(jnp.searchsorted(ends, tile_start, side="right"), N_EXPERTS - 1).astype(jnp.int32)
    e1 = route[:, 0].astype(jnp.int32)
    e2 = route[:, 1].astype(jnp.int32)
    d1 = offs[e1] + route[:, 2].astype(jnp.int32)
    d2 = offs[e2] + route[:, 3].astype(jnp.int32)
    xs = _dispatch(x, d1, d2, n_tiles * MOE_TM)
    y = _moe_group(xs, tile_expert, n_used, w_in, w_out, tf)
    return _moe_combine(x, route, y, d1, d2, g, b)


def _softplus(x):
    return jnp.maximum(x, 0.0) + jnp.log(1.0 + jnp.exp(-jnp.abs(x)))


def _lru_gates(xc, wa, ba, wx, bx, sp):
    xcb = xc.astype(bf16)
    r = jax.nn.sigmoid(jnp.dot(xcb, wa, preferred_element_type=f32) + ba)
    gi = jax.nn.sigmoid(jnp.dot(xcb, wx, preferred_element_type=f32) + bx)
    log_a = sp * r
    th = jnp.tanh(log_a)
    one_minus_a2 = -2.0 * th / (1.0 - th)
    return jnp.exp(log_a), jnp.sqrt(one_minus_a2) * gi * xc


def _lru_core_kernel(xb_ref, g_ref, cw_ref, cb_ref, wa_ref, ba_ref, wx_ref, bx_ref, lam_ref,
                     y_ref, hlast_ref, ext_ref, a_ref, b_ref, h_ref, *, ts, nb):
    i = pl.program_id(0)
    hist = 8

    @pl.when(i == 0)
    def _():
        ext_ref[:, 0:hist, :] = jnp.zeros((nb, hist, ext_ref.shape[2]), f32)
        h_ref[...] = jnp.zeros_like(h_ref)

    ext_ref[:, hist:hist + ts, :] = xb_ref[...]
    sp = -LRU_C * _softplus(-lam_ref[...])
    cw = cw_ref[...]
    for b in range(nb):
        xc = cb_ref[...]
        for k in range(CONV_W):
            off = hist - (CONV_W - 1) + k
            xc = xc + cw[k:k + 1, :] * ext_ref[b, off:off + ts, :]
        a, bt = _lru_gates(xc, wa_ref[...], ba_ref[...], wx_ref[...], bx_ref[...], sp)
        a_ref[b] = a
        b_ref[b] = bt
    ext_ref[:, 0:hist, :] = ext_ref[:, ts:ts + hist, :]

    def step(t, h):
        h = a_ref[:, t, :] * h + b_ref[:, t, :]
        b_ref[:, t, :] = h
        return h

    h = lax.fori_loop(0, ts, step, h_ref[...], unroll=8)
    h_ref[...] = h
    y_ref[...] = b_ref[...] * g_ref[...]

    @pl.when(i == pl.num_programs(0) - 1)
    def _():
        hlast_ref[...] = h


def _lru_core(xb, g, cw, cb, wa, ba, wx, bx, lam, ts=64):
    nb, s, c = xb.shape
    assert s % ts == 0 and ts % 8 == 0
    blk = pl.BlockSpec((nb, ts, c), lambda i: (0, i, 0))
    row = _const_spec((1, c))
    return pl.pallas_call(
        functools.partial(_lru_core_kernel, ts=ts, nb=nb), grid=(s // ts,),
        in_specs=[blk, blk, _const_spec((CONV_W, c)), row, _const_spec((c, c)), row,
                  _const_spec((c, c)), row, row],
        out_specs=[blk, pl.BlockSpec((nb, c), lambda i: (0, 0))],
        out_shape=[jax.ShapeDtypeStruct((nb, s, c), f32), jax.ShapeDtypeStruct((nb, c), f32)],
        scratch_shapes=[pltpu.VMEM((nb, ts + 8, c), f32), pltpu.VMEM((nb, ts, c), f32),
                        pltpu.VMEM((nb, ts, c), f32), pltpu.VMEM((nb, c), f32)],
        compiler_params=_cparams(("arbitrary",)), name="lru_core",
    )(xb, g, cw, cb.reshape(1, c), wa, ba.reshape(1, c), wx, bx.reshape(1, c), lam.reshape(1, c))


def _lru_short_kernel(xb_ref, g_ref, cs_ref, h0_ref, cw_ref, cb_ref, wa_ref, ba_ref, wx_ref, bx_ref,
                      lam_ref, y_ref, hlast_ref, *, s):
    rows = [cs_ref[k] for k in range(CONV_W - 1)] + [xb_ref[t] for t in range(s)]
    sp = -LRU_C * _softplus(-lam_ref[...])
    cw = cw_ref[...]
    h = h0_ref[...]
    for t in range(s):
        xc = cb_ref[...]
        for k in range(CONV_W):
            xc = xc + cw[k:k + 1, :] * rows[t + k]
        a, bt = _lru_gates(xc, wa_ref[...], ba_ref[...], wx_ref[...], bx_ref[...], sp)
        h = a * h + bt
        y_ref[t] = h * g_ref[t]
    hlast_ref[...] = h


def _lru_short(xb, g, cs, h0, cw, cb, wa, ba, wx, bx, lam):
    s, nb, c = xb.shape
    return pl.pallas_call(
        functools.partial(_lru_short_kernel, s=s),
        out_shape=[jax.ShapeDtypeStruct((s, nb, c), f32), jax.ShapeDtypeStruct((nb, c), f32)],
        compiler_params=pltpu.CompilerParams(vmem_limit_bytes=VMEM_LIMIT_V7X), name="lru_short",
    )(xb, g, cs, h0, cw, cb.reshape(1, c), wa, ba.reshape(1, c), wx, bx.reshape(1, c), lam.reshape(1, c))


def _block_diag(w):
    n, k, _ = w.shape
    eye = jnp.eye(n, dtype=w.dtype)
    return (eye[:, None, :, None] * w[:, :, None, :]).reshape(n * k, n * k)


def _gelu_ep(y):
    return jax.nn.gelu(y)


def _lru_layer(xp, xs, nbp, nbs, conv_s, h_s, ln_g, ln_b, w_in, cw, cb, w_a, b_a, w_x, b_x, lam, w_out):
    c = D_RNN
    w_gate = w_in[:, :c].astype(bf16)
    w_xb = w_in[:, c:].astype(bf16)
    wa = _block_diag(w_a).astype(bf16)
    wx = _block_diag(w_x).astype(bf16)
    wo = w_out.astype(bf16)
    sp_len = xp.shape[0] // nbp
    ss_len = xs.shape[0] // nbs

    gp, xbp = _proj(xp, [w_gate, w_xb], [_gelu_ep, None], tm=256, name="lru_in")
    gs, xbs = _proj(xs, [w_gate, w_xb], [_gelu_ep, None], tm=xs.shape[0], name="lru_in_s")
    xbp3 = xbp.reshape(nbp, sp_len, c)
    yp, hp = _lru_core(xbp3, gp.reshape(nbp, sp_len, c), cw, cb, wa, b_a, wx, b_x, lam)
    xbs3 = xbs.reshape(nbs, ss_len, c)
    ys_t, hs = _lru_short(xbs3.transpose(1, 0, 2), gs.reshape(nbs, ss_len, c).transpose(1, 0, 2),
                          conv_s.transpose(1, 0, 2), h_s, cw, cb, wa, b_a, wx, b_x, lam)
    ys = ys_t.transpose(1, 0, 2).reshape(nbs * ss_len, c)
    xp = _outproj_ln(yp.reshape(nbp * sp_len, c), wo, xp, ln_g, ln_b, name="lru_out")
    xs = _outproj_ln(ys, wo, xs, ln_g, ln_b, tm=xs.shape[0], name="lru_out_s")
    conv_p = xbp3[:, sp_len - (CONV_W - 1):]
    conv_sn = jnp.concatenate([conv_s, xbs3], axis=1)[:, ss_len:]
    return xp, xs, (conv_p, conv_sn, hp, hs)


SSD_GN = SSD_GROUPS * SSD_STATE
SSD_HPG = SSD_HEADS // SSD_GROUPS
SSD_DT_PAD = 128


def _ssd_core_kernel(*refs, q, valid, zero_init):
    if zero_init:
        (xbc_ref, dt_ref, zs_ref, cw_ref, cb_ref, alog_ref, dsk_ref, ng_ref,
         y_ref, sfin_ref, ext_ref, st_ref, yacc_ref) = refs
    else:
        (xbc_ref, dt_ref, zs_ref, hist_ref, s0_ref, cw_ref, cb_ref, alog_ref, dsk_ref, ng_ref,
         y_ref, sfin_ref, ext_ref, st_ref, yacc_ref) = refs
    ci = pl.program_id(1)
    hist = 8

    @pl.when(ci == 0)
    def _():
        if zero_init:
            ext_ref[0:hist, :] = jnp.zeros((hist, ext_ref.shape[1]), f32)
            st_ref[...] = jnp.zeros_like(st_ref)
        else:
            ext_ref[0:hist, :] = hist_ref[...]
            st_ref[...] = s0_ref[...]

    ext_ref[hist:hist + q, :] = xbc_ref[...]
    cw = cw_ref[...]
    xc = cb_ref[...]
    for k in range(CONV_W):
        off = hist - (CONV_W - 1) + k
        xc = xc + cw[k:k + 1, :] * ext_ref[off:off + q, :]
    xc = xc * jax.nn.sigmoid(xc)
    if q >= hist:
        ext_ref[0:hist, :] = ext_ref[q:q + hist, :]

    row = lax.broadcasted_iota(jnp.int32, (q, q), 0)
    col = lax.broadcasted_iota(jnp.int32, (q, q), 1)
    causal = col <= row
    dt = dt_ref[...]
    if valid < q:
        dt = jnp.where(lax.broadcasted_iota(jnp.int32, dt.shape, 0) < valid, dt, 0.0)
    a_neg = -jnp.exp(alog_ref[...])
    tril = jnp.where(causal, 1.0, 0.0)
    acum = jnp.dot(tril, dt * a_neg, precision=lax.Precision.HIGHEST, preferred_element_type=f32)
    acum_t = acum.T
    last = acum[q - 1:q, :]
    dec_out = jnp.exp(last - acum)
    dec_chunk_t = jnp.exp(acum_t[:, q - 1:q])
    e_acum = jnp.exp(acum)

    for g in range(SSD_GROUPS):
        bm = xc[:, SSD_INNER + g * SSD_STATE:SSD_INNER + (g + 1) * SSD_STATE].astype(bf16)
        cm = xc[:, SSD_INNER + SSD_GN + g * SSD_STATE:SSD_INNER + SSD_GN + (g + 1) * SSD_STATE].astype(bf16)
        cb = lax.dot_general(cm, bm, (((1,), (1,)), ((), ())), preferred_element_type=f32)
        for hh in range(SSD_HPG):
            h = g * SSD_HPG + hh
            xh = xc[:, h * SSD_HEAD_DIM:(h + 1) * SSD_HEAD_DIM]
            xdt = xh * dt[:, h:h + 1]
            seg = acum[:, h:h + 1] - acum_t[h:h + 1, :]
            decay = jnp.exp(jnp.where(causal, seg, NEG_INF))
            y_diag = jnp.dot((cb * decay).astype(bf16), xdt.astype(bf16), preferred_element_type=f32)
            s_prev = st_ref[h]
            y_off = lax.dot_general(cm, s_prev.astype(bf16), (((1,), (1,)), ((), ())),
                                    preferred_element_type=f32) * e_acum[:, h:h + 1]
            s_new = lax.dot_general((xdt * dec_out[:, h:h + 1]).astype(bf16), bm,
                                    (((0,), (0,)), ((), ())), preferred_element_type=f32)
            st_ref[h] = dec_chunk_t[h:h + 1, :] * s_prev + s_new
            yacc_ref[:, h * SSD_HEAD_DIM:(h + 1) * SSD_HEAD_DIM] = y_diag + y_off

    y = yacc_ref[...] + dsk_ref[...] * xc[:, :SSD_INNER]
    yg = y * zs_ref[...]
    gw = SSD_INNER // SSD_GROUPS
    for g in range(SSD_GROUPS):
        v = yg[:, g * gw:(g + 1) * gw]
        v = v * lax.rsqrt(jnp.mean(v * v, -1, keepdims=True) + RMS_EPS)
        y_ref[:, g * gw:(g + 1) * gw] = v * ng_ref[:, g * gw:(g + 1) * gw]

    @pl.when(ci == pl.num_programs(1) - 1)
    def _():
        sfin_ref[...] = st_ref[...]


def _ssd_core(xbc, dt, zs, hist, s0, cw, cb, a_log, d_skip, norm_g, q, valid):
    nb, s, cd = xbc.shape
    assert s % q == 0
    zero_init = s0 is None
    blk = lambda w: pl.BlockSpec((None, q, w), lambda b, c: (b, c, 0))
    st_spec = pl.BlockSpec((None, SSD_HEADS, SSD_HEAD_DIM, SSD_STATE), lambda b, c: (b, 0, 0, 0))
    in_specs = [blk(cd), blk(SSD_DT_PAD), blk(SSD_INNER)]
    args = [xbc, dt, zs]
    if not zero_init:
        in_specs += [pl.BlockSpec((None, 8, cd), lambda b, c: (b, 0, 0)), st_spec]
        args += [hist, s0]
    in_specs += [_const_spec((CONV_W, cd)), _const_spec((1, cd)), _const_spec((1, SSD_DT_PAD)),
                 _const_spec((1, SSD_INNER)), _const_spec((1, SSD_INNER))]
    pad_h = SSD_DT_PAD - SSD_HEADS
    args += [cw, cb.reshape(1, cd), jnp.pad(a_log, (0, pad_h)).reshape(1, SSD_DT_PAD),
             jnp.repeat(d_skip, SSD_HEAD_DIM).reshape(1, SSD_INNER), norm_g.reshape(1, SSD_INNER)]
    return pl.pallas_call(
        functools.partial(_ssd_core_kernel, q=q, valid=valid, zero_init=zero_init),
        grid=(nb, s // q), in_specs=in_specs,
        out_specs=[blk(SSD_INNER), st_spec],
        out_shape=[jax.ShapeDtypeStruct((nb, s, SSD_INNER), f32),
                   jax.ShapeDtypeStruct((nb, SSD_HEADS, SSD_HEAD_DIM, SSD_STATE), f32)],
        scratch_shapes=[pltpu.VMEM((q + 8, cd), f32), pltpu.VMEM((SSD_HEADS, SSD_HEAD_DIM, SSD_STATE), f32),
                        pltpu.VMEM((q, SSD_INNER), f32)],
        compiler_params=_cparams(("parallel", "arbitrary")), name="ssd_core",
    )(*args)


def _silu_ep(y, *_):
    return y * jax.nn.sigmoid(y)


def _dt_ep(y, bias):
    return _softplus(y + bias)


def _pass_ep(y, *_):
    return y


def _ssd_layer(xp, xs, nbp, nbs, conv_s, st_s, ln_g, ln_b, w_in, cw, cb, dt_bias, a_log, d_skip, norm_g, w_out):
    cd = SSD_CONV_DIM
    pad_h = SSD_DT_PAD - SSD_HEADS
    w_z = w_in[:, :SSD_INNER].astype(bf16)
    w_xbc = w_in[:, SSD_INNER:SSD_INNER + cd].astype(bf16)
    w_dt = jnp.pad(w_in[:, SSD_INNER + cd:], ((0, 0), (0, pad_h))).astype(bf16)
    bias = jnp.pad(dt_bias, (0, pad_h)).reshape(1, SSD_DT_PAD)
    wo = w_out.astype(bf16)
    sp_len = xp.shape[0] // nbp
    ss_len = xs.shape[0] // nbs
    eps = [_silu_ep, _pass_ep, _dt_ep]

    zsp, xbcp, dtp = _proj(xp, [w_z, w_xbc, w_dt], eps, aux=(bias,), aux_period=(0,), tm=256, name="ssd_in")
    zss, xbcs, dts = _proj(xs, [w_z, w_xbc, w_dt], eps, aux=(bias,), aux_period=(0,), tm=xs.shape[0], name="ssd_in_s")
    xbcp3 = xbcp.reshape(nbp, sp_len, cd)
    yp, sp_fin = _ssd_core(xbcp3, dtp.reshape(nbp, sp_len, SSD_DT_PAD), zsp.reshape(nbp, sp_len, SSD_INNER),
                           None, None, cw, cb, a_log, d_skip, norm_g, q=math.gcd(sp_len, SSD_CHUNK), valid=SSD_CHUNK)
    qs = 8
    assert ss_len <= qs
    tpad = lambda a: jnp.pad(a, ((0, 0), (0, qs - ss_len), (0, 0)))
    xbcs3 = xbcs.reshape(nbs, ss_len, cd)
    hist = jnp.pad(conv_s, ((0, 0), (8 - (CONV_W - 1), 0), (0, 0)))
    ys, ss_fin = _ssd_core(tpad(xbcs3), tpad(dts.reshape(nbs, ss_len, SSD_DT_PAD)),
                           tpad(zss.reshape(nbs, ss_len, SSD_INNER)), hist, st_s,
                           cw, cb, a_log, d_skip, norm_g, q=qs, valid=ss_len)
    ys = ys[:, :ss_len].reshape(nbs * ss_len, SSD_INNER)
    xp = _outproj_ln(yp.reshape(nbp * sp_len, SSD_INNER), wo, xp, ln_g, ln_b, name="ssd_out")
    xs = _outproj_ln(ys, wo, xs, ln_g, ln_b, tm=xs.shape[0], name="ssd_out_s")
    conv_p = xbcp3[:, sp_len - (CONV_W - 1):]
    conv_sn = jnp.concatenate([conv_s, xbcs3], axis=1)[:, ss_len:]
    return xp, xs, (conv_p, conv_sn, sp_fin, ss_fin)


NSA_TILE = 128
NSA_SCALE = NSA_HD ** -0.5


def _rope_tables(pos):
    inv = ROPE_THETA ** (-jnp.arange(0, NSA_HD, 2, dtype=f32) / NSA_HD)
    ang = pos[:, None] * inv[None, :]
    cos, sin = jnp.cos(ang), jnp.sin(ang)
    return jnp.tile(cos, (1, 4)), jnp.tile(jnp.concatenate([-sin, sin], axis=1), (1, 2))


def _rope_cols(y, cos, sgn_sin, blocks):
    lane = lax.broadcasted_iota(jnp.int32, (y.shape[0], 128), 1)
    first_half = (lane % NSA_HD) < NSA_HD // 2
    out = []
    for c in range(y.shape[1] // 128):
        blk = y[:, 128 * c:128 * (c + 1)]
        if c in blocks:
            partner = jnp.where(first_half, pltpu.roll(blk, 128 - NSA_HD // 2, 1), pltpu.roll(blk, NSA_HD // 2, 1))
            blk = blk * cos + partner * sgn_sin
        out.append(blk)
    return jnp.concatenate(out, axis=1)


def _q_ep(y, cos, sin):
    return _rope_cols(y, cos, sin, range(NSA_HQ // 128)) * NSA_SCALE


def _rows_ep(y, cos, sin):
    return _rope_cols(y, cos, sin, (4, 5))


def _wrows_ep(y, cos, sin):
    return _rope_cols(y, cos, sin, (0, 1))


def _sigmoid_ep(y, *_):
    return jax.nn.sigmoid(y)


def _nsa_cmp_kernel(pt_ref, pa_ref, pb_ref, cos_ref, sin_ref, o_ref):
    del pt_ref
    per_page = NSA_TILE // CMP_BLOCK
    for n, ref in enumerate((pa_ref, pb_ref)):
        m = ref[...].reshape(per_page, CMP_BLOCK, ref.shape[1]).sum(1) * (1.0 / CMP_BLOCK)
        lo, hi = n * per_page, (n + 1) * per_page
        o_ref[lo:hi, :] = _rope_cols(m, cos_ref[lo:hi, :], sin_ref[lo:hi, :], (0, 1))


def _nsa_cmp(pool, page_table, cos, sin):
    nb, n_pages = page_table.shape
    assert n_pages % 2 == 0
    w = 2 * NSA_HK
    per_step = 2 * NSA_TILE // CMP_BLOCK
    grid_spec = pltpu.PrefetchScalarGridSpec(
        num_scalar_prefetch=1, grid=(nb, n_pages // 2),
        in_specs=[pl.BlockSpec((None, NSA_TILE, w), lambda b, p, pt: (pt[b, 2 * p], 0, 0)),
                  pl.BlockSpec((None, NSA_TILE, w), lambda b, p, pt: (pt[b, 2 * p + 1], 0, 0)),
                  pl.BlockSpec((per_step, 128), lambda b, p, pt: (p, 0)),
                  pl.BlockSpec((per_step, 128), lambda b, p, pt: (p, 0))],
        out_specs=pl.BlockSpec((None, per_step, w), lambda b, p, pt: (b, p, 0)))
    return pl.pallas_call(
        _nsa_cmp_kernel, grid_spec=grid_spec,
        out_shape=jax.ShapeDtypeStruct((nb, n_pages * NSA_TILE // CMP_BLOCK, w), f32),
        compiler_params=_cparams(("parallel", "arbitrary")), name="nsa_cmp",
    )(page_table, pool, pool, cos, sin)


def _even_odd(cmp):
    nb, nc, w = cmp.shape
    return cmp.reshape(nb, nc // 2, 2, w).transpose(0, 2, 1, 3).reshape(nb, nc, w)


def _topk_mask_rows(score_t, k):
    n = score_t.shape[0]
    assert n % 8 == 0
    pieces = [score_t[8 * v:8 * v + 8] for v in range(n // 8)]
    ridx = lax.broadcasted_iota(jnp.int32, pieces[0].shape, 0)
    cnts = [jnp.zeros(pieces[0].shape, f32) for _ in pieces]
    for i in range(n):
        si = score_t[i:i + 1, :]
        for v, pc in enumerate(pieces):
            if 8 * v > i:
                cnts[v] = cnts[v] + jnp.where(si >= pc, 1.0, 0.0)
            elif 8 * v + 7 < i:
                cnts[v] = cnts[v] + jnp.where(si > pc, 1.0, 0.0)
            else:
                tie_wins = jnp.where(ridx + 8 * v > i, 1.0, 0.0)
                cnts[v] = cnts[v] + jnp.where(si > pc, 1.0, 0.0) + jnp.where(si == pc, tie_wins, 0.0)
    return jnp.concatenate([jnp.where(c < k, 1.0, 0.0) for c in cnts], axis=0)


def _topk_mask_lanes(score, k, n_real):
    lane = lax.broadcasted_iota(jnp.int32, score.shape, 1)
    cnt = jnp.zeros(score.shape, f32)
    for i in range(n_real):
        si = score[:, i:i + 1]
        cnt = cnt + jnp.where(si > score, 1.0, 0.0) + jnp.where(si == score, jnp.where(lane > i, 1.0, 0.0), 0.0)
    return jnp.where(cnt < k, 1.0, 0.0)


def _softmax_rows(s):
    e = jnp.exp(s - jnp.max(s, -1, keepdims=True))
    return e / jnp.sum(e, -1, keepdims=True)


def _flash_step(qk, kt, vt, ok, carry):
    m, l, acc = carry
    s = lax.dot_general(qk, kt, (((1,), (1,)), ((), ())), preferred_element_type=f32)
    s = jnp.where(ok, s, NEG_INF)
    m_new = jnp.maximum(m, jnp.max(s, -1, keepdims=True))
    alpha = jnp.exp(m - m_new)
    p = jnp.exp(s - m_new)
    l = alpha * l + jnp.sum(p, -1, keepdims=True)
    acc = alpha * acc + jnp.dot(p.astype(bf16), vt, preferred_element_type=f32)
    return m_new, l, acc


def _flash_init(rows):
    return (jnp.full((rows, 1), NEG_INF, f32), jnp.zeros((rows, 1), f32), jnp.zeros((rows, NSA_HD), f32))


def _rope_rows(y, cos_t, sin_t, heads):
    half = NSA_HD // 2
    out = []
    for h in range(heads):
        top = y[NSA_HD * h:NSA_HD * h + half]
        bot = y[NSA_HD * h + half:NSA_HD * (h + 1)]
        out += [top * cos_t - bot * sin_t, bot * cos_t + top * sin_t]
    return jnp.concatenate(out, axis=0)


def _nsa_in_kernel(x_ref, wn_ref, wq_ref, wr_ref, ww_ref, wg_ref, cos_ref, sin_ref, cost_ref, sint_ref,
                   nat_ref, qt_ref, rt_ref, wt_ref, gt_ref):
    x = x_ref[...].astype(bf16)
    nat = jnp.dot(x, wn_ref[...], preferred_element_type=f32)
    nat_ref[...] = _rope_cols(nat, cos_ref[...], sin_ref[...], (4, 5, 6, 7))
    nt = (((1,), (1,)), ((), ()))
    cos_t, sin_t = cost_ref[...], sint_ref[...]
    q_t = lax.dot_general(wq_ref[...], x, nt, preferred_element_type=f32)
    qt_ref[...] = _rope_rows(q_t, cos_t, sin_t, NSA_HEADS) * NSA_SCALE
    r_t = lax.dot_general(wr_ref[...], x, nt, preferred_element_type=f32)
    rt_ref[0:2 * NSA_HK, :] = r_t[0:2 * NSA_HK]
    rt_ref[2 * NSA_HK:3 * NSA_HK, :] = _rope_rows(r_t[2 * NSA_HK:3 * NSA_HK], cos_t, sin_t, NSA_KV_HEADS)
    rt_ref[3 * NSA_HK:4 * NSA_HK, :] = r_t[3 * NSA_HK:4 * NSA_HK]
    w_t = lax.dot_general(ww_ref[...], x, nt, preferred_element_type=f32)
    wt_ref[0:NSA_HK, :] = _rope_rows(w_t[0:NSA_HK], cos_t, sin_t, NSA_KV_HEADS)
    wt_ref[NSA_HK:2 * NSA_HK, :] = w_t[NSA_HK:2 * NSA_HK]
    gt_ref[...] = jax.nn.sigmoid(lax.dot_general(wg_ref[...], x, nt, preferred_element_type=f32))


def _nsa_in(x, nb, w_in, tm=256):
    m, d = x.shape
    s_len = m // nb
    assert s_len % tm == 0
    hq, hk = NSA_HQ, NSA_HK
    w_q = w_in[:, :hq]
    w_kv = w_in[:, hq:hq + 6 * hk].reshape(d, 6, hk)
    w_g = w_in[:, hq + 6 * hk:]
    w_nat = w_kv[:, jnp.array([0, 1, 2, 4])].reshape(d, 4 * hk).astype(bf16)
    w_q_t = w_q.T.astype(bf16)
    w_rows_t = w_kv[:, 0:4].reshape(d, 4 * hk).T.astype(bf16)
    w_win_t = w_kv[:, 4:6].reshape(d, 2 * hk).T.astype(bf16)
    w_g_t = w_g.T.astype(bf16)
    pos = jnp.arange(s_len, dtype=f32)
    cos, sin = _rope_tables(pos)
    inv = ROPE_THETA ** (-jnp.arange(0, NSA_HD, 2, dtype=f32) / NSA_HD)
    ang_t = inv[:, None] * pos[None, :]
    nt = s_len // tm
    ng = 3 * NSA_HEADS
    t_spec = lambda r: pl.BlockSpec((None, r, tm), lambda b, i: (b, 0, i))
    tab = pl.BlockSpec((tm, 128), lambda b, i: (i, 0))
    tab_t = pl.BlockSpec((NSA_HD // 2, tm), lambda b, i: (0, i))
    return pl.pallas_call(
        _nsa_in_kernel, grid=(nb, nt),
        in_specs=[pl.BlockSpec((tm, d), lambda b, i: (b * nt + i, 0)),
                  _const_spec(w_nat.shape), _const_spec(w_q_t.shape), _const_spec(w_rows_t.shape),
                  _const_spec(w_win_t.shape), _const_spec(w_g_t.shape), tab, tab, tab_t, tab_t],
        out_specs=[pl.BlockSpec((tm, 4 * hk), lambda b, i: (b * nt + i, 0)), t_spec(hq), t_spec(4 * hk),
                   t_spec(2 * hk), t_spec(ng)],
        out_shape=[jax.ShapeDtypeStruct((m, 4 * hk), f32), jax.ShapeDtypeStruct((nb, hq, s_len), f32),
                   jax.ShapeDtypeStruct((nb, 4 * hk, s_len), f32), jax.ShapeDtypeStruct((nb, 2 * hk, s_len), f32),
                   jax.ShapeDtypeStruct((nb, ng, s_len), f32)],
        compiler_params=_cparams(("parallel", "parallel")), name="nsa_in",
    )(x, w_nat, w_q_t, w_rows_t, w_win_t, w_g_t, cos, sin, jnp.cos(ang_t), jnp.sin(ang_t))


def _flash_cols(k_tile, v_t, q_ref, bias, m_ref, l_ref, acc_ref):
    qw = bias.shape[1]
    for g in range(q_ref.shape[1] // qw):
        c = slice(g * qw, (g + 1) * qw)
        s = jnp.dot(k_tile, q_ref[:, c], preferred_element_type=f32) + bias
        m_old = m_ref[:, c]
        m_new = jnp.maximum(m_old, jnp.max(s, 0, keepdims=True))
        alpha = jnp.exp(m_old - m_new)
        p = jnp.exp(s - m_new)
        l_ref[:, c] = alpha * l_ref[:, c] + jnp.sum(p, 0, keepdims=True)
        m_ref[:, c] = m_new
        acc_ref[:, c] = alpha * acc_ref[:, c] + jnp.dot(v_t, p.astype(bf16), preferred_element_type=f32)


def _nsa_seq_kernel(qt_ref, gt_ref, kcb_ref, vcbt_ref, knat_ref, vst_ref, vwt_ref, o_ref,
                    q_sc, sel_sc, oc_sc, os_sc, m_sc, l_sc, acc_sc, *, s_len):
    i = pl.program_id(1)
    qb = NSA_TILE
    nc = s_len // CMP_BLOCK
    nsb = s_len // SEL_BLOCK
    cols = NSA_GROUP * qb
    tile4 = lambda a: jnp.concatenate([a] * NSA_GROUP, axis=1)
    t_q = i * qb + lax.broadcasted_iota(jnp.int32, (1, qb), 1)
    t_cols = tile4(t_q)
    cl = lax.broadcasted_iota(jnp.int32, (nc, 1), 0)
    cblk = jnp.where(cl < nc // 2, 2 * cl, 2 * (cl - nc // 2) + 1)
    cmask = (cblk + 1) * CMP_BLOCK - 1 <= t_cols
    has_cmp = jnp.where(t_cols >= CMP_BLOCK - 1, 1.0, 0.0)
    blk = lax.broadcasted_iota(jnp.int32, (nsb, 1), 0)
    valid = blk * SEL_BLOCK <= t_q
    forced = (blk == 0) | (blk == t_q // SEL_BLOCK)
    key_in = lax.broadcasted_iota(jnp.int32, (NSA_TILE, 1), 0)
    blk_of_key = lax.broadcasted_iota(jnp.int32, (NSA_TILE, nsb), 0) // SEL_BLOCK
    blk_lane = lax.broadcasted_iota(jnp.int32, (NSA_TILE, nsb), 1)
    gates = gt_ref[...]
    kcols = [slice(NSA_HD * k, NSA_HD * (k + 1)) for k in range(NSA_KV_HEADS)]
    wcols = [slice(NSA_HK + NSA_HD * k, NSA_HK + NSA_HD * (k + 1)) for k in range(NSA_KV_HEADS)]

    def reset():
        m_sc[...] = jnp.full(m_sc.shape, NEG_INF, f32)
        l_sc[...] = jnp.zeros_like(l_sc)
        acc_sc[...] = jnp.zeros_like(acc_sc)

    for k in range(NSA_KV_HEADS):
        kcol = kcols[k]
        q_t = jnp.concatenate([qt_ref[NSA_HD * (NSA_GROUP * k + g):NSA_HD * (NSA_GROUP * k + g + 1), :]
                               for g in range(NSA_GROUP)], axis=1).astype(bf16)
        q_sc[k] = q_t

        s_c = jnp.dot(kcb_ref[:, kcol].astype(bf16), q_t, preferred_element_type=f32)
        s_c = jnp.where(cmask, s_c, NEG_INF)
        e = jnp.exp(s_c - jnp.max(s_c, 0, keepdims=True))
        p_c = e / jnp.sum(e, 0, keepdims=True) * has_cmp
        o_c = jnp.dot(vcbt_ref[kcol, :].astype(bf16), p_c.astype(bf16), preferred_element_type=f32)

        imp = p_c[:, 0:qb]
        for g in range(1, NSA_GROUP):
            imp = imp + p_c[:, g * qb:(g + 1) * qb]
        imp = imp[:nc // 2] + imp[nc // 2:]
        score = jnp.where(forced, FORCE_SCORE, jnp.where(valid, imp, -1.0))
        sel_sc[k] = _topk_mask_rows(score, SEL_TOPK).astype(bf16)
        oc_sc[k] = o_c

    def sel_body(j, carry):
        off = pl.multiple_of(j * NSA_TILE, NSA_TILE)
        keys = pl.ds(off, NSA_TILE)
        expand = jnp.where(blk_lane == blk_of_key + j * (NSA_TILE // SEL_BLOCK), 1.0, 0.0).astype(bf16)
        causal = off + key_in <= t_q
        for k in range(NSA_KV_HEADS):
            chosen = jnp.dot(expand, sel_sc[k], preferred_element_type=f32)
            bias = jnp.where((chosen > 0.5) & causal, 0.0, NEG_INF)
            _flash_cols(knat_ref[keys, kcols[k]].astype(bf16), vst_ref[kcols[k], keys].astype(bf16),
                        q_sc.at[k], bias, m_sc.at[k], l_sc.at[k], acc_sc.at[k])
        return carry

    reset()
    lax.fori_loop(0, i + 1, sel_body, 0)
    os_sc[...] = acc_sc[...] / l_sc[...]

    def win_body(j, carry):
        off = pl.multiple_of(j * NSA_TILE, NSA_TILE)
        keys = pl.ds(off, NSA_TILE)
        kpos = off + key_in
        bias = jnp.where((kpos <= t_q) & (kpos > t_q - WINDOW), 0.0, NEG_INF)
        for k in range(NSA_KV_HEADS):
            _flash_cols(knat_ref[keys, wcols[k]].astype(bf16), vwt_ref[kcols[k], keys].astype(bf16),
                        q_sc.at[k], bias, m_sc.at[k], l_sc.at[k], acc_sc.at[k])
        return carry

    reset()
    lax.fori_loop(jnp.maximum(i - WINDOW // NSA_TILE, 0), i + 1, win_body, 0)

    for k in range(NSA_KV_HEADS):
        o_c = oc_sc[k]
        o_s = os_sc[k]
        o_w = acc_sc[k] / l_sc[k]
        for pair in range(NSA_GROUP // 2):
            o_t = []
            for g in (2 * pair, 2 * pair + 1):
                h = NSA_GROUP * k + g
                c = slice(g * qb, (g + 1) * qb)
                o_t.append(gates[3 * h:3 * h + 1, :] * o_c[:, c] + gates[3 * h + 1:3 * h + 2, :] * o_s[:, c]
                           + gates[3 * h + 2:3 * h + 3, :] * o_w[:, c])
            lane0 = NSA_HD * (NSA_GROUP * k + 2 * pair)
            o_ref[:, lane0:lane0 + 2 * NSA_HD] = jnp.concatenate(o_t, axis=0).T


def _nsa_seq(q_t, gates_t, kcb, vcb_t, nat, rows_t, wrows_t):
    nb, _, s_len = q_t.shape
    assert s_len % NSA_TILE == 0 and (s_len // CMP_BLOCK) % 2 == 0 and (s_len // SEL_BLOCK) % 8 == 0
    nc = s_len // CMP_BLOCK
    hk = NSA_HK
    cols = NSA_GROUP * NSA_TILE
    per_b = lambda shape, idx: pl.BlockSpec((None,) + shape, lambda b, i: (b,) + idx)
    return pl.pallas_call(
        functools.partial(_nsa_seq_kernel, s_len=s_len), grid=(nb, s_len // NSA_TILE),
        in_specs=[pl.BlockSpec((None, NSA_HQ, NSA_TILE), lambda b, i: (b, 0, i)),
                  pl.BlockSpec((None, 3 * NSA_HEADS, NSA_TILE), lambda b, i: (b, 0, i)),
                  per_b((nc, hk), (0, 0)), per_b((hk, nc), (0, 0)),
                  per_b((s_len, 2 * hk), (0, 1)),
                  per_b((hk, s_len), (3, 0)),
                  per_b((hk, s_len), (1, 0))],
        out_specs=pl.BlockSpec((None, NSA_TILE, NSA_HQ), lambda b, i: (b, i, 0)),
        out_shape=jax.ShapeDtypeStruct((nb, s_len, NSA_HQ), f32),
        scratch_shapes=[pltpu.VMEM((NSA_KV_HEADS, NSA_HD, cols), bf16),
                        pltpu.VMEM((NSA_KV_HEADS, s_len // SEL_BLOCK, NSA_TILE), bf16),
                        pltpu.VMEM((NSA_KV_HEADS, NSA_HD, cols), f32), pltpu.VMEM((NSA_KV_HEADS, NSA_HD, cols), f32),
                        pltpu.VMEM((NSA_KV_HEADS, 1, cols), f32), pltpu.VMEM((NSA_KV_HEADS, 1, cols), f32),
                        pltpu.VMEM((NSA_KV_HEADS, NSA_HD, cols), f32)],
        compiler_params=_cparams(("parallel", "arbitrary")), name="nsa_seq",
    )(q_t, gates_t, kcb, vcb_t, nat, rows_t, wrows_t)


DEC_CHUNK_PAGES = 8


def _flash_step_t(qk, k_t, v_t, ok, carry):
    m, l, acc = carry
    s = jnp.dot(qk, k_t, preferred_element_type=f32)
    s = jnp.where(ok, s, NEG_INF)
    m_new = jnp.maximum(m, jnp.max(s, -1, keepdims=True))
    alpha = jnp.exp(m - m_new)
    p = jnp.exp(s - m_new)
    l = alpha * l + jnp.sum(p, -1, keepdims=True)
    acc = alpha * acc + lax.dot_general(p.astype(bf16), v_t, (((1,), (1,)), ((), ())), preferred_element_type=f32)
    return m_new, l, acc


def _nsa_dec_kernel(pt_ref, q_ref, g_ref, cos_ref, sin_ref, pool_hbm, new_ref, win_ref, wnew_ref, o_ref,
                    buf, sems, cmp_sc, sel_sc, m_sc, l_sc, acc_sc, *, past_len, n_new):
    b = pl.program_id(0)
    rows = NSA_GROUP * n_new
    t_len = past_len + n_new
    nc = t_len // CMP_BLOCK
    nsb = -(-t_len // SEL_BLOCK)
    n_pages = past_len // NSA_TILE
    cpg = DEC_CHUNK_PAGES
    n_chunks = n_pages // cpg
    ckeys = cpg * NSA_TILE
    new_pad = new_ref.shape[0]
    qi = lax.broadcasted_iota(jnp.int32, (rows, 1), 0) % n_new
    t_rows = past_len + qi

    def page_copy(c, i):
        half = c // n_chunks
        page = pt_ref[b, (c % n_chunks) * cpg + i]
        return pltpu.make_async_copy(pool_hbm.at[page, half], buf.at[c % 2, :, pl.ds(i * NSA_TILE, NSA_TILE)],
                                     sems.at[c % 2])

    def start_chunk(c):
        for i in range(cpg):
            page_copy(c, i).start()

    def wait_chunk(c):
        for i in range(cpg):
            page_copy(c, i).wait()

    start_chunk(0)
    cmp_sc[...] = jnp.zeros_like(cmp_sc)

    def pass1(c, carry):
        start_chunk(c + 1)
        wait_chunk(c)
        x = buf[c % 2]
        rr = lax.broadcasted_iota(jnp.int32, (ckeys, nc), 0)
        ll = lax.broadcasted_iota(jnp.int32, (ckeys, nc), 1)
        blk = c * (ckeys // CMP_BLOCK) + rr // CMP_BLOCK
        lane_of_blk = jnp.where(blk % 2 == 0, blk // 2, nc // 2 + blk // 2)
        avg = jnp.where(ll == lane_of_blk, 1.0 / CMP_BLOCK, 0.0).astype(bf16)
        hi = x.astype(bf16)
        r1 = x - hi.astype(f32)
        mid = r1.astype(bf16)
        lo = (r1 - mid.astype(f32)).astype(bf16)
        cmp_sc[...] += (jnp.dot(hi, avg, preferred_element_type=f32) + jnp.dot(mid, avg, preferred_element_type=f32)
                        + jnp.dot(lo, avg, preferred_element_type=f32))
        return carry

    lax.fori_loop(0, n_chunks, pass1, 0)

    cl = lax.broadcasted_iota(jnp.int32, (1, nc), 1)
    cblk = jnp.where(cl < nc // 2, 2 * cl, 2 * (cl - nc // 2) + 1)
    cmask = (cblk + 1) * CMP_BLOCK - 1 <= t_rows
    rr = lax.broadcasted_iota(jnp.int32, (rows, rows), 0) % n_new
    rc = lax.broadcasted_iota(jnp.int32, (rows, rows), 1) % n_new
    group_sum = jnp.where(rr == rc, 1.0, 0.0)
    blk = lax.broadcasted_iota(jnp.int32, (rows, nc), 1)
    valid = blk * SEL_BLOCK <= t_rows
    forced = (blk == 0) | (blk == t_rows // SEL_BLOCK)
    half_d = NSA_HD // 2
    o_c = []
    for k in range(NSA_KV_HEADS):
        qk = q_ref[k].astype(bf16)
        top = cmp_sc[NSA_HD * k:NSA_HD * k + half_d, :]
        bot = cmp_sc[NSA_HD * k + half_d:NSA_HD * (k + 1), :]
        cos, sin = cos_ref[...], sin_ref[...]
        kcb_t = jnp.concatenate([top * cos - bot * sin, bot * cos + top * sin], axis=0).astype(bf16)
        vcb_t = cmp_sc[NSA_HK + NSA_HD * k:NSA_HK + NSA_HD * (k + 1), :].astype(bf16)
        s_c = jnp.dot(qk, kcb_t, preferred_element_type=f32)
        p_c = _softmax_rows(jnp.where(cmask, s_c, NEG_INF))
        p_c = p_c * jnp.where(t_rows >= CMP_BLOCK - 1, 1.0, 0.0)
        o_c.append(lax.dot_general(p_c.astype(bf16), vcb_t, (((1,), (1,)), ((), ())), preferred_element_type=f32))
        imp = jnp.dot(group_sum, p_c, precision=lax.Precision.HIGHEST, preferred_element_type=f32)
        imp = imp[:, :nc // 2] + imp[:, nc // 2:]
        imp = jnp.concatenate([imp, jnp.zeros((rows, nc - nc // 2), f32)], axis=1)
        score = jnp.where(forced, FORCE_SCORE, jnp.where(valid, imp, -1.0))
        score = jnp.where(blk < nsb, score, -2.0)
        sel_sc[k] = _topk_mask_lanes(score, SEL_TOPK, nsb)
        m_sc[k] = jnp.full((rows, 1), NEG_INF, f32)
        l_sc[k] = jnp.zeros((rows, 1), f32)
        acc_sc[k] = jnp.zeros((rows, NSA_HD), f32)

    def pass2(c, carry):
        @pl.when(c + 1 < 2 * n_chunks)
        def _():
            start_chunk(c + 1)

        wait_chunk(c)
        key0 = (c - n_chunks) * ckeys
        eb = lax.broadcasted_iota(jnp.int32, (nc, ckeys), 0)
        ek = (key0 + lax.broadcasted_iota(jnp.int32, (nc, ckeys), 1)) // SEL_BLOCK
        expand = jnp.where(eb == ek, 1.0, 0.0).astype(bf16)
        kpos = key0 + lax.broadcasted_iota(jnp.int32, (1, ckeys), 1)
        for k in range(NSA_KV_HEADS):
            qk = q_ref[k].astype(bf16)
            k_t = buf[c % 2, NSA_HD * k:NSA_HD * (k + 1), :].astype(bf16)
            v_t = buf[c % 2, NSA_HK + NSA_HD * k:NSA_HK + NSA_HD * (k + 1), :].astype(bf16)
            msk = jnp.dot(sel_sc[k].astype(bf16), expand, preferred_element_type=f32)
            ok = (msk > 0.5) & (kpos <= t_rows)
            m_sc[k], l_sc[k], acc_sc[k] = _flash_step_t(qk, k_t, v_t, ok, (m_sc[k], l_sc[k], acc_sc[k]))
        return carry

    lax.fori_loop(n_chunks, 2 * n_chunks, pass2, 0)

    jn = lax.broadcasted_iota(jnp.int32, (1, new_pad), 1)
    npos = past_len + jn
    wpos = past_len - WINDOW + lax.broadcasted_iota(jnp.int32, (1, WINDOW), 1)
    for k in range(NSA_KV_HEADS):
        qk = q_ref[k].astype(bf16)
        kc = slice(NSA_HD * k, NSA_HD * (k + 1))
        vc = slice(NSA_HK + NSA_HD * k, NSA_HK + NSA_HD * (k + 1))
        in_last = sel_sc[k][:, nsb - 1:nsb] > 0.5
        ok = in_last & (npos <= t_rows) & (jn < n_new)
        _, l_s, acc_s = _flash_step(qk, new_ref[:, kc].astype(bf16), new_ref[:, vc].astype(bf16), ok,
                                    (m_sc[k], l_sc[k], acc_sc[k]))
        o_s = acc_s / l_s
        ok1 = (wpos <= t_rows) & (wpos > t_rows - WINDOW)
        carry = _flash_step_t(qk, win_ref[kc, :].astype(bf16), win_ref[vc, :].astype(bf16), ok1, _flash_init(rows))
        ok2 = (npos <= t_rows) & (npos > t_rows - WINDOW) & (jn < n_new)
        _, l_w, acc_w = _flash_step(qk, wnew_ref[:, kc].astype(bf16), wnew_ref[:, vc].astype(bf16), ok2, carry)
        o_w = acc_w / l_w
        g = g_ref[k]
        o_ref[k] = g[:, 0:1] * o_c[k] + g[:, 1:2] * o_s + g[:, 2:3] * o_w


def _nsa_dec(q, gates, cos_t, sin_t, pool_v, page_table, new_rows, win_t, new_wrows, past_len, n_new):
    nb, n_pages = page_table.shape
    rows = NSA_GROUP * n_new
    w = 2 * NSA_HK
    nc = (past_len + n_new) // CMP_BLOCK
    assert past_len == n_pages * NSA_TILE and past_len % SEL_BLOCK == 0 and n_new <= CMP_BLOCK
    assert win_t.shape[2] == WINDOW and past_len >= WINDOW and n_pages % DEC_CHUNK_PAGES == 0
    assert nc == past_len // CMP_BLOCK and nc % 2 == 0 and -(-(past_len + n_new) // SEL_BLOCK) <= nc
    assert pool_v.shape[1:] == (2, w, NSA_TILE) and cos_t.shape == (NSA_HD // 2, nc)
    per_b = lambda shape: pl.BlockSpec((None,) + shape, lambda b, pt: (b,) + (0,) * len(shape))
    const = lambda shape: pl.BlockSpec(shape, lambda b, pt: (0,) * len(shape))
    grid_spec = pltpu.PrefetchScalarGridSpec(
        num_scalar_prefetch=1, grid=(nb,),
        in_specs=[per_b((NSA_KV_HEADS, rows, NSA_HD)), per_b((NSA_KV_HEADS, rows, 3)),
                  const(cos_t.shape), const(sin_t.shape), pl.BlockSpec(memory_space=pl.ANY),
                  pl.BlockSpec((None, new_rows.shape[1], w), lambda b, pt: (b, 0, 1)),
                  per_b((w, WINDOW)), per_b((new_wrows.shape[1], w))],
        out_specs=per_b((NSA_KV_HEADS, rows, NSA_HD)),
        scratch_shapes=[pltpu.VMEM((2, w, DEC_CHUNK_PAGES * NSA_TILE), f32), pltpu.SemaphoreType.DMA((2,)),
                        pltpu.VMEM((w, nc), f32), pltpu.VMEM((NSA_KV_HEADS, rows, nc), f32),
                        pltpu.VMEM((NSA_KV_HEADS, rows, 1), f32), pltpu.VMEM((NSA_KV_HEADS, rows, 1), f32),
                        pltpu.VMEM((NSA_KV_HEADS, rows, NSA_HD), f32)])
    return pl.pallas_call(
        functools.partial(_nsa_dec_kernel, past_len=past_len, n_new=n_new), grid_spec=grid_spec,
        out_shape=jax.ShapeDtypeStruct((nb, NSA_KV_HEADS, rows, NSA_HD), f32),
        compiler_params=_cparams(("arbitrary",)), name="nsa_dec",
    )(page_table, q, gates, cos_t, sin_t, pool_v, new_rows, win_t, new_wrows)


def _nsa_layer(xp, xs, nbp, nbs, pool, win, page_table, ln_g, ln_b, w_in, w_out):
    hq, hk = NSA_HQ, NSA_HK
    ws = [w_in[:, :hq].astype(bf16), w_in[:, hq:hq + 4 * hk].astype(bf16),
          w_in[:, hq + 4 * hk:hq + 6 * hk].astype(bf16), w_in[:, hq + 6 * hk:].astype(bf16)]
    wo = w_out.astype(bf16)
    eps = [_q_ep, _rows_ep, _wrows_ep, _sigmoid_ep]
    sp_len = xp.shape[0] // nbp
    ss_len = xs.shape[0] // nbs
    n_pages = page_table.shape[1]
    page = pool.shape[1]
    assert page == NSA_TILE
    past_len = n_pages * page
    ms = xs.shape[0]

    cos_s, sin_s = _rope_tables(jnp.tile(past_len + jnp.arange(ss_len, dtype=f32), nbs))
    nat_p, q_t, rows_t, wrows_t, g_t = _nsa_in(xp, nbp, w_in)
    qs, rows_s, wrows_s, gs = _proj(xs, ws, eps, aux=(cos_s, sin_s), aux_period=(ms, ms), tm=ms, name="nsa_in_s")

    def cmp_tables(nc):
        return _rope_tables(jnp.arange(nc, dtype=f32) * CMP_BLOCK + (CMP_BLOCK - 1) / 2.0)

    pages_p = sp_len // page
    pt_p = jnp.arange(nbp * pages_p, dtype=jnp.int32).reshape(nbp, pages_p)
    cmp_p = _even_odd(_nsa_cmp(nat_p.reshape(nbp * pages_p, page, 4 * hk), pt_p, *cmp_tables(sp_len // CMP_BLOCK)))
    op = _nsa_seq(q_t, g_t, cmp_p[:, :, :hk], cmp_p[:, :, hk:].transpose(0, 2, 1),
                  nat_p.reshape(nbp, sp_len, 4 * hk), rows_t, wrows_t)

    pool_v = pool.transpose(0, 2, 3, 4, 1).reshape(pool.shape[0], 2, 2 * hk, page)
    win_t = win.transpose(0, 2, 3, 4, 1).reshape(nbs, 2 * hk, win.shape[1])
    nc_s = past_len // CMP_BLOCK
    cpos = jnp.arange(nc_s, dtype=f32) * CMP_BLOCK + (CMP_BLOCK - 1) / 2.0
    cpos = jnp.concatenate([cpos[0::2], cpos[1::2]])
    inv = ROPE_THETA ** (-jnp.arange(0, NSA_HD, 2, dtype=f32) / NSA_HD)
    ang_t = inv[:, None] * cpos[None, :]
    rows_s3 = rows_s.reshape(nbs, ss_len, 4 * hk)
    wrows_s3 = wrows_s.reshape(nbs, ss_len, 2 * hk)
    new_pad = 8
    tpad = lambda a: jnp.pad(a, ((0, 0), (0, new_pad - ss_len), (0, 0)))
    to_heads = lambda a, last: (a.reshape(nbs, ss_len, NSA_KV_HEADS, NSA_GROUP, last).transpose(0, 2, 3, 1, 4)
                                .reshape(nbs, NSA_KV_HEADS, NSA_GROUP * ss_len, last))
    os_ = _nsa_dec(to_heads(qs, NSA_HD), to_heads(gs, 3), jnp.cos(ang_t), jnp.sin(ang_t), pool_v, page_table,
                   tpad(rows_s3), win_t, tpad(wrows_s3), past_len, ss_len)
    os_ = (os_.reshape(nbs, NSA_KV_HEADS, NSA_GROUP, ss_len, NSA_HD).transpose(0, 3, 1, 2, 4).reshape(ms, hq))

    xp = _outproj_ln(op.reshape(nbp * sp_len, hq), wo, xp, ln_g, ln_b, name="nsa_out")
    xs = _outproj_ln(os_, wo, xs, ln_g, ln_b, tm=ms, name="nsa_out_s")
    kv_shape = (4, NSA_KV_HEADS, NSA_HD)
    win_shape = (2, NSA_KV_HEADS, NSA_HD)
    from_t = lambda a, shape: a.reshape((a.shape[0],) + shape + (a.shape[2],)).transpose(0, 4, 1, 2, 3)
    rp = from_t(rows_t, kv_shape)
    rs = rows_s.reshape((nbs, ss_len) + kv_shape)
    wp = from_t(wrows_t[:, :, sp_len - min(WINDOW, sp_len):], win_shape)
    wsn_t = jnp.concatenate([win_t, wrows_s3.transpose(0, 2, 1)], axis=2)
    wsn = from_t(wsn_t[:, :, wsn_t.shape[2] - min(WINDOW, wsn_t.shape[2]):], win_shape)
    return xp, xs, (rp, rs, wp, wsn)


FFN_TF = 256
MOE_TF = 512
ROW_TILE = 1024


def kernel(x_prompt, x_sample, state_l0_lru_conv, state_l0_lru_h, cache_l1_nsa_kv, cache_l1_nsa_win, page_table,
           state_l2_ssd_conv, state_l2_ssd_ssm, state_l3_lru_conv, state_l3_lru_h, ln_g, ln_b, lru_w_in,
           lru_conv_w, lru_conv_b, lru_w_a, lru_b_a, lru_w_x, lru_b_x, lru_lam, lru_w_out, nsa_w_in, nsa_w_out,
           ssd_w_in, ssd_conv_w, ssd_conv_b, ssd_dt_bias, ssd_a_log, ssd_d, ssd_norm_g, ssd_w_out, ffn_w_in,
           ffn_w_out, moe_router_w, moe_router_b, moe_w_in, moe_w_out):
    nbp, sp_len, d = x_prompt.shape
    nbs, ss_len, _ = x_sample.shape
    xp = x_prompt.reshape(nbp * sp_len, d)
    xs = x_sample.reshape(nbs * ss_len, d)
    ms = xs.shape[0]
    lru_state = {0: (state_l0_lru_conv, state_l0_lru_h), 3: (state_l3_lru_conv, state_l3_lru_h)}
    new = {}
    for i in range(DEPTH):
        kind, j = i % 3, i // 3
        g0, b0 = ln_g[i, 0], ln_b[i, 0]
        if kind == 0:
            conv_s, h_s = lru_state[i]
            xp, xs, new[i] = _lru_layer(xp, xs, nbp, nbs, conv_s, h_s, g0, b0, lru_w_in[j], lru_conv_w[j],
                                        lru_conv_b[j], lru_w_a[j], lru_b_a[j], lru_w_x[j], lru_b_x[j],
                                        lru_lam[j], lru_w_out[j])
        elif kind == 1:
            xp, xs, new[i] = _nsa_layer(xp, xs, nbp, nbs, cache_l1_nsa_kv, cache_l1_nsa_win, page_table, g0, b0,
                                        nsa_w_in[j], nsa_w_out[j])
        else:
            xp, xs, new[i] = _ssd_layer(xp, xs, nbp, nbs, state_l2_ssd_conv, state_l2_ssd_ssm, g0, b0,
                                        ssd_w_in[j], ssd_conv_w[j], ssd_conv_b[j], ssd_dt_bias[j], ssd_a_log[j],
                                        ssd_d[j], ssd_norm_g[j], ssd_w_out[j])
        g1, b1 = ln_g[i, 1], ln_b[i, 1]
        k = i // 2
        if i % 2 == 0:
            xp = _ffn_ln(xp, ffn_w_in[k], ffn_w_out[k], g1, b1, ROW_TILE, FFN_TF, name="ffn")
            xs = _ffn_ln(xs, ffn_w_in[k], ffn_w_out[k], g1, b1, ms, FFN_TF, name="ffn_s")
        else:
            xp = _moe_sparse_ln(xp, moe_router_w[k], moe_router_b[k], moe_w_in[k], moe_w_out[k], g1, b1, MOE_TF)
            cs = _router(xs, moe_router_w[k], moe_router_b[k], tm=ms)
            xs = _moe_ln(xs, cs, moe_w_in[k], moe_w_out[k], g1, b1, ms, MOE_TF)
    out = [xp.reshape(nbp, sp_len, d), xs.reshape(nbs, ss_len, d)]
    for i in range(DEPTH):
        out.extend(new[i])
    return tuple(out)
```

```python
import functools
import math

import jax
import jax.numpy as jnp
from jax import lax
from jax.experimental import pallas as pl
from jax.experimental.pallas import tpu as pltpu

f32 = jnp.float32
bf16 = jnp.bfloat16

D_MODEL = 1024
DEPTH = 4
ALPHA = (2 * DEPTH) ** 0.25
LN_EPS = 1e-5
RMS_EPS = 1e-5
CONV_W = 4
NEG_INF = -1e30

D_RNN = 1344
LRU_BLOCKS = 16
LRU_BS = D_RNN // LRU_BLOCKS
LRU_C = 8.0

NSA_HEADS = 16
NSA_KV_HEADS = 4
NSA_HD = 64
NSA_GROUP = NSA_HEADS // NSA_KV_HEADS
CMP_BLOCK = 32
SEL_BLOCK = 64
SEL_TOPK = 16
WINDOW = 512
FORCE_SCORE = 1e4
ROPE_THETA = 10000.0
NSA_HQ = NSA_HEADS * NSA_HD
NSA_HK = NSA_KV_HEADS * NSA_HD

SSD_INNER = 2 * D_MODEL
SSD_HEAD_DIM = 64
SSD_HEADS = SSD_INNER // SSD_HEAD_DIM
SSD_GROUPS = 4
SSD_STATE = 128
SSD_CHUNK = 128
SSD_CONV_DIM = SSD_INNER + 2 * SSD_GROUPS * SSD_STATE

N_EXPERTS = 8
TOP_K = 2

VMEM_LIMIT_V7X = 56 * 1024 * 1024


def _cparams(sem):
    return pltpu.CompilerParams(dimension_semantics=sem, vmem_limit_bytes=VMEM_LIMIT_V7X)


def _const_spec(shape):
    nd = len(shape)
    return pl.BlockSpec(shape, lambda *_: (0,) * nd, pipeline_mode=pl.Buffered(1))


def _layer_norm(v, g, b):
    mu = jnp.mean(v, -1, keepdims=True)
    d = v - mu
    var = jnp.mean(d * d, -1, keepdims=True)
    return d * lax.rsqrt(var + LN_EPS) * g + b


def _proj_kernel(*refs, epilogues, n_aux):
    n_out = len(epilogues)
    x_ref = refs[0]
    w_refs = refs[1:1 + n_out]
    aux_refs = refs[1 + n_out:1 + n_out + n_aux]
    o_refs = refs[1 + n_out + n_aux:]
    x = x_ref[...].astype(bf16)
    aux = [a[...] for a in aux_refs]
    for w_ref, o_ref, ep in zip(w_refs, o_refs, epilogues):
        y = jnp.dot(x, w_ref[...], preferred_element_type=f32)
        if ep is not None:
            y = ep(y, *aux)
        o_ref[...] = y.astype(o_ref.dtype)


def _proj(x, ws, epilogues, aux=(), aux_period=(), tm=256, name="proj"):
    m, k = x.shape
    assert m % tm == 0
    in_specs = [pl.BlockSpec((tm, k), lambda i: (i, 0))]
    in_specs += [_const_spec(w.shape) for w in ws]
    for a, p in zip(aux, aux_period):
        if p:
            assert p % tm == 0 and a.shape[0] == p
            in_specs.append(pl.BlockSpec((tm, a.shape[1]), functools.partial(lambda i, n: (i % n, 0), n=p // tm)))
        else:
            in_specs.append(_const_spec(a.shape))
    out_shape = [jax.ShapeDtypeStruct((m, w.shape[1]), f32) for w in ws]
    out_specs = [pl.BlockSpec((tm, w.shape[1]), lambda i: (i, 0)) for w in ws]
    return pl.pallas_call(
        functools.partial(_proj_kernel, epilogues=tuple(epilogues), n_aux=len(aux)),
        grid=(m // tm,), in_specs=in_specs, out_specs=out_specs, out_shape=out_shape,
        compiler_params=_cparams(("parallel",)), name=name,
    )(x, *ws, *aux)


def _outproj_ln_kernel(a_ref, w_ref, x_ref, g_ref, b_ref, o_ref):
    y = jnp.dot(a_ref[...].astype(bf16), w_ref[...], preferred_element_type=f32)
    o_ref[...] = _layer_norm(ALPHA * x_ref[...] + y, g_ref[...], b_ref[...])


def _outproj_ln(a, w, x, g, b, tm=256, name="outproj_ln"):
    m, k = a.shape
    d = w.shape[1]
    assert m % tm == 0
    return pl.pallas_call(
        _outproj_ln_kernel, grid=(m // tm,),
        in_specs=[pl.BlockSpec((tm, k), lambda i: (i, 0)), _const_spec(w.shape),
                  pl.BlockSpec((tm, d), lambda i: (i, 0)), _const_spec((1, d)), _const_spec((1, d))],
        out_specs=pl.BlockSpec((tm, d), lambda i: (i, 0)),
        out_shape=jax.ShapeDtypeStruct((m, d), f32),
        compiler_params=_cparams(("parallel",)), name=name,
    )(a, w, x, g.reshape(1, d), b.reshape(1, d))


def _ffn_ln_kernel(x_ref, wg_ref, wu_ref, wo_ref, g_ref, b_ref, o_ref, h_ref):
    j = pl.program_id(1)
    tf = wg_ref.shape[1]
    xb = x_ref[...].astype(bf16)
    hg = jnp.dot(xb, wg_ref[...].astype(bf16), preferred_element_type=f32)
    hu = jnp.dot(xb, wu_ref[...].astype(bf16), preferred_element_type=f32)
    h_ref[:, pl.ds(pl.multiple_of(j * tf, tf), tf)] = (hg * jax.nn.sigmoid(hg) * hu).astype(bf16)

    @pl.when(j == pl.num_programs(1) - 1)
    def _():
        y = jnp.dot(h_ref[...], wo_ref[...].astype(bf16), preferred_element_type=f32)
        o_ref[...] = _layer_norm(ALPHA * x_ref[...] + y, g_ref[...], b_ref[...])


def _ffn_ln(x, w_in, w_out, layer, g, b, tm, tf, name="ffn_ln"):
    m, d = x.shape
    f = w_out.shape[1]
    assert m % tm == 0 and f % tf == 0 and tf % 128 == 0
    nf = f // tf
    return pl.pallas_call(
        _ffn_ln_kernel, grid=(m // tm, nf),
        in_specs=[pl.BlockSpec((tm, d), lambda i, j: (i, 0)),
                  pl.BlockSpec((None, d, tf), lambda i, j: (layer, 0, j)),
                  pl.BlockSpec((None, d, tf), lambda i, j: (layer, 0, j + nf)),
                  pl.BlockSpec((None, f, d), lambda i, j: (layer, 0, 0), pipeline_mode=pl.Buffered(1)),
                  _const_spec((1, d)), _const_spec((1, d))],
        out_specs=pl.BlockSpec((tm, d), lambda i, j: (i, 0)),
        out_shape=jax.ShapeDtypeStruct((m, d), f32),
        scratch_shapes=[pltpu.VMEM((tm, f), bf16)],
        compiler_params=_cparams(("parallel", "arbitrary")), name=name,
    )(x, w_in, w_in, w_out, g.reshape(1, d), b.reshape(1, d))


def _router_kernel(x_ref, w_ref, b_ref, c_ref):
    logits = jnp.dot(x_ref[...], w_ref[...], precision=lax.Precision.HIGHEST,
                     preferred_element_type=f32) + b_ref[...]
    e_idx, i1, i2, w1, w2 = _top2(logits)
    c_ref[...] = jnp.where(e_idx == i1, w1, 0.0) + jnp.where(e_idx == i2, w2, 0.0)


def _router(x, w, b, tm=512):
    m, d = x.shape
    assert m % tm == 0
    return pl.pallas_call(
        _router_kernel, grid=(m // tm,),
        in_specs=[pl.BlockSpec((tm, d), lambda i: (i, 0)), _const_spec(w.shape),
                  _const_spec((1, N_EXPERTS))],
        out_specs=pl.BlockSpec((tm, N_EXPERTS), lambda i: (i, 0)),
        out_shape=jax.ShapeDtypeStruct((m, N_EXPERTS), f32),
        compiler_params=_cparams(("parallel",)), name="router",
    )(x, w, b.reshape(1, N_EXPERTS))


def _moe_ln_kernel(x_ref, c_ref, wg_ref, wu_ref, wo_ref, g_ref, b_ref, o_ref, acc_ref):
    e = pl.program_id(1)
    j = pl.program_id(2)
    x = x_ref[...]
    xb = x.astype(bf16)
    c = c_ref[...]
    ce = jnp.sum(jnp.where(lax.broadcasted_iota(jnp.int32, c.shape, 1) == e, c, 0.0), -1, keepdims=True)
    hg = jnp.dot(xb, wg_ref[...].astype(bf16), preferred_element_type=f32)
    hu = jnp.dot(xb, wu_ref[...].astype(bf16), preferred_element_type=f32)
    act = (hg * jax.nn.sigmoid(hg) * hu).astype(bf16)
    part = ce * jnp.dot(act, wo_ref[...].astype(bf16), preferred_element_type=f32)
    first = jnp.logical_and(e == 0, j == 0)

    @pl.when(first)
    def _():
        acc_ref[...] = part

    @pl.when(jnp.logical_not(first))
    def _():
        acc_ref[...] += part

    @pl.when(jnp.logical_and(e == pl.num_programs(1) - 1, j == pl.num_programs(2) - 1))
    def _():
        o_ref[...] = _layer_norm(ALPHA * x + acc_ref[...], g_ref[...], b_ref[...])


def _moe_ln(x, comb, w_in, w_out, layer, g, b, tm, tf):
    m, d = x.shape
    _, ne, f, _ = w_out.shape
    assert m % tm == 0 and f % tf == 0
    nf = f // tf
    return pl.pallas_call(
        _moe_ln_kernel, grid=(m // tm, ne, nf),
        in_specs=[pl.BlockSpec((tm, d), lambda i, e, j: (i, 0)),
                  pl.BlockSpec((tm, ne), lambda i, e, j: (i, 0)),
                  pl.BlockSpec((None, None, d, tf), lambda i, e, j: (layer, e, 0, j)),
                  pl.BlockSpec((None, None, d, tf), lambda i, e, j: (layer, e, 0, j + nf)),
                  pl.BlockSpec((None, None, tf, d), lambda i, e, j: (layer, e, j, 0)),
                  _const_spec((1, d)), _const_spec((1, d))],
        out_specs=pl.BlockSpec((tm, d), lambda i, e, j: (i, 0)),
        out_shape=jax.ShapeDtypeStruct((m, d), f32),
        scratch_shapes=[pltpu.VMEM((tm, d), f32)],
        compiler_params=_cparams(("parallel", "arbitrary", "arbitrary")), name="moe_ln",
    )(x, comb, w_in, w_in, w_out, g.reshape(1, d), b.reshape(1, d))


MOE_TM = 1024
MOE_ROUTE_TM = 512
MOE_DISPATCH_TM = 512
MOE_COMBINE_TM = 256
ROUTE_COLS = 8


def _top2(logits):
    e_idx = lax.broadcasted_iota(jnp.int32, logits.shape, 1)
    v1 = jnp.max(logits, -1, keepdims=True)
    i1 = jnp.min(jnp.where(logits == v1, e_idx, N_EXPERTS), -1, keepdims=True)
    rest = jnp.where(e_idx == i1, -jnp.inf, logits)
    v2 = jnp.max(rest, -1, keepdims=True)
    i2 = jnp.min(jnp.where(rest == v2, e_idx, N_EXPERTS), -1, keepdims=True)
    e2 = jnp.exp(v2 - v1)
    den = 1.0 + e2
    return e_idx, i1, i2, 1.0 / den, e2 / den


def _route_kernel(x_ref, w_ref, b_ref, r_ref, cnt_ref, carry_ref):
    i = pl.program_id(0)
    tm = x_ref.shape[0]

    @pl.when(i == 0)
    def _():
        carry_ref[...] = jnp.zeros_like(carry_ref)

    logits = jnp.dot(x_ref[...], w_ref[...], precision=lax.Precision.HIGHEST,
                     preferred_element_type=f32) + b_ref[...]
    e_idx, i1, i2, w1, w2 = _top2(logits)
    hit1 = e_idx == i1
    hit2 = e_idx == i2
    a = jnp.where(hit1, 1.0, 0.0) + jnp.where(hit2, 1.0, 0.0)
    row = lax.broadcasted_iota(jnp.int32, (tm, tm), 0)
    col = lax.broadcasted_iota(jnp.int32, (tm, tm), 1)
    before = jnp.where(col < row, 1.0, 0.0).astype(bf16)
    c = carry_ref[...] + jnp.dot(before, a.astype(bf16), preferred_element_type=f32)
    pos1 = jnp.sum(jnp.where(hit1, c, 0.0), -1, keepdims=True)
    pos2 = jnp.sum(jnp.where(hit2, c, 0.0), -1, keepdims=True)
    carry_ref[...] += jnp.sum(a, 0, keepdims=True)
    cols = (i1.astype(f32), i2.astype(f32), pos1, pos2, w1, w2)
    rec = jnp.zeros((tm, ROUTE_COLS), f32)
    for n, v in enumerate(cols):
        rec = jnp.where(e_idx == n, v, rec)
    r_ref[...] = rec

    @pl.when(i == pl.num_programs(0) - 1)
    def _():
        cnt_ref[...] = carry_ref[...]


def _route(x, w, b):
    m, d = x.shape
    tm = MOE_ROUTE_TM
    assert m % tm == 0 and ROUTE_COLS == N_EXPERTS
    return pl.pallas_call(
        _route_kernel, grid=(m // tm,),
        in_specs=[pl.BlockSpec((tm, d), lambda i: (i, 0)), _const_spec(w.shape), _const_spec((1, N_EXPERTS))],
        out_specs=[pl.BlockSpec((tm, ROUTE_COLS), lambda i: (i, 0)), pl.BlockSpec((1, N_EXPERTS), lambda i: (0, 0))],
        out_shape=[jax.ShapeDtypeStruct((m, ROUTE_COLS), f32), jax.ShapeDtypeStruct((1, N_EXPERTS), f32)],
        scratch_shapes=[pltpu.VMEM((1, N_EXPERTS), f32)],
        compiler_params=_cparams(("arbitrary",)), name="moe_route",
    )(x, w, b.reshape(1, N_EXPERTS))


def _row_copy(src, src_row, dst, dst_row, sem):
    return pltpu.make_async_copy(src.at[pl.ds(src_row, 1)], dst.at[pl.ds(dst_row, 1)], sem)


def _dispatch_kernel(d1_ref, d2_ref, x_ref, init_hbm, o_hbm, sem):
    del init_hbm
    base = pl.program_id(0) * MOE_DISPATCH_TM

    def copies(t):
        tok = base + t
        return (_row_copy(x_ref, t, o_hbm, d1_ref[tok], sem), _row_copy(x_ref, t, o_hbm, d2_ref[tok], sem))

    def issue(t, carry):
        for cp in copies(t):
            cp.start()
        return carry

    def drain(t, carry):
        for cp in copies(t):
            cp.wait()
        return carry

    lax.fori_loop(0, MOE_DISPATCH_TM, issue, 0)
    lax.fori_loop(0, MOE_DISPATCH_TM, drain, 0)


def _dispatch(x, d1, d2, n_rows):
    m, d = x.shape
    assert m % MOE_DISPATCH_TM == 0
    any_spec = pl.BlockSpec(memory_space=pl.ANY)
    grid_spec = pltpu.PrefetchScalarGridSpec(
        num_scalar_prefetch=2, grid=(m // MOE_DISPATCH_TM,),
        in_specs=[pl.BlockSpec((MOE_DISPATCH_TM, d), lambda i, *_: (i, 0)), any_spec], out_specs=any_spec,
        scratch_shapes=[pltpu.SemaphoreType.DMA(())])
    return pl.pallas_call(
        _dispatch_kernel, grid_spec=grid_spec, out_shape=jax.ShapeDtypeStruct((n_rows, d), x.dtype),
        input_output_aliases={3: 0},
        compiler_params=_cparams(("arbitrary",)), name="moe_dispatch",
    )(d1, d2, x, jnp.zeros((n_rows, d), x.dtype))


def _moe_group_kernel(te_ref, nu_ref, x_ref, wg_ref, wu_ref, wo_ref, o_ref, acc_ref):
    del te_ref
    t = pl.program_id(0)
    j = pl.program_id(1)
    used = t < nu_ref[0]
    last = j == pl.num_programs(1) - 1

    @pl.when(used)
    def _():
        xb = x_ref[...].astype(bf16)
        hg = jnp.dot(xb, wg_ref[...].astype(bf16), preferred_element_type=f32)
        hu = jnp.dot(xb, wu_ref[...].astype(bf16), preferred_element_type=f32)
        act = (hg * jax.nn.sigmoid(hg) * hu).astype(bf16)
        part = jnp.dot(act, wo_ref[...].astype(bf16), preferred_element_type=f32)

        @pl.when(j == 0)
        def _():
            acc_ref[...] = part

        @pl.when(j > 0)
        def _():
            acc_ref[...] += part

        @pl.when(last)
        def _():
            o_ref[...] = acc_ref[...]

    @pl.when(jnp.logical_and(jnp.logical_not(used), last))
    def _():
        o_ref[...] = jnp.zeros_like(o_ref)


def _moe_group(xs, tile_expert, n_used, w_in, w_out, layer, tf):
    r, d = xs.shape
    _, ne, f, _ = w_out.shape
    assert r % MOE_TM == 0 and f % tf == 0
    nf = f // tf
    n_tiles = r // MOE_TM

    def jj(t, j, nu):
        return jnp.where(t < nu[0], j, nf - 1)

    grid_spec = pltpu.PrefetchScalarGridSpec(
        num_scalar_prefetch=2, grid=(n_tiles, nf),
        in_specs=[pl.BlockSpec((MOE_TM, d), lambda t, j, te, nu: (jnp.maximum(jnp.minimum(t, nu[0] - 1), 0), 0)),
                  pl.BlockSpec((None, None, d, tf), lambda t, j, te, nu: (layer, te[t], 0, jj(t, j, nu))),
                  pl.BlockSpec((None, None, d, tf), lambda t, j, te, nu: (layer, te[t], 0, jj(t, j, nu) + nf)),
                  pl.BlockSpec((None, None, tf, d), lambda t, j, te, nu: (layer, te[t], jj(t, j, nu), 0))],
        out_specs=pl.BlockSpec((MOE_TM, d), lambda t, j, te, nu: (t, 0)),
        scratch_shapes=[pltpu.VMEM((MOE_TM, d), f32)])
    return pl.pallas_call(
        _moe_group_kernel, grid_spec=grid_spec, out_shape=jax.ShapeDtypeStruct((r, d), f32),
        compiler_params=_cparams(("arbitrary", "arbitrary")), name="moe_group",
    )(tile_expert, n_used, xs, w_in, w_in, w_out)


def _moe_combine_kernel(d1_ref, d2_ref, x_ref, r_ref, y_hbm, g_ref, b_ref, o_ref, ya_ref, yb_ref, sem):
    tm = x_ref.shape[0]
    base = pl.program_id(0) * tm

    def copies(t):
        tok = base + t
        return (_row_copy(y_hbm, d1_ref[tok], ya_ref, t, sem), _row_copy(y_hbm, d2_ref[tok], yb_ref, t, sem))

    def issue(t, carry):
        for cp in copies(t):
            cp.start()
        return carry

    def drain(t, carry):
        for cp in copies(t):
            cp.wait()
        return carry

    lax.fori_loop(0, tm, issue, 0)
    lax.fori_loop(0, tm, drain, 0)
    r = r_ref[...]
    mix = r[:, 4:5] * ya_ref[...] + r[:, 5:6] * yb_ref[...]
    o_ref[...] = _layer_norm(ALPHA * x_ref[...] + mix, g_ref[...], b_ref[...])


def _moe_combine(x, route, y, d1, d2, g, b):
    m, d = x.shape
    tm = MOE_COMBINE_TM
    assert m % tm == 0
    grid_spec = pltpu.PrefetchScalarGridSpec(
        num_scalar_prefetch=2, grid=(m // tm,),
        in_specs=[pl.BlockSpec((tm, d), lambda i, *_: (i, 0)),
                  pl.BlockSpec((tm, ROUTE_COLS), lambda i, *_: (i, 0)),
                  pl.BlockSpec(memory_space=pl.ANY),
                  pl.BlockSpec((1, d), lambda i, *_: (0, 0)), pl.BlockSpec((1, d), lambda i, *_: (0, 0))],
        out_specs=pl.BlockSpec((tm, d), lambda i, *_: (i, 0)),
        scratch_shapes=[pltpu.VMEM((tm, d), f32), pltpu.VMEM((tm, d), f32), pltpu.SemaphoreType.DMA(())])
    return pl.pallas_call(
        _moe_combine_kernel, grid_spec=grid_spec, out_shape=jax.ShapeDtypeStruct((m, d), f32),
        compiler_params=_cparams(("arbitrary",)), name="moe_combine",
    )(d1, d2, x, route, y, g.reshape(1, d), b.reshape(1, d))


def _moe_sparse_ln(x, router_w, router_b, w_in, w_out, layer, g, b, tf):
    m, d = x.shape
    route, counts = _route(x, router_w, router_b)
    counts = counts[0].astype(jnp.int32)
    padded = (counts + MOE_TM - 1) // MOE_TM * MOE_TM
    ends = jnp.cumsum(padded)
    offs = ends - padded
    n_tiles = -(-TOP_K * m // MOE_TM) + N_EXPERTS
    n_used = (ends[-1] // MOE_TM).reshape(1)
    tile_start = jnp.minimum(jnp.arange(n_tiles, dtype=jnp.int32), n_used[0] - 1) * MOE_TM
    tile_expert = jnp.minimum(jnp.sum((tile_start[:, None] >= ends[None, :]).astype(jnp.int32), axis=1), N_EXPERTS - 1)
    e1 = route[:, 0].astype(jnp.int32)
    e2 = route[:, 1].astype(jnp.int32)
    d1 = offs[e1] + route[:, 2].astype(jnp.int32)
    d2 = offs[e2] + route[:, 3].astype(jnp.int32)
    xs = _dispatch(x, d1, d2, n_tiles * MOE_TM)
    y = _moe_group(xs, tile_expert, n_used, w_in, w_out, layer, tf)
    return _moe_combine(x, route, y, d1, d2, g, b)


def _softplus(x):
    return jnp.maximum(x, 0.0) + jnp.log(1.0 + jnp.exp(-jnp.abs(x)))


def _lru_gates(xc, wa, ba, wx, bx, sp):
    xcb = xc.astype(bf16)
    r = jax.nn.sigmoid(jnp.dot(xcb, wa, preferred_element_type=f32) + ba)
    gi = jax.nn.sigmoid(jnp.dot(xcb, wx, preferred_element_type=f32) + bx)
    log_a = sp * r
    th = jnp.tanh(log_a)
    one_minus_a2 = -2.0 * th / (1.0 - th)
    return jnp.exp(log_a), jnp.sqrt(one_minus_a2) * gi * xc


def _lru_core_kernel(xb_ref, g_ref, cw_ref, cb_ref, wa_ref, ba_ref, wx_ref, bx_ref, lam_ref,
                     y_ref, hlast_ref, ext_ref, a_ref, b_ref, h_ref, *, ts, nb):
    i = pl.program_id(0)
    hist = 8

    @pl.when(i == 0)
    def _():
        ext_ref[:, 0:hist, :] = jnp.zeros((nb, hist, ext_ref.shape[2]), f32)
        h_ref[...] = jnp.zeros_like(h_ref)

    ext_ref[:, hist:hist + ts, :] = xb_ref[...]
    sp = -LRU_C * _softplus(-lam_ref[...])
    cw = cw_ref[...]
    for b in range(nb):
        xc = cb_ref[...]
        for k in range(CONV_W):
            off = hist - (CONV_W - 1) + k
            xc = xc + cw[k:k + 1, :] * ext_ref[b, off:off + ts, :]
        a, bt = _lru_gates(xc, wa_ref[...], ba_ref[...], wx_ref[...], bx_ref[...], sp)
        a_ref[b] = a
        b_ref[b] = bt
    ext_ref[:, 0:hist, :] = ext_ref[:, ts:ts + hist, :]

    def step(t, h):
        h = a_ref[:, t, :] * h + b_ref[:, t, :]
        b_ref[:, t, :] = h
        return h

    h = lax.fori_loop(0, ts, step, h_ref[...], unroll=8)
    h_ref[...] = h
    y_ref[...] = b_ref[...] * g_ref[...]

    @pl.when(i == pl.num_programs(0) - 1)
    def _():
        hlast_ref[...] = h


def _lru_core(xb, g, cw, cb, wa, ba, wx, bx, lam, ts=64):
    nb, s, c = xb.shape
    assert s % ts == 0 and ts % 8 == 0
    blk = pl.BlockSpec((nb, ts, c), lambda i: (0, i, 0))
    row = _const_spec((1, c))
    return pl.pallas_call(
        functools.partial(_lru_core_kernel, ts=ts, nb=nb), grid=(s // ts,),
        in_specs=[blk, blk, _const_spec((CONV_W, c)), row, _const_spec((c, c)), row,
                  _const_spec((c, c)), row, row],
        out_specs=[blk, pl.BlockSpec((nb, c), lambda i: (0, 0))],
        out_shape=[jax.ShapeDtypeStruct((nb, s, c), f32), jax.ShapeDtypeStruct((nb, c), f32)],
        scratch_shapes=[pltpu.VMEM((nb, ts + 8, c), f32), pltpu.VMEM((nb, ts, c), f32),
                        pltpu.VMEM((nb, ts, c), f32), pltpu.VMEM((nb, c), f32)],
        compiler_params=_cparams(("arbitrary",)), name="lru_core",
    )(xb, g, cw, cb.reshape(1, c), wa, ba.reshape(1, c), wx, bx.reshape(1, c), lam.reshape(1, c))


def _lru_short_kernel(xb_ref, g_ref, cs_ref, h0_ref, cw_ref, cb_ref, wa_ref, ba_ref, wx_ref, bx_ref,
                      lam_ref, y_ref, hlast_ref, *, s):
    rows = [cs_ref[k] for k in range(CONV_W - 1)] + [xb_ref[t] for t in range(s)]
    sp = -LRU_C * _softplus(-lam_ref[...])
    cw = cw_ref[...]
    h = h0_ref[...]
    for t in range(s):
        xc = cb_ref[...]
        for k in range(CONV_W):
            xc = xc + cw[k:k + 1, :] * rows[t + k]
        a, bt = _lru_gates(xc, wa_ref[...], ba_ref[...], wx_ref[...], bx_ref[...], sp)
        h = a * h + bt
        y_ref[t] = h * g_ref[t]
    hlast_ref[...] = h


def _lru_short(xb, g, cs, h0, cw, cb, wa, ba, wx, bx, lam):
    s, nb, c = xb.shape
    return pl.pallas_call(
        functools.partial(_lru_short_kernel, s=s),
        out_shape=[jax.ShapeDtypeStruct((s, nb, c), f32), jax.ShapeDtypeStruct((nb, c), f32)],
        compiler_params=pltpu.CompilerParams(vmem_limit_bytes=VMEM_LIMIT_V7X), name="lru_short",
    )(xb, g, cs, h0, cw, cb.reshape(1, c), wa, ba.reshape(1, c), wx, bx.reshape(1, c), lam.reshape(1, c))


def _block_diag(w):
    n, k, _ = w.shape
    eye = jnp.eye(n, dtype=w.dtype)
    return (eye[:, None, :, None] * w[:, :, None, :]).reshape(n * k, n * k)


def _gelu_ep(y):
    return jax.nn.gelu(y)


def _lru_layer(xp, xs, nbp, nbs, conv_s, h_s, ln_g, ln_b, w_in, cw, cb, w_a, b_a, w_x, b_x, lam, w_out):
    c = D_RNN
    w_gate = w_in[:, :c].astype(bf16)
    w_xb = w_in[:, c:].astype(bf16)
    wa = _block_diag(w_a).astype(bf16)
    wx = _block_diag(w_x).astype(bf16)
    wo = w_out.astype(bf16)
    sp_len = xp.shape[0] // nbp
    ss_len = xs.shape[0] // nbs

    gp, xbp = _proj(xp, [w_gate, w_xb], [_gelu_ep, None], tm=256, name="lru_in")
    gs, xbs = _proj(xs, [w_gate, w_xb], [_gelu_ep, None], tm=xs.shape[0], name="lru_in_s")
    xbp3 = xbp.reshape(nbp, sp_len, c)
    yp, hp = _lru_core(xbp3, gp.reshape(nbp, sp_len, c), cw, cb, wa, b_a, wx, b_x, lam)
    xbs3 = xbs.reshape(nbs, ss_len, c)
    ys_t, hs = _lru_short(xbs3.transpose(1, 0, 2), gs.reshape(nbs, ss_len, c).transpose(1, 0, 2),
                          conv_s.transpose(1, 0, 2), h_s, cw, cb, wa, b_a, wx, b_x, lam)
    ys = ys_t.transpose(1, 0, 2).reshape(nbs * ss_len, c)
    xp = _outproj_ln(yp.reshape(nbp * sp_len, c), wo, xp, ln_g, ln_b, name="lru_out")
    xs = _outproj_ln(ys, wo, xs, ln_g, ln_b, tm=xs.shape[0], name="lru_out_s")
    conv_p = xbp3[:, sp_len - (CONV_W - 1):]
    conv_sn = jnp.concatenate([conv_s, xbs3], axis=1)[:, ss_len:]
    return xp, xs, (conv_p, conv_sn, hp, hs)


SSD_GN = SSD_GROUPS * SSD_STATE
SSD_HPG = SSD_HEADS // SSD_GROUPS
SSD_DT_PAD = 128


def _ssd_core_kernel(*refs, q, valid, zero_init):
    if zero_init:
        (xbc_ref, dt_ref, zs_ref, cw_ref, cb_ref, alog_ref, dsk_ref, ng_ref,
         y_ref, sfin_ref, ext_ref, st_ref, yacc_ref) = refs
    else:
        (xbc_ref, dt_ref, zs_ref, hist_ref, s0_ref, cw_ref, cb_ref, alog_ref, dsk_ref, ng_ref,
         y_ref, sfin_ref, ext_ref, st_ref, yacc_ref) = refs
    ci = pl.program_id(1)
    hist = 8

    @pl.when(ci == 0)
    def _():
        if zero_init:
            ext_ref[0:hist, :] = jnp.zeros((hist, ext_ref.shape[1]), f32)
            st_ref[...] = jnp.zeros_like(st_ref)
        else:
            ext_ref[0:hist, :] = hist_ref[...]
            st_ref[...] = s0_ref[...]

    ext_ref[hist:hist + q, :] = xbc_ref[...]
    cw = cw_ref[...]
    xc = cb_ref[...]
    for k in range(CONV_W):
        off = hist - (CONV_W - 1) + k
        xc = xc + cw[k:k + 1, :] * ext_ref[off:off + q, :]
    xc = xc * jax.nn.sigmoid(xc)
    if q >= hist:
        ext_ref[0:hist, :] = ext_ref[q:q + hist, :]

    row = lax.broadcasted_iota(jnp.int32, (q, q), 0)
    col = lax.broadcasted_iota(jnp.int32, (q, q), 1)
    causal = col <= row
    dt = dt_ref[...]
    if valid < q:
        dt = jnp.where(lax.broadcasted_iota(jnp.int32, dt.shape, 0) < valid, dt, 0.0)
    a_neg = -jnp.exp(alog_ref[...])
    tril = jnp.where(causal, 1.0, 0.0)
    acum = jnp.dot(tril, dt * a_neg, precision=lax.Precision.HIGHEST, preferred_element_type=f32)
    acum_t = acum.T
    last = acum[q - 1:q, :]
    dec_out = jnp.exp(last - acum)
    dec_chunk_t = jnp.exp(acum_t[:, q - 1:q])
    e_acum = jnp.exp(acum)

    for g in range(SSD_GROUPS):
        bm = xc[:, SSD_INNER + g * SSD_STATE:SSD_INNER + (g + 1) * SSD_STATE].astype(bf16)
        cm = xc[:, SSD_INNER + SSD_GN + g * SSD_STATE:SSD_INNER + SSD_GN + (g + 1) * SSD_STATE].astype(bf16)
        cb = lax.dot_general(cm, bm, (((1,), (1,)), ((), ())), preferred_element_type=f32)
        for hh in range(SSD_HPG):
            h = g * SSD_HPG + hh
            xh = xc[:, h * SSD_HEAD_DIM:(h + 1) * SSD_HEAD_DIM]
            xdt = xh * dt[:, h:h + 1]
            seg = acum[:, h:h + 1] - acum_t[h:h + 1, :]
            decay = jnp.exp(jnp.where(causal, seg, NEG_INF))
            y_diag = jnp.dot((cb * decay).astype(bf16), xdt.astype(bf16), preferred_element_type=f32)
            s_prev = st_ref[h]
            y_off = lax.dot_general(cm, s_prev.astype(bf16), (((1,), (1,)), ((), ())),
                                    preferred_element_type=f32) * e_acum[:, h:h + 1]
            s_new = lax.dot_general((xdt * dec_out[:, h:h + 1]).astype(bf16), bm,
                                    (((0,), (0,)), ((), ())), preferred_element_type=f32)
            st_ref[h] = dec_chunk_t[h:h + 1, :] * s_prev + s_new
            yacc_ref[:, h * SSD_HEAD_DIM:(h + 1) * SSD_HEAD_DIM] = y_diag + y_off

    y = yacc_ref[...] + dsk_ref[...] * xc[:, :SSD_INNER]
    yg = y * zs_ref[...]
    gw = SSD_INNER // SSD_GROUPS
    for g in range(SSD_GROUPS):
        v = yg[:, g * gw:(g + 1) * gw]
        v = v * lax.rsqrt(jnp.mean(v * v, -1, keepdims=True) + RMS_EPS)
        y_ref[:, g * gw:(g + 1) * gw] = v * ng_ref[:, g * gw:(g + 1) * gw]

    @pl.when(ci == pl.num_programs(1) - 1)
    def _():
        sfin_ref[...] = st_ref[...]


def _ssd_core(xbc, dt, zs, hist, s0, cw, cb, a_log, d_skip, norm_g, q, valid):
    nb, s, cd = xbc.shape
    assert s % q == 0
    zero_init = s0 is None
    blk = lambda w: pl.BlockSpec((None, q, w), lambda b, c: (b, c, 0))
    st_spec = pl.BlockSpec((None, SSD_HEADS, SSD_HEAD_DIM, SSD_STATE), lambda b, c: (b, 0, 0, 0))
    in_specs = [blk(cd), blk(SSD_DT_PAD), blk(SSD_INNER)]
    args = [xbc, dt, zs]
    if not zero_init:
        in_specs += [pl.BlockSpec((None, 8, cd), lambda b, c: (b, 0, 0)), st_spec]
        args += [hist, s0]
    in_specs += [_const_spec((CONV_W, cd)), _const_spec((1, cd)), _const_spec((1, SSD_DT_PAD)),
                 _const_spec((1, SSD_INNER)), _const_spec((1, SSD_INNER))]
    pad_h = SSD_DT_PAD - SSD_HEADS
    args += [cw, cb.reshape(1, cd), jnp.pad(a_log, (0, pad_h)).reshape(1, SSD_DT_PAD),
             jnp.repeat(d_skip, SSD_HEAD_DIM).reshape(1, SSD_INNER), norm_g.reshape(1, SSD_INNER)]
    return pl.pallas_call(
        functools.partial(_ssd_core_kernel, q=q, valid=valid, zero_init=zero_init),
        grid=(nb, s // q), in_specs=in_specs,
        out_specs=[blk(SSD_INNER), st_spec],
        out_shape=[jax.ShapeDtypeStruct((nb, s, SSD_INNER), f32),
                   jax.ShapeDtypeStruct((nb, SSD_HEADS, SSD_HEAD_DIM, SSD_STATE), f32)],
        scratch_shapes=[pltpu.VMEM((q + 8, cd), f32), pltpu.VMEM((SSD_HEADS, SSD_HEAD_DIM, SSD_STATE), f32),
                        pltpu.VMEM((q, SSD_INNER), f32)],
        compiler_params=_cparams(("parallel", "arbitrary")), name="ssd_core",
    )(*args)


def _silu_ep(y, *_):
    return y * jax.nn.sigmoid(y)


def _dt_ep(y, bias):
    return _softplus(y + bias)


def _pass_ep(y, *_):
    return y


def _ssd_layer(xp, xs, nbp, nbs, conv_s, st_s, ln_g, ln_b, w_in, cw, cb, dt_bias, a_log, d_skip, norm_g, w_out):
    cd = SSD_CONV_DIM
    pad_h = SSD_DT_PAD - SSD_HEADS
    w_z = w_in[:, :SSD_INNER].astype(bf16)
    w_xbc = w_in[:, SSD_INNER:SSD_INNER + cd].astype(bf16)
    w_dt = jnp.pad(w_in[:, SSD_INNER + cd:], ((0, 0), (0, pad_h))).astype(bf16)
    bias = jnp.pad(dt_bias, (0, pad_h)).reshape(1, SSD_DT_PAD)
    wo = w_out.astype(bf16)
    sp_len = xp.shape[0] // nbp
    ss_len = xs.shape[0] // nbs
    eps = [_silu_ep, _pass_ep, _dt_ep]

    zsp, xbcp, dtp = _proj(xp, [w_z, w_xbc, w_dt], eps, aux=(bias,), aux_period=(0,), tm=256, name="ssd_in")
    zss, xbcs, dts = _proj(xs, [w_z, w_xbc, w_dt], eps, aux=(bias,), aux_period=(0,), tm=xs.shape[0], name="ssd_in_s")
    xbcp3 = xbcp.reshape(nbp, sp_len, cd)
    yp, sp_fin = _ssd_core(xbcp3, dtp.reshape(nbp, sp_len, SSD_DT_PAD), zsp.reshape(nbp, sp_len, SSD_INNER),
                           None, None, cw, cb, a_log, d_skip, norm_g, q=math.gcd(sp_len, SSD_CHUNK), valid=SSD_CHUNK)
    qs = 8
    assert ss_len <= qs
    tpad = lambda a: jnp.pad(a, ((0, 0), (0, qs - ss_len), (0, 0)))
    xbcs3 = xbcs.reshape(nbs, ss_len, cd)
    hist = jnp.pad(conv_s, ((0, 0), (8 - (CONV_W - 1), 0), (0, 0)))
    ys, ss_fin = _ssd_core(tpad(xbcs3), tpad(dts.reshape(nbs, ss_len, SSD_DT_PAD)),
                           tpad(zss.reshape(nbs, ss_len, SSD_INNER)), hist, st_s,
                           cw, cb, a_log, d_skip, norm_g, q=qs, valid=ss_len)
    ys = ys[:, :ss_len].reshape(nbs * ss_len, SSD_INNER)
    xp = _outproj_ln(yp.reshape(nbp * sp_len, SSD_INNER), wo, xp, ln_g, ln_b, name="ssd_out")
    xs = _outproj_ln(ys, wo, xs, ln_g, ln_b, tm=xs.shape[0], name="ssd_out_s")
    conv_p = xbcp3[:, sp_len - (CONV_W - 1):]
    conv_sn = jnp.concatenate([conv_s, xbcs3], axis=1)[:, ss_len:]
    return xp, xs, (conv_p, conv_sn, sp_fin, ss_fin)


NSA_TILE = 128
NSA_SCALE = NSA_HD ** -0.5


def _rope_tables(pos):
    inv = ROPE_THETA ** (-jnp.arange(0, NSA_HD, 2, dtype=f32) / NSA_HD)
    ang = pos[:, None] * inv[None, :]
    cos, sin = jnp.cos(ang), jnp.sin(ang)
    return jnp.tile(cos, (1, 4)), jnp.tile(jnp.concatenate([-sin, sin], axis=1), (1, 2))


def _rope_cols(y, cos, sgn_sin, blocks):
    lane = lax.broadcasted_iota(jnp.int32, (y.shape[0], 128), 1)
    first_half = (lane % NSA_HD) < NSA_HD // 2
    out = []
    for c in range(y.shape[1] // 128):
        blk = y[:, 128 * c:128 * (c + 1)]
        if c in blocks:
            partner = jnp.where(first_half, pltpu.roll(blk, 128 - NSA_HD // 2, 1), pltpu.roll(blk, NSA_HD // 2, 1))
            blk = blk * cos + partner * sgn_sin
        out.append(blk)
    return jnp.concatenate(out, axis=1)


def _q_ep(y, cos, sin):
    return _rope_cols(y, cos, sin, range(NSA_HQ // 128)) * NSA_SCALE


def _rows_ep(y, cos, sin):
    return _rope_cols(y, cos, sin, (4, 5))


def _wrows_ep(y, cos, sin):
    return _rope_cols(y, cos, sin, (0, 1))


def _sigmoid_ep(y, *_):
    return jax.nn.sigmoid(y)


def _nsa_cmp_kernel(pt_ref, pa_ref, pb_ref, cos_ref, sin_ref, o_ref):
    del pt_ref
    per_page = NSA_TILE // CMP_BLOCK
    for n, ref in enumerate((pa_ref, pb_ref)):
        m = ref[...].reshape(per_page, CMP_BLOCK, ref.shape[1]).sum(1) * (1.0 / CMP_BLOCK)
        lo, hi = n * per_page, (n + 1) * per_page
        o_ref[lo:hi, :] = _rope_cols(m, cos_ref[lo:hi, :], sin_ref[lo:hi, :], (0, 1))


def _nsa_cmp(pool, page_table, cos, sin):
    nb, n_pages = page_table.shape
    assert n_pages % 2 == 0
    w = 2 * NSA_HK
    per_step = 2 * NSA_TILE // CMP_BLOCK
    grid_spec = pltpu.PrefetchScalarGridSpec(
        num_scalar_prefetch=1, grid=(nb, n_pages // 2),
        in_specs=[pl.BlockSpec((None, NSA_TILE, w), lambda b, p, pt: (pt[b, 2 * p], 0, 0)),
                  pl.BlockSpec((None, NSA_TILE, w), lambda b, p, pt: (pt[b, 2 * p + 1], 0, 0)),
                  pl.BlockSpec((per_step, 128), lambda b, p, pt: (p, 0)),
                  pl.BlockSpec((per_step, 128), lambda b, p, pt: (p, 0))],
        out_specs=pl.BlockSpec((None, per_step, w), lambda b, p, pt: (b, p, 0)))
    return pl.pallas_call(
        _nsa_cmp_kernel, grid_spec=grid_spec,
        out_shape=jax.ShapeDtypeStruct((nb, n_pages * NSA_TILE // CMP_BLOCK, w), f32),
        compiler_params=_cparams(("parallel", "arbitrary")), name="nsa_cmp",
    )(page_table, pool, pool, cos, sin)


def _even_odd(cmp):
    nb, nc, w = cmp.shape
    return cmp.reshape(nb, nc // 2, 2, w).transpose(0, 2, 1, 3).reshape(nb, nc, w)


def _topk_mask_rows(score_t, k):
    n = score_t.shape[0]
    assert n % 8 == 0
    pieces = [score_t[8 * v:8 * v + 8] for v in range(n // 8)]
    ridx = lax.broadcasted_iota(jnp.int32, pieces[0].shape, 0)
    cnts = [jnp.zeros(pieces[0].shape, f32) for _ in pieces]
    for i in range(n):
        si = score_t[i:i + 1, :]
        for v, pc in enumerate(pieces):
            if 8 * v > i:
                cnts[v] = cnts[v] + jnp.where(si >= pc, 1.0, 0.0)
            elif 8 * v + 7 < i:
                cnts[v] = cnts[v] + jnp.where(si > pc, 1.0, 0.0)
            else:
                tie_wins = jnp.where(ridx + 8 * v > i, 1.0, 0.0)
                cnts[v] = cnts[v] + jnp.where(si > pc, 1.0, 0.0) + jnp.where(si == pc, tie_wins, 0.0)
    return jnp.concatenate([jnp.where(c < k, 1.0, 0.0) for c in cnts], axis=0)


def _topk_mask_lanes(score, k, n_real):
    lane = lax.broadcasted_iota(jnp.int32, score.shape, 1)
    cnt = jnp.zeros(score.shape, f32)
    for i in range(n_real):
        si = score[:, i:i + 1]
        cnt = cnt + jnp.where(si > score, 1.0, 0.0) + jnp.where(si == score, jnp.where(lane > i, 1.0, 0.0), 0.0)
    return jnp.where(cnt < k, 1.0, 0.0)


def _softmax_rows(s):
    e = jnp.exp(s - jnp.max(s, -1, keepdims=True))
    return e / jnp.sum(e, -1, keepdims=True)


def _flash_step(qk, kt, vt, ok, carry):
    m, l, acc = carry
    s = lax.dot_general(qk, kt, (((1,), (1,)), ((), ())), preferred_element_type=f32)
    s = jnp.where(ok, s, NEG_INF)
    m_new = jnp.maximum(m, jnp.max(s, -1, keepdims=True))
    alpha = jnp.exp(m - m_new)
    p = jnp.exp(s - m_new)
    l = alpha * l + jnp.sum(p, -1, keepdims=True)
    acc = alpha * acc + jnp.dot(p.astype(bf16), vt, preferred_element_type=f32)
    return m_new, l, acc


def _flash_init(rows):
    return (jnp.full((rows, 1), NEG_INF, f32), jnp.zeros((rows, 1), f32), jnp.zeros((rows, NSA_HD), f32))


def _rope_rows(y, cos_t, sin_t, heads):
    half = NSA_HD // 2
    out = []
    for h in range(heads):
        top = y[NSA_HD * h:NSA_HD * h + half]
        bot = y[NSA_HD * h + half:NSA_HD * (h + 1)]
        out += [top * cos_t - bot * sin_t, bot * cos_t + top * sin_t]
    return jnp.concatenate(out, axis=0)


def _nsa_in_kernel(x_ref, wn_ref, wq_ref, wr_ref, ww_ref, wg_ref, cos_ref, sin_ref, cost_ref, sint_ref,
                   nat_ref, qt_ref, rt_ref, wt_ref, gt_ref):
    x = x_ref[...].astype(bf16)
    nat = jnp.dot(x, wn_ref[...], preferred_element_type=f32)
    nat_ref[...] = _rope_cols(nat, cos_ref[...], sin_ref[...], (4, 5, 6, 7))
    nt = (((1,), (1,)), ((), ()))
    cos_t, sin_t = cost_ref[...], sint_ref[...]
    q_t = lax.dot_general(wq_ref[...], x, nt, preferred_element_type=f32)
    qt_ref[...] = _rope_rows(q_t, cos_t, sin_t, NSA_HEADS) * NSA_SCALE
    r_t = lax.dot_general(wr_ref[...], x, nt, preferred_element_type=f32)
    rt_ref[0:2 * NSA_HK, :] = r_t[0:2 * NSA_HK]
    rt_ref[2 * NSA_HK:3 * NSA_HK, :] = _rope_rows(r_t[2 * NSA_HK:3 * NSA_HK], cos_t, sin_t, NSA_KV_HEADS)
    rt_ref[3 * NSA_HK:4 * NSA_HK, :] = r_t[3 * NSA_HK:4 * NSA_HK]
    w_t = lax.dot_general(ww_ref[...], x, nt, preferred_element_type=f32)
    wt_ref[0:NSA_HK, :] = _rope_rows(w_t[0:NSA_HK], cos_t, sin_t, NSA_KV_HEADS)
    wt_ref[NSA_HK:2 * NSA_HK, :] = w_t[NSA_HK:2 * NSA_HK]
    gt_ref[...] = jax.nn.sigmoid(lax.dot_general(wg_ref[...], x, nt, preferred_element_type=f32))


def _nsa_in(x, nb, w_in, tm=256):
    m, d = x.shape
    s_len = m // nb
    assert s_len % tm == 0
    hq, hk = NSA_HQ, NSA_HK
    w_q = w_in[:, :hq]
    w_kv = w_in[:, hq:hq + 6 * hk].reshape(d, 6, hk)
    w_g = w_in[:, hq + 6 * hk:]
    w_nat = w_kv[:, jnp.array([0, 1, 2, 4])].reshape(d, 4 * hk).astype(bf16)
    w_q_t = w_q.T.astype(bf16)
    w_rows_t = w_kv[:, 0:4].reshape(d, 4 * hk).T.astype(bf16)
    w_win_t = w_kv[:, 4:6].reshape(d, 2 * hk).T.astype(bf16)
    w_g_t = w_g.T.astype(bf16)
    pos = jnp.arange(s_len, dtype=f32)
    cos, sin = _rope_tables(pos)
    inv = ROPE_THETA ** (-jnp.arange(0, NSA_HD, 2, dtype=f32) / NSA_HD)
    ang_t = inv[:, None] * pos[None, :]
    nt = s_len // tm
    ng = 3 * NSA_HEADS
    t_spec = lambda r: pl.BlockSpec((None, r, tm), lambda b, i: (b, 0, i))
    tab = pl.BlockSpec((tm, 128), lambda b, i: (i, 0))
    tab_t = pl.BlockSpec((NSA_HD // 2, tm), lambda b, i: (0, i))
    return pl.pallas_call(
        _nsa_in_kernel, grid=(nb, nt),
        in_specs=[pl.BlockSpec((tm, d), lambda b, i: (b * nt + i, 0)),
                  _const_spec(w_nat.shape), _const_spec(w_q_t.shape), _const_spec(w_rows_t.shape),
                  _const_spec(w_win_t.shape), _const_spec(w_g_t.shape), tab, tab, tab_t, tab_t],
        out_specs=[pl.BlockSpec((tm, 4 * hk), lambda b, i: (b * nt + i, 0)), t_spec(hq), t_spec(4 * hk),
                   t_spec(2 * hk), t_spec(ng)],
        out_shape=[jax.ShapeDtypeStruct((m, 4 * hk), f32), jax.ShapeDtypeStruct((nb, hq, s_len), f32),
                   jax.ShapeDtypeStruct((nb, 4 * hk, s_len), f32), jax.ShapeDtypeStruct((nb, 2 * hk, s_len), f32),
                   jax.ShapeDtypeStruct((nb, ng, s_len), f32)],
        compiler_params=_cparams(("parallel", "parallel")), name="nsa_in",
    )(x, w_nat, w_q_t, w_rows_t, w_win_t, w_g_t, cos, sin, jnp.cos(ang_t), jnp.sin(ang_t))


def _flash_cols(k_tile, v_t, q_ref, bias, m_ref, l_ref, acc_ref):
    qw = NSA_TILE
    for g in range(q_ref.shape[1] // qw):
        c = slice(g * qw, (g + 1) * qw)
        s = jnp.dot(k_tile, q_ref[:, c], preferred_element_type=f32)
        if bias is not None:
            s = s + bias
        m_old = m_ref[:, c]
        m_new = jnp.maximum(m_old, jnp.max(s, 0, keepdims=True))
        alpha = jnp.exp(m_old - m_new)
        p = jnp.exp(s - m_new)
        l_ref[:, c] = alpha * l_ref[:, c] + jnp.sum(p, 0, keepdims=True)
        m_ref[:, c] = m_new
        acc_ref[:, c] = alpha * acc_ref[:, c] + jnp.dot(v_t, p.astype(bf16), preferred_element_type=f32)


def _nsa_seq_kernel(qt_ref, gt_ref, kcb_ref, vcbt_ref, knat_ref, vst_ref, vwt_ref, o_ref,
                    q_sc, oc_sc, os_sc, m_sc, l_sc, acc_sc, *, s_len):
    i = pl.program_id(1)
    qb = NSA_TILE
    nc = s_len // CMP_BLOCK
    nsb = s_len // SEL_BLOCK
    cols = NSA_GROUP * qb
    tile4 = lambda a: jnp.concatenate([a] * NSA_GROUP, axis=1)
    t_q = i * qb + lax.broadcasted_iota(jnp.int32, (1, qb), 1)
    t_cols = tile4(t_q)
    cl = lax.broadcasted_iota(jnp.int32, (nc, 1), 0)
    cblk = jnp.where(cl < nc // 2, 2 * cl, 2 * (cl - nc // 2) + 1)
    cmask = (cblk + 1) * CMP_BLOCK - 1 <= t_cols
    has_cmp = jnp.where(t_cols >= CMP_BLOCK - 1, 1.0, 0.0)
    blk = lax.broadcasted_iota(jnp.int32, (nsb, 1), 0)
    valid = blk * SEL_BLOCK <= t_q
    forced = (blk == 0) | (blk == t_q // SEL_BLOCK)
    key_in = lax.broadcasted_iota(jnp.int32, (NSA_TILE, 1), 0)
    blk_of_key = lax.broadcasted_iota(jnp.int32, (NSA_TILE, 2 * NSA_HD), 0) // SEL_BLOCK
    lane = lax.broadcasted_iota(jnp.int32, (NSA_TILE, 2 * NSA_HD), 1)
    own_lanes = [lane < NSA_HD, lane >= NSA_HD]
    gates = gt_ref[...]
    kcols = [slice(NSA_HD * k, NSA_HD * (k + 1)) for k in range(NSA_KV_HEADS)]
    wcols = [slice(NSA_HK + NSA_HD * k, NSA_HK + NSA_HD * (k + 1)) for k in range(NSA_KV_HEADS)]

    def reset():
        m_sc[...] = jnp.full(m_sc.shape, NEG_INF, f32)
        l_sc[...] = jnp.zeros_like(l_sc)
        acc_sc[...] = jnp.zeros_like(acc_sc)

    for k in range(NSA_KV_HEADS):
        kcol = kcols[k]
        q_t = jnp.concatenate([qt_ref[NSA_HD * (NSA_GROUP * k + g):NSA_HD * (NSA_GROUP * k + g + 1), :]
                               for g in range(NSA_GROUP)], axis=1).astype(bf16)

        s_c = jnp.dot(kcb_ref[:, kcol].astype(bf16), q_t, preferred_element_type=f32)
        s_c = jnp.where(cmask, s_c, NEG_INF)
        e = jnp.exp(s_c - jnp.max(s_c, 0, keepdims=True))
        p_c = e / jnp.sum(e, 0, keepdims=True) * has_cmp
        o_c = jnp.dot(vcbt_ref[kcol, :].astype(bf16), p_c.astype(bf16), preferred_element_type=f32)

        imp = p_c[:, 0:qb]
        for g in range(1, NSA_GROUP):
            imp = imp + p_c[:, g * qb:(g + 1) * qb]
        imp = imp[:nc // 2] + imp[nc // 2:]
        score = jnp.where(forced, FORCE_SCORE, jnp.where(valid, imp, -1.0))
        sel = _topk_mask_rows(score, SEL_TOPK)
        sel_bias = tile4(jnp.where(sel > 0.5, 0.0, NEG_INF)).astype(bf16)
        fill = jnp.zeros((NSA_HD - nsb, cols), bf16)
        q_sc[k] = jnp.concatenate([q_t, sel_bias, fill] if k % 2 == 0 else [sel_bias, fill, q_t], axis=0)
        oc_sc[k] = o_c

    def sel_tile(j, bias):
        off = pl.multiple_of(j * NSA_TILE, NSA_TILE)
        keys = pl.ds(off, NSA_TILE)
        blk_key = blk_of_key + j * (NSA_TILE // SEL_BLOCK)
        onehot = [jnp.where(lane - NSA_HD == blk_key, 1.0, 0.0).astype(bf16),
                  jnp.where(lane == blk_key, 1.0, 0.0).astype(bf16)]
        for pair in range(NSA_KV_HEADS // 2):
            k_both = knat_ref[keys, 2 * NSA_HD * pair:2 * NSA_HD * (pair + 1)].astype(bf16)
            for own in range(2):
                k = 2 * pair + own
                k_aug = jnp.where(own_lanes[own], k_both, onehot[own])
                _flash_cols(k_aug, vst_ref[kcols[k], keys].astype(bf16), q_sc.at[k], bias,
                            m_sc.at[k], l_sc.at[k], acc_sc.at[k])

    def win_tile(j, bias):
        off = pl.multiple_of(j * NSA_TILE, NSA_TILE)
        keys = pl.ds(off, NSA_TILE)
        for k in range(NSA_KV_HEADS):
            _flash_cols(knat_ref[keys, wcols[k]].astype(bf16), vwt_ref[kcols[k], keys].astype(bf16),
                        q_sc.at[k, pl.ds(NSA_HD * (k % 2), NSA_HD)], bias, m_sc.at[k], l_sc.at[k], acc_sc.at[k])

    def causal_bias(j):
        return jnp.where(j * NSA_TILE + key_in <= t_q, 0.0, NEG_INF)

    def window_bias(j):
        kpos = j * NSA_TILE + key_in
        return jnp.where((kpos <= t_q) & (kpos > t_q - WINDOW), 0.0, NEG_INF)

    def loop(tile_fn, lo, hi):
        def body(j, carry):
            tile_fn(j, None)
            return carry
        lax.fori_loop(lo, hi, body, 0)

    reset()
    loop(sel_tile, 0, i)
    sel_tile(i, causal_bias(i))
    os_sc[...] = acc_sc[...] / l_sc[...]

    reset()
    n_back = WINDOW // NSA_TILE

    @pl.when(i >= n_back)
    def _():
        win_tile(i - n_back, window_bias(i - n_back))

    loop(win_tile, jnp.maximum(i - n_back + 1, 0), i)
    win_tile(i, window_bias(i))

    for k in range(NSA_KV_HEADS):
        o_c = oc_sc[k]
        o_s = os_sc[k]
        o_w = acc_sc[k] / l_sc[k]
        for pair in range(NSA_GROUP // 2):
            o_t = []
            for g in (2 * pair, 2 * pair + 1):
                h = NSA_GROUP * k + g
                c = slice(g * qb, (g + 1) * qb)
                o_t.append(gates[3 * h:3 * h + 1, :] * o_c[:, c] + gates[3 * h + 1:3 * h + 2, :] * o_s[:, c]
                           + gates[3 * h + 2:3 * h + 3, :] * o_w[:, c])
            lane0 = NSA_HD * (NSA_GROUP * k + 2 * pair)
            o_ref[:, lane0:lane0 + 2 * NSA_HD] = jnp.concatenate(o_t, axis=0).T


def _nsa_seq(q_t, gates_t, kcb, vcb_t, nat, rows_t, wrows_t):
    nb, _, s_len = q_t.shape
    assert s_len % NSA_TILE == 0 and (s_len // CMP_BLOCK) % 2 == 0 and (s_len // SEL_BLOCK) % 8 == 0
    assert s_len // SEL_BLOCK <= NSA_HD and 2 * NSA_HD == NSA_TILE
    nc = s_len // CMP_BLOCK
    hk = NSA_HK
    cols = NSA_GROUP * NSA_TILE
    per_b = lambda shape, idx: pl.BlockSpec((None,) + shape, lambda b, i: (b,) + idx)
    return pl.pallas_call(
        functools.partial(_nsa_seq_kernel, s_len=s_len), grid=(nb, s_len // NSA_TILE),
        in_specs=[pl.BlockSpec((None, NSA_HQ, NSA_TILE), lambda b, i: (b, 0, i)),
                  pl.BlockSpec((None, 3 * NSA_HEADS, NSA_TILE), lambda b, i: (b, 0, i)),
                  per_b((nc, hk), (0, 0)), per_b((hk, nc), (0, 0)),
                  per_b((s_len, 2 * hk), (0, 1)),
                  per_b((hk, s_len), (3, 0)),
                  per_b((hk, s_len), (1, 0))],
        out_specs=pl.BlockSpec((None, NSA_TILE, NSA_HQ), lambda b, i: (b, i, 0)),
        out_shape=jax.ShapeDtypeStruct((nb, s_len, NSA_HQ), f32),
        scratch_shapes=[pltpu.VMEM((NSA_KV_HEADS, 2 * NSA_HD, cols), bf16),
                        pltpu.VMEM((NSA_KV_HEADS, NSA_HD, cols), f32), pltpu.VMEM((NSA_KV_HEADS, NSA_HD, cols), f32),
                        pltpu.VMEM((NSA_KV_HEADS, 1, cols), f32), pltpu.VMEM((NSA_KV_HEADS, 1, cols), f32),
                        pltpu.VMEM((NSA_KV_HEADS, NSA_HD, cols), f32)],
        compiler_params=_cparams(("parallel", "arbitrary")), name="nsa_seq",
    )(q_t, gates_t, kcb, vcb_t, nat, rows_t, wrows_t)


DEC_CHUNK_PAGES = 8


def _flash_step_t(qk, k_t, v_t, ok, carry):
    m, l, acc = carry
    s = jnp.dot(qk, k_t, preferred_element_type=f32)
    s = jnp.where(ok, s, NEG_INF)
    m_new = jnp.maximum(m, jnp.max(s, -1, keepdims=True))
    alpha = jnp.exp(m - m_new)
    p = jnp.exp(s - m_new)
    l = alpha * l + jnp.sum(p, -1, keepdims=True)
    acc = alpha * acc + lax.dot_general(p.astype(bf16), v_t, (((1,), (1,)), ((), ())), preferred_element_type=f32)
    return m_new, l, acc


def _nsa_dec_kernel(pt_ref, q_ref, g_ref, cos_ref, sin_ref, pool_hbm, new_ref, win_ref, wnew_ref, o_ref,
                    buf, sems, cmp_sc, sel_sc, m_sc, l_sc, acc_sc, *, past_len, n_new):
    b = pl.program_id(0)
    rows = NSA_GROUP * n_new
    t_len = past_len + n_new
    nc = t_len // CMP_BLOCK
    nsb = -(-t_len // SEL_BLOCK)
    n_pages = past_len // NSA_TILE
    cpg = DEC_CHUNK_PAGES
    n_chunks = n_pages // cpg
    ckeys = cpg * NSA_TILE
    new_pad = new_ref.shape[0]
    qi = lax.broadcasted_iota(jnp.int32, (rows, 1), 0) % n_new
    t_rows = past_len + qi

    def page_copy(c, i):
        half = c // n_chunks
        page = pt_ref[b, (c % n_chunks) * cpg + i]
        return pltpu.make_async_copy(pool_hbm.at[page, half], buf.at[c % 2, :, pl.ds(i * NSA_TILE, NSA_TILE)],
                                     sems.at[c % 2])

    def start_chunk(c):
        for i in range(cpg):
            page_copy(c, i).start()

    def wait_chunk(c):
        for i in range(cpg):
            page_copy(c, i).wait()

    start_chunk(0)
    cmp_sc[...] = jnp.zeros_like(cmp_sc)

    def pass1(c, carry):
        start_chunk(c + 1)
        wait_chunk(c)
        x = buf[c % 2]
        rr = lax.broadcasted_iota(jnp.int32, (ckeys, nc), 0)
        ll = lax.broadcasted_iota(jnp.int32, (ckeys, nc), 1)
        blk = c * (ckeys // CMP_BLOCK) + rr // CMP_BLOCK
        lane_of_blk = jnp.where(blk % 2 == 0, blk // 2, nc // 2 + blk // 2)
        avg = jnp.where(ll == lane_of_blk, 1.0 / CMP_BLOCK, 0.0).astype(bf16)
        hi = x.astype(bf16)
        r1 = x - hi.astype(f32)
        mid = r1.astype(bf16)
        lo = (r1 - mid.astype(f32)).astype(bf16)
        cmp_sc[...] += (jnp.dot(hi, avg, preferred_element_type=f32) + jnp.dot(mid, avg, preferred_element_type=f32)
                        + jnp.dot(lo, avg, preferred_element_type=f32))
        return carry

    lax.fori_loop(0, n_chunks, pass1, 0)

    cl = lax.broadcasted_iota(jnp.int32, (1, nc), 1)
    cblk = jnp.where(cl < nc // 2, 2 * cl, 2 * (cl - nc // 2) + 1)
    cmask = (cblk + 1) * CMP_BLOCK - 1 <= t_rows
    rr = lax.broadcasted_iota(jnp.int32, (rows, rows), 0) % n_new
    rc = lax.broadcasted_iota(jnp.int32, (rows, rows), 1) % n_new
    group_sum = jnp.where(rr == rc, 1.0, 0.0)
    blk = lax.broadcasted_iota(jnp.int32, (rows, nc), 1)
    valid = blk * SEL_BLOCK <= t_rows
    forced = (blk == 0) | (blk == t_rows // SEL_BLOCK)
    half_d = NSA_HD // 2
    o_c = []
    for k in range(NSA_KV_HEADS):
        qk = q_ref[k].astype(bf16)
        top = cmp_sc[NSA_HD * k:NSA_HD * k + half_d, :]
        bot = cmp_sc[NSA_HD * k + half_d:NSA_HD * (k + 1), :]
        cos, sin = cos_ref[...], sin_ref[...]
        kcb_t = jnp.concatenate([top * cos - bot * sin, bot * cos + top * sin], axis=0).astype(bf16)
        vcb_t = cmp_sc[NSA_HK + NSA_HD * k:NSA_HK + NSA_HD * (k + 1), :].astype(bf16)
        s_c = jnp.dot(qk, kcb_t, preferred_element_type=f32)
        p_c = _softmax_rows(jnp.where(cmask, s_c, NEG_INF))
        p_c = p_c * jnp.where(t_rows >= CMP_BLOCK - 1, 1.0, 0.0)
        o_c.append(lax.dot_general(p_c.astype(bf16), vcb_t, (((1,), (1,)), ((), ())), preferred_element_type=f32))
        imp = jnp.dot(group_sum, p_c, precision=lax.Precision.HIGHEST, preferred_element_type=f32)
        imp = imp[:, :nc // 2] + imp[:, nc // 2:]
        imp = jnp.concatenate([imp, jnp.zeros((rows, nc - nc // 2), f32)], axis=1)
        score = jnp.where(forced, FORCE_SCORE, jnp.where(valid, imp, -1.0))
        score = jnp.where(blk < nsb, score, -2.0)
        sel_sc[k] = _topk_mask_lanes(score, SEL_TOPK, nsb)
        m_sc[k] = jnp.full((rows, 1), NEG_INF, f32)
        l_sc[k] = jnp.zeros((rows, 1), f32)
        acc_sc[k] = jnp.zeros((rows, NSA_HD), f32)

    def pass2(c, carry):
        @pl.when(c + 1 < 2 * n_chunks)
        def _():
            start_chunk(c + 1)

        wait_chunk(c)
        key0 = (c - n_chunks) * ckeys
        eb = lax.broadcasted_iota(jnp.int32, (nc, ckeys), 0)
        ek = (key0 + lax.broadcasted_iota(jnp.int32, (nc, ckeys), 1)) // SEL_BLOCK
        expand = jnp.where(eb == ek, 1.0, 0.0).astype(bf16)
        kpos = key0 + lax.broadcasted_iota(jnp.int32, (1, ckeys), 1)
        for k in range(NSA_KV_HEADS):
            qk = q_ref[k].astype(bf16)
            k_t = buf[c % 2, NSA_HD * k:NSA_HD * (k + 1), :].astype(bf16)
            v_t = buf[c % 2, NSA_HK + NSA_HD * k:NSA_HK + NSA_HD * (k + 1), :].astype(bf16)
            msk = jnp.dot(sel_sc[k].astype(bf16), expand, preferred_element_type=f32)
            ok = (msk > 0.5) & (kpos <= t_rows)
            m_sc[k], l_sc[k], acc_sc[k] = _flash_step_t(qk, k_t, v_t, ok, (m_sc[k], l_sc[k], acc_sc[k]))
        return carry

    lax.fori_loop(n_chunks, 2 * n_chunks, pass2, 0)

    jn = lax.broadcasted_iota(jnp.int32, (1, new_pad), 1)
    npos = past_len + jn
    wpos = past_len - WINDOW + lax.broadcasted_iota(jnp.int32, (1, WINDOW), 1)
    for k in range(NSA_KV_HEADS):
        qk = q_ref[k].astype(bf16)
        kc = slice(NSA_HD * k, NSA_HD * (k + 1))
        vc = slice(NSA_HK + NSA_HD * k, NSA_HK + NSA_HD * (k + 1))
        in_last = sel_sc[k][:, nsb - 1:nsb] > 0.5
        ok = in_last & (npos <= t_rows) & (jn < n_new)
        _, l_s, acc_s = _flash_step(qk, new_ref[:, kc].astype(bf16), new_ref[:, vc].astype(bf16), ok,
                                    (m_sc[k], l_sc[k], acc_sc[k]))
        o_s = acc_s / l_s
        ok1 = (wpos <= t_rows) & (wpos > t_rows - WINDOW)
        carry = _flash_step_t(qk, win_ref[kc, :].astype(bf16), win_ref[vc, :].astype(bf16), ok1, _flash_init(rows))
        ok2 = (npos <= t_rows) & (npos > t_rows - WINDOW) & (jn < n_new)
        _, l_w, acc_w = _flash_step(qk, wnew_ref[:, kc].astype(bf16), wnew_ref[:, vc].astype(bf16), ok2, carry)
        o_w = acc_w / l_w
        g = g_ref[k]
        o_ref[k] = g[:, 0:1] * o_c[k] + g[:, 1:2] * o_s + g[:, 2:3] * o_w


def _nsa_dec(q, gates, cos_t, sin_t, pool_v, page_table, new_rows, win_t, new_wrows, past_len, n_new):
    nb, n_pages = page_table.shape
    rows = NSA_GROUP * n_new
    w = 2 * NSA_HK
    nc = (past_len + n_new) // CMP_BLOCK
    assert past_len == n_pages * NSA_TILE and past_len % SEL_BLOCK == 0 and n_new <= CMP_BLOCK
    assert win_t.shape[2] == WINDOW and past_len >= WINDOW and n_pages % DEC_CHUNK_PAGES == 0
    assert nc == past_len // CMP_BLOCK and nc % 2 == 0 and -(-(past_len + n_new) // SEL_BLOCK) <= nc
    assert pool_v.shape[1:] == (2, w, NSA_TILE) and cos_t.shape == (NSA_HD // 2, nc)
    per_b = lambda shape: pl.BlockSpec((None,) + shape, lambda b, pt: (b,) + (0,) * len(shape))
    const = lambda shape: pl.BlockSpec(shape, lambda b, pt: (0,) * len(shape))
    grid_spec = pltpu.PrefetchScalarGridSpec(
        num_scalar_prefetch=1, grid=(nb,),
        in_specs=[per_b((NSA_KV_HEADS, rows, NSA_HD)), per_b((NSA_KV_HEADS, rows, 3)),
                  const(cos_t.shape), const(sin_t.shape), pl.BlockSpec(memory_space=pl.ANY),
                  pl.BlockSpec((None, new_rows.shape[1], w), lambda b, pt: (b, 0, 1)),
                  per_b((w, WINDOW)), per_b((new_wrows.shape[1], w))],
        out_specs=per_b((NSA_KV_HEADS, rows, NSA_HD)),
        scratch_shapes=[pltpu.VMEM((2, w, DEC_CHUNK_PAGES * NSA_TILE), f32), pltpu.SemaphoreType.DMA((2,)),
                        pltpu.VMEM((w, nc), f32), pltpu.VMEM((NSA_KV_HEADS, rows, nc), f32),
                        pltpu.VMEM((NSA_KV_HEADS, rows, 1), f32), pltpu.VMEM((NSA_KV_HEADS, rows, 1), f32),
                        pltpu.VMEM((NSA_KV_HEADS, rows, NSA_HD), f32)])
    return pl.pallas_call(
        functools.partial(_nsa_dec_kernel, past_len=past_len, n_new=n_new), grid_spec=grid_spec,
        out_shape=jax.ShapeDtypeStruct((nb, NSA_KV_HEADS, rows, NSA_HD), f32),
        compiler_params=_cparams(("arbitrary",)), name="nsa_dec",
    )(page_table, q, gates, cos_t, sin_t, pool_v, new_rows, win_t, new_wrows)


def _nsa_layer(xp, xs, nbp, nbs, pool, win, page_table, ln_g, ln_b, w_in, w_out):
    hq, hk = NSA_HQ, NSA_HK
    ws = [w_in[:, :hq].astype(bf16), w_in[:, hq:hq + 4 * hk].astype(bf16),
          w_in[:, hq + 4 * hk:hq + 6 * hk].astype(bf16), w_in[:, hq + 6 * hk:].astype(bf16)]
    wo = w_out.astype(bf16)
    eps = [_q_ep, _rows_ep, _wrows_ep, _sigmoid_ep]
    sp_len = xp.shape[0] // nbp
    ss_len = xs.shape[0] // nbs
    n_pages = page_table.shape[1]
    page = pool.shape[1]
    assert page == NSA_TILE
    past_len = n_pages * page
    ms = xs.shape[0]

    cos_s, sin_s = _rope_tables(jnp.tile(past_len + jnp.arange(ss_len, dtype=f32), nbs))
    nat_p, q_t, rows_t, wrows_t, g_t = _nsa_in(xp, nbp, w_in)
    qs, rows_s, wrows_s, gs = _proj(xs, ws, eps, aux=(cos_s, sin_s), aux_period=(ms, ms), tm=ms, name="nsa_in_s")

    def cmp_tables(nc):
        return _rope_tables(jnp.arange(nc, dtype=f32) * CMP_BLOCK + (CMP_BLOCK - 1) / 2.0)

    pages_p = sp_len // page
    pt_p = jnp.arange(nbp * pages_p, dtype=jnp.int32).reshape(nbp, pages_p)
    cmp_p = _even_odd(_nsa_cmp(nat_p.reshape(nbp * pages_p, page, 4 * hk), pt_p, *cmp_tables(sp_len // CMP_BLOCK)))
    op = _nsa_seq(q_t, g_t, cmp_p[:, :, :hk], cmp_p[:, :, hk:].transpose(0, 2, 1),
                  nat_p.reshape(nbp, sp_len, 4 * hk), rows_t, wrows_t)

    pool_v = pool.transpose(0, 2, 3, 4, 1).reshape(pool.shape[0], 2, 2 * hk, page)
    win_t = win.transpose(0, 2, 3, 4, 1).reshape(nbs, 2 * hk, win.shape[1])
    nc_s = past_len // CMP_BLOCK
    cpos = jnp.arange(nc_s, dtype=f32) * CMP_BLOCK + (CMP_BLOCK - 1) / 2.0
    cpos = jnp.concatenate([cpos[0::2], cpos[1::2]])
    inv = ROPE_THETA ** (-jnp.arange(0, NSA_HD, 2, dtype=f32) / NSA_HD)
    ang_t = inv[:, None] * cpos[None, :]
    rows_s3 = rows_s.reshape(nbs, ss_len, 4 * hk)
    wrows_s3 = wrows_s.reshape(nbs, ss_len, 2 * hk)
    new_pad = 8
    tpad = lambda a: jnp.pad(a, ((0, 0), (0, new_pad - ss_len), (0, 0)))
    to_heads = lambda a, last: (a.reshape(nbs, ss_len, NSA_KV_HEADS, NSA_GROUP, last).transpose(0, 2, 3, 1, 4)
                                .reshape(nbs, NSA_KV_HEADS, NSA_GROUP * ss_len, last))
    os_ = _nsa_dec(to_heads(qs, NSA_HD), to_heads(gs, 3), jnp.cos(ang_t), jnp.sin(ang_t), pool_v, page_table,
                   tpad(rows_s3), win_t, tpad(wrows_s3), past_len, ss_len)
    os_ = (os_.reshape(nbs, NSA_KV_HEADS, NSA_GROUP, ss_len, NSA_HD).transpose(0, 3, 1, 2, 4).reshape(ms, hq))

    xp = _outproj_ln(op.reshape(nbp * sp_len, hq), wo, xp, ln_g, ln_b, name="nsa_out")
    xs = _outproj_ln(os_, wo, xs, ln_g, ln_b, tm=ms, name="nsa_out_s")
    kv_shape = (4, NSA_KV_HEADS, NSA_HD)
    win_shape = (2, NSA_KV_HEADS, NSA_HD)
    from_t = lambda a, shape: a.reshape((a.shape[0],) + shape + (a.shape[2],)).transpose(0, 4, 1, 2, 3)
    rp = from_t(rows_t, kv_shape)
    rs = rows_s.reshape((nbs, ss_len) + kv_shape)
    wp = from_t(wrows_t[:, :, sp_len - min(WINDOW, sp_len):], win_shape)
    wsn_t = jnp.concatenate([win_t, wrows_s3.transpose(0, 2, 1)], axis=2)
    wsn = from_t(wsn_t[:, :, wsn_t.shape[2] - min(WINDOW, wsn_t.shape[2]):], win_shape)
    return xp, xs, (rp, rs, wp, wsn)


FFN_TF = 256
MOE_TF = 512
ROW_TILE = 1024


def kernel(x_prompt, x_sample, state_l0_lru_conv, state_l0_lru_h, cache_l1_nsa_kv, cache_l1_nsa_win, page_table,
           state_l2_ssd_conv, state_l2_ssd_ssm, state_l3_lru_conv, state_l3_lru_h, ln_g, ln_b, lru_w_in,
           lru_conv_w, lru_conv_b, lru_w_a, lru_b_a, lru_w_x, lru_b_x, lru_lam, lru_w_out, nsa_w_in, nsa_w_out,
           ssd_w_in, ssd_conv_w, ssd_conv_b, ssd_dt_bias, ssd_a_log, ssd_d, ssd_norm_g, ssd_w_out, ffn_w_in,
           ffn_w_out, moe_router_w, moe_router_b, moe_w_in, moe_w_out):
    nbp, sp_len, d = x_prompt.shape
    nbs, ss_len, _ = x_sample.shape
    xp = x_prompt.reshape(nbp * sp_len, d)
    xs = x_sample.reshape(nbs * ss_len, d)
    ms = xs.shape[0]
    lru_state = {0: (state_l0_lru_conv, state_l0_lru_h), 3: (state_l3_lru_conv, state_l3_lru_h)}
    new = {}
    for i in range(DEPTH):
        kind, j = i % 3, i // 3
        g0, b0 = ln_g[i, 0], ln_b[i, 0]
        if kind == 0:
            conv_s, h_s = lru_state[i]
            xp, xs, new[i] = _lru_layer(xp, xs, nbp, nbs, conv_s, h_s, g0, b0, lru_w_in[j], lru_conv_w[j],
                                        lru_conv_b[j], lru_w_a[j], lru_b_a[j], lru_w_x[j], lru_b_x[j],
                                        lru_lam[j], lru_w_out[j])
        elif kind == 1:
            xp, xs, new[i] = _nsa_layer(xp, xs, nbp, nbs, cache_l1_nsa_kv, cache_l1_nsa_win, page_table, g0, b0,
                                        nsa_w_in[j], nsa_w_out[j])
        else:
            xp, xs, new[i] = _ssd_layer(xp, xs, nbp, nbs, state_l2_ssd_conv, state_l2_ssd_ssm, g0, b0,
                                        ssd_w_in[j], ssd_conv_w[j], ssd_conv_b[j], ssd_dt_bias[j], ssd_a_log[j],
                                        ssd_d[j], ssd_norm_g[j], ssd_w_out[j])
        g1, b1 = ln_g[i, 1], ln_b[i, 1]
        k = i // 2
        if i % 2 == 0:
            xp = _ffn_ln(xp, ffn_w_in, ffn_w_out, k, g1, b1, ROW_TILE, FFN_TF, name="ffn")
            xs = _ffn_ln(xs, ffn_w_in, ffn_w_out, k, g1, b1, ms, FFN_TF, name="ffn_s")
        else:
            xp = _moe_sparse_ln(xp, moe_router_w[k], moe_router_b[k], moe_w_in, moe_w_out, k, g1, b1, MOE_TF)
            cs = _router(xs, moe_router_w[k], moe_router_b[k], tm=ms)
            xs = _moe_ln(xs, cs, moe_w_in, moe_w_out, k, g1, b1, ms, MOE_TF)
    out = [xp.reshape(nbp, sp_len, d), xs.reshape(nbs, ss_len, d)]
    for i in range(DEPTH):
        out.extend(new[i])
    return tuple(out)
```

```python
import functools
import math

import jax
import jax.numpy as jnp
from jax import lax
from jax.experimental import pallas as pl
from jax.experimental.pallas import tpu as pltpu

f32 = jnp.float32
bf16 = jnp.bfloat16

D_MODEL = 1024
DEPTH = 4
ALPHA = (2 * DEPTH) ** 0.25
LN_EPS = 1e-5
RMS_EPS = 1e-5
CONV_W = 4
NEG_INF = -1e30

D_RNN = 1344
LRU_BLOCKS = 16
LRU_BS = D_RNN // LRU_BLOCKS
LRU_C = 8.0

NSA_HEADS = 16
NSA_KV_HEADS = 4
NSA_HD = 64
NSA_GROUP = NSA_HEADS // NSA_KV_HEADS
CMP_BLOCK = 32
SEL_BLOCK = 64
SEL_TOPK = 16
WINDOW = 512
FORCE_SCORE = 1e4
ROPE_THETA = 10000.0
NSA_HQ = NSA_HEADS * NSA_HD
NSA_HK = NSA_KV_HEADS * NSA_HD

SSD_INNER = 2 * D_MODEL
SSD_HEAD_DIM = 64
SSD_HEADS = SSD_INNER // SSD_HEAD_DIM
SSD_GROUPS = 4
SSD_STATE = 128
SSD_CHUNK = 128
SSD_CONV_DIM = SSD_INNER + 2 * SSD_GROUPS * SSD_STATE

N_EXPERTS = 8
TOP_K = 2

VMEM_LIMIT_V7X = 56 * 1024 * 1024


def _cparams(sem):
    return pltpu.CompilerParams(dimension_semantics=sem, vmem_limit_bytes=VMEM_LIMIT_V7X)


def _const_spec(shape):
    nd = len(shape)
    return pl.BlockSpec(shape, lambda *_: (0,) * nd, pipeline_mode=pl.Buffered(1))


def _layer_norm(v, g, b):
    mu = jnp.mean(v, -1, keepdims=True)
    d = v - mu
    var = jnp.mean(d * d, -1, keepdims=True)
    return d * lax.rsqrt(var + LN_EPS) * g + b


def _proj_kernel(*refs, epilogues, n_aux):
    n_out = len(epilogues)
    x_ref = refs[0]
    w_refs = refs[1:1 + n_out]
    aux_refs = refs[1 + n_out:1 + n_out + n_aux]
    o_refs = refs[1 + n_out + n_aux:]
    x = x_ref[...].astype(bf16)
    aux = [a[...] for a in aux_refs]
    for w_ref, o_ref, ep in zip(w_refs, o_refs, epilogues):
        y = jnp.dot(x, w_ref[...], preferred_element_type=f32)
        if ep is not None:
            y = ep(y, *aux)
        o_ref[...] = y.astype(o_ref.dtype)


def _proj(x, ws, epilogues, aux=(), aux_period=(), tm=256, name="proj"):
    m, k = x.shape
    assert m % tm == 0
    in_specs = [pl.BlockSpec((tm, k), lambda i: (i, 0))]
    in_specs += [_const_spec(w.shape) for w in ws]
    for a, p in zip(aux, aux_period):
        if p:
            assert p % tm == 0 and a.shape[0] == p
            in_specs.append(pl.BlockSpec((tm, a.shape[1]), functools.partial(lambda i, n: (i % n, 0), n=p // tm)))
        else:
            in_specs.append(_const_spec(a.shape))
    out_shape = [jax.ShapeDtypeStruct((m, w.shape[1]), f32) for w in ws]
    out_specs = [pl.BlockSpec((tm, w.shape[1]), lambda i: (i, 0)) for w in ws]
    return pl.pallas_call(
        functools.partial(_proj_kernel, epilogues=tuple(epilogues), n_aux=len(aux)),
        grid=(m // tm,), in_specs=in_specs, out_specs=out_specs, out_shape=out_shape,
        compiler_params=_cparams(("parallel",)), name=name,
    )(x, *ws, *aux)


def _outproj_ln_kernel(a_ref, w_ref, x_ref, g_ref, b_ref, o_ref):
    y = jnp.dot(a_ref[...].astype(bf16), w_ref[...], preferred_element_type=f32)
    o_ref[...] = _layer_norm(ALPHA * x_ref[...] + y, g_ref[...], b_ref[...])


def _outproj_ln(a, w, x, g, b, tm=256, name="outproj_ln"):
    m, k = a.shape
    d = w.shape[1]
    assert m % tm == 0
    return pl.pallas_call(
        _outproj_ln_kernel, grid=(m // tm,),
        in_specs=[pl.BlockSpec((tm, k), lambda i: (i, 0)), _const_spec(w.shape),
                  pl.BlockSpec((tm, d), lambda i: (i, 0)), _const_spec((1, d)), _const_spec((1, d))],
        out_specs=pl.BlockSpec((tm, d), lambda i: (i, 0)),
        out_shape=jax.ShapeDtypeStruct((m, d), f32),
        compiler_params=_cparams(("parallel",)), name=name,
    )(a, w, x, g.reshape(1, d), b.reshape(1, d))


def _ffn_ln_kernel(x_ref, wg_ref, wu_ref, wo_ref, g_ref, b_ref, o_ref, h_ref):
    j = pl.program_id(1)
    tf = wg_ref.shape[1]
    xb = x_ref[...].astype(bf16)
    hg = jnp.dot(xb, wg_ref[...].astype(bf16), preferred_element_type=f32)
    hu = jnp.dot(xb, wu_ref[...].astype(bf16), preferred_element_type=f32)
    h_ref[:, pl.ds(pl.multiple_of(j * tf, tf), tf)] = (hg * jax.nn.sigmoid(hg) * hu).astype(bf16)

    @pl.when(j == pl.num_programs(1) - 1)
    def _():
        y = jnp.dot(h_ref[...], wo_ref[...].astype(bf16), preferred_element_type=f32)
        o_ref[...] = _layer_norm(ALPHA * x_ref[...] + y, g_ref[...], b_ref[...])


def _ffn_ln(x, w_in, w_out, layer, g, b, tm, tf, name="ffn_ln"):
    m, d = x.shape
    f = w_out.shape[1]
    assert m % tm == 0 and f % tf == 0 and tf % 128 == 0
    nf = f // tf
    return pl.pallas_call(
        _ffn_ln_kernel, grid=(m // tm, nf),
        in_specs=[pl.BlockSpec((tm, d), lambda i, j: (i, 0)),
                  pl.BlockSpec((None, d, tf), lambda i, j: (layer, 0, j)),
                  pl.BlockSpec((None, d, tf), lambda i, j: (layer, 0, j + nf)),
                  pl.BlockSpec((None, f, d), lambda i, j: (layer, 0, 0), pipeline_mode=pl.Buffered(1)),
                  _const_spec((1, d)), _const_spec((1, d))],
        out_specs=pl.BlockSpec((tm, d), lambda i, j: (i, 0)),
        out_shape=jax.ShapeDtypeStruct((m, d), f32),
        scratch_shapes=[pltpu.VMEM((tm, f), bf16)],
        compiler_params=_cparams(("parallel", "arbitrary")), name=name,
    )(x, w_in, w_in, w_out, g.reshape(1, d), b.reshape(1, d))


def _router_kernel(x_ref, w_ref, b_ref, c_ref):
    logits = jnp.dot(x_ref[...], w_ref[...], precision=lax.Precision.HIGHEST,
                     preferred_element_type=f32) + b_ref[...]
    e_idx, i1, i2, w1, w2 = _top2(logits)
    c_ref[...] = jnp.where(e_idx == i1, w1, 0.0) + jnp.where(e_idx == i2, w2, 0.0)


def _router(x, w, b, tm=512):
    m, d = x.shape
    assert m % tm == 0
    return pl.pallas_call(
        _router_kernel, grid=(m // tm,),
        in_specs=[pl.BlockSpec((tm, d), lambda i: (i, 0)), _const_spec(w.shape),
                  _const_spec((1, N_EXPERTS))],
        out_specs=pl.BlockSpec((tm, N_EXPERTS), lambda i: (i, 0)),
        out_shape=jax.ShapeDtypeStruct((m, N_EXPERTS), f32),
        compiler_params=_cparams(("parallel",)), name="router",
    )(x, w, b.reshape(1, N_EXPERTS))


def _moe_ln_kernel(x_ref, c_ref, wg_ref, wu_ref, wo_ref, g_ref, b_ref, o_ref, acc_ref):
    e = pl.program_id(1)
    j = pl.program_id(2)
    x = x_ref[...]
    xb = x.astype(bf16)
    c = c_ref[...]
    ce = jnp.sum(jnp.where(lax.broadcasted_iota(jnp.int32, c.shape, 1) == e, c, 0.0), -1, keepdims=True)
    hg = jnp.dot(xb, wg_ref[...].astype(bf16), preferred_element_type=f32)
    hu = jnp.dot(xb, wu_ref[...].astype(bf16), preferred_element_type=f32)
    act = (hg * jax.nn.sigmoid(hg) * hu).astype(bf16)
    part = ce * jnp.dot(act, wo_ref[...].astype(bf16), preferred_element_type=f32)
    first = jnp.logical_and(e == 0, j == 0)

    @pl.when(first)
    def _():
        acc_ref[...] = part

    @pl.when(jnp.logical_not(first))
    def _():
        acc_ref[...] += part

    @pl.when(jnp.logical_and(e == pl.num_programs(1) - 1, j == pl.num_programs(2) - 1))
    def _():
        o_ref[...] = _layer_norm(ALPHA * x + acc_ref[...], g_ref[...], b_ref[...])


def _moe_ln(x, comb, w_in, w_out, layer, g, b, tm, tf):
    m, d = x.shape
    _, ne, f, _ = w_out.shape
    assert m % tm == 0 and f % tf == 0
    nf = f // tf
    return pl.pallas_call(
        _moe_ln_kernel, grid=(m // tm, ne, nf),
        in_specs=[pl.BlockSpec((tm, d), lambda i, e, j: (i, 0)),
                  pl.BlockSpec((tm, ne), lambda i, e, j: (i, 0)),
                  pl.BlockSpec((None, None, d, tf), lambda i, e, j: (layer, e, 0, j)),
                  pl.BlockSpec((None, None, d, tf), lambda i, e, j: (layer, e, 0, j + nf)),
                  pl.BlockSpec((None, None, tf, d), lambda i, e, j: (layer, e, j, 0)),
                  _const_spec((1, d)), _const_spec((1, d))],
        out_specs=pl.BlockSpec((tm, d), lambda i, e, j: (i, 0)),
        out_shape=jax.ShapeDtypeStruct((m, d), f32),
        scratch_shapes=[pltpu.VMEM((tm, d), f32)],
        compiler_params=_cparams(("parallel", "arbitrary", "arbitrary")), name="moe_ln",
    )(x, comb, w_in, w_in, w_out, g.reshape(1, d), b.reshape(1, d))


MOE_TM = 1024
MOE_ROUTE_TM = 512
MOE_DISPATCH_TM = 512
MOE_COMBINE_TM = 256
ROUTE_COLS = 8
ROW_DMA_UNROLL = 8


def _top2(logits):
    e_idx = lax.broadcasted_iota(jnp.int32, logits.shape, 1)
    v1 = jnp.max(logits, -1, keepdims=True)
    i1 = jnp.min(jnp.where(logits == v1, e_idx, N_EXPERTS), -1, keepdims=True)
    rest = jnp.where(e_idx == i1, -jnp.inf, logits)
    v2 = jnp.max(rest, -1, keepdims=True)
    i2 = jnp.min(jnp.where(rest == v2, e_idx, N_EXPERTS), -1, keepdims=True)
    e2 = jnp.exp(v2 - v1)
    den = 1.0 + e2
    return e_idx, i1, i2, 1.0 / den, e2 / den


def _route_kernel(x_ref, w_ref, b_ref, r_ref, cnt_ref, carry_ref):
    i = pl.program_id(0)
    tm = x_ref.shape[0]

    @pl.when(i == 0)
    def _():
        carry_ref[...] = jnp.zeros_like(carry_ref)

    logits = jnp.dot(x_ref[...], w_ref[...], precision=lax.Precision.HIGHEST,
                     preferred_element_type=f32) + b_ref[...]
    e_idx, i1, i2, w1, w2 = _top2(logits)
    hit1 = e_idx == i1
    hit2 = e_idx == i2
    a = jnp.where(hit1, 1.0, 0.0) + jnp.where(hit2, 1.0, 0.0)
    row = lax.broadcasted_iota(jnp.int32, (tm, tm), 0)
    col = lax.broadcasted_iota(jnp.int32, (tm, tm), 1)
    before = jnp.where(col < row, 1.0, 0.0).astype(bf16)
    c = carry_ref[...] + jnp.dot(before, a.astype(bf16), preferred_element_type=f32)
    pos1 = jnp.sum(jnp.where(hit1, c, 0.0), -1, keepdims=True)
    pos2 = jnp.sum(jnp.where(hit2, c, 0.0), -1, keepdims=True)
    carry_ref[...] += jnp.sum(a, 0, keepdims=True)
    cols = (i1.astype(f32), i2.astype(f32), pos1, pos2, w1, w2)
    rec = jnp.zeros((tm, ROUTE_COLS), f32)
    for n, v in enumerate(cols):
        rec = jnp.where(e_idx == n, v, rec)
    r_ref[...] = rec

    @pl.when(i == pl.num_programs(0) - 1)
    def _():
        cnt_ref[...] = carry_ref[...]


def _route(x, w, b):
    m, d = x.shape
    tm = MOE_ROUTE_TM
    assert m % tm == 0 and ROUTE_COLS == N_EXPERTS
    return pl.pallas_call(
        _route_kernel, grid=(m // tm,),
        in_specs=[pl.BlockSpec((tm, d), lambda i: (i, 0)), _const_spec(w.shape), _const_spec((1, N_EXPERTS))],
        out_specs=[pl.BlockSpec((tm, ROUTE_COLS), lambda i: (i, 0)), pl.BlockSpec((1, N_EXPERTS), lambda i: (0, 0))],
        out_shape=[jax.ShapeDtypeStruct((m, ROUTE_COLS), f32), jax.ShapeDtypeStruct((1, N_EXPERTS), f32)],
        scratch_shapes=[pltpu.VMEM((1, N_EXPERTS), f32)],
        compiler_params=_cparams(("arbitrary",)), name="moe_route",
    )(x, w, b.reshape(1, N_EXPERTS))


def _row_copy(src, src_row, dst, dst_row, sem):
    return pltpu.make_async_copy(src.at[pl.ds(src_row, 1)], dst.at[pl.ds(dst_row, 1)], sem)


def _dispatch_kernel(d1_ref, d2_ref, x_ref, init_hbm, o_hbm, sem):
    del init_hbm
    base = pl.program_id(0) * MOE_DISPATCH_TM

    def copies(t):
        tok = base + t
        return (_row_copy(x_ref, t, o_hbm, d1_ref[tok], sem), _row_copy(x_ref, t, o_hbm, d2_ref[tok], sem))

    def issue(t, carry):
        for cp in copies(t):
            cp.start()
        return carry

    def drain(t, carry):
        for cp in copies(t):
            cp.wait()
        return carry

    lax.fori_loop(0, MOE_DISPATCH_TM, issue, 0, unroll=ROW_DMA_UNROLL)
    lax.fori_loop(0, MOE_DISPATCH_TM, drain, 0, unroll=ROW_DMA_UNROLL)


def _dispatch(x, d1, d2, n_rows):
    m, d = x.shape
    assert m % MOE_DISPATCH_TM == 0
    any_spec = pl.BlockSpec(memory_space=pl.ANY)
    grid_spec = pltpu.PrefetchScalarGridSpec(
        num_scalar_prefetch=2, grid=(m // MOE_DISPATCH_TM,),
        in_specs=[pl.BlockSpec((MOE_DISPATCH_TM, d), lambda i, *_: (i, 0)), any_spec], out_specs=any_spec,
        scratch_shapes=[pltpu.SemaphoreType.DMA(())])
    return pl.pallas_call(
        _dispatch_kernel, grid_spec=grid_spec, out_shape=jax.ShapeDtypeStruct((n_rows, d), x.dtype),
        input_output_aliases={3: 0},
        compiler_params=_cparams(("arbitrary",)), name="moe_dispatch",
    )(d1, d2, x, jnp.zeros((n_rows, d), x.dtype))


def _moe_group_kernel(te_ref, nu_ref, x_ref, wg_ref, wu_ref, wo_ref, o_ref, acc_ref):
    del te_ref
    t = pl.program_id(0)
    j = pl.program_id(1)
    used = t < nu_ref[0]
    last = j == pl.num_programs(1) - 1

    @pl.when(used)
    def _():
        xb = x_ref[...].astype(bf16)
        hg = jnp.dot(xb, wg_ref[...].astype(bf16), preferred_element_type=f32)
        hu = jnp.dot(xb, wu_ref[...].astype(bf16), preferred_element_type=f32)
        act = (hg * jax.nn.sigmoid(hg) * hu).astype(bf16)
        part = jnp.dot(act, wo_ref[...].astype(bf16), preferred_element_type=f32)

        @pl.when(j == 0)
        def _():
            acc_ref[...] = part

        @pl.when(j > 0)
        def _():
            acc_ref[...] += part

        @pl.when(last)
        def _():
            o_ref[...] = acc_ref[...]

    @pl.when(jnp.logical_and(jnp.logical_not(used), last))
    def _():
        o_ref[...] = jnp.zeros_like(o_ref)


def _moe_group(xs, tile_expert, n_used, w_in, w_out, layer, tf):
    r, d = xs.shape
    _, ne, f, _ = w_out.shape
    assert r % MOE_TM == 0 and f % tf == 0
    nf = f // tf
    n_tiles = r // MOE_TM

    def jj(t, j, nu):
        return jnp.where(t < nu[0], j, nf - 1)

    grid_spec = pltpu.PrefetchScalarGridSpec(
        num_scalar_prefetch=2, grid=(n_tiles, nf),
        in_specs=[pl.BlockSpec((MOE_TM, d), lambda t, j, te, nu: (jnp.maximum(jnp.minimum(t, nu[0] - 1), 0), 0)),
                  pl.BlockSpec((None, None, d, tf), lambda t, j, te, nu: (layer, te[t], 0, jj(t, j, nu))),
                  pl.BlockSpec((None, None, d, tf), lambda t, j, te, nu: (layer, te[t], 0, jj(t, j, nu) + nf)),
                  pl.BlockSpec((None, None, tf, d), lambda t, j, te, nu: (layer, te[t], jj(t, j, nu), 0))],
        out_specs=pl.BlockSpec((MOE_TM, d), lambda t, j, te, nu: (t, 0)),
        scratch_shapes=[pltpu.VMEM((MOE_TM, d), f32)])
    return pl.pallas_call(
        _moe_group_kernel, grid_spec=grid_spec, out_shape=jax.ShapeDtypeStruct((r, d), f32),
        compiler_params=_cparams(("arbitrary", "arbitrary")), name="moe_group",
    )(tile_expert, n_used, xs, w_in, w_in, w_out)


def _moe_combine_kernel(d1_ref, d2_ref, x_ref, r_ref, y_hbm, g_ref, b_ref, o_ref, ya_ref, yb_ref, sem):
    tm = x_ref.shape[0]
    base = pl.program_id(0) * tm

    def copies(t):
        tok = base + t
        return (_row_copy(y_hbm, d1_ref[tok], ya_ref, t, sem), _row_copy(y_hbm, d2_ref[tok], yb_ref, t, sem))

    def issue(t, carry):
        for cp in copies(t):
            cp.start()
        return carry

    def drain(t, carry):
        for cp in copies(t):
            cp.wait()
        return carry

    lax.fori_loop(0, tm, issue, 0, unroll=ROW_DMA_UNROLL)
    lax.fori_loop(0, tm, drain, 0, unroll=ROW_DMA_UNROLL)
    r = r_ref[...]
    mix = r[:, 4:5] * ya_ref[...] + r[:, 5:6] * yb_ref[...]
    o_ref[...] = _layer_norm(ALPHA * x_ref[...] + mix, g_ref[...], b_ref[...])


def _moe_combine(x, route, y, d1, d2, g, b):
    m, d = x.shape
    tm = MOE_COMBINE_TM
    assert m % tm == 0
    grid_spec = pltpu.PrefetchScalarGridSpec(
        num_scalar_prefetch=2, grid=(m // tm,),
        in_specs=[pl.BlockSpec((tm, d), lambda i, *_: (i, 0)),
                  pl.BlockSpec((tm, ROUTE_COLS), lambda i, *_: (i, 0)),
                  pl.BlockSpec(memory_space=pl.ANY),
                  pl.BlockSpec((1, d), lambda i, *_: (0, 0)), pl.BlockSpec((1, d), lambda i, *_: (0, 0))],
        out_specs=pl.BlockSpec((tm, d), lambda i, *_: (i, 0)),
        scratch_shapes=[pltpu.VMEM((tm, d), f32), pltpu.VMEM((tm, d), f32), pltpu.SemaphoreType.DMA(())])
    return pl.pallas_call(
        _moe_combine_kernel, grid_spec=grid_spec, out_shape=jax.ShapeDtypeStruct((m, d), f32),
        compiler_params=_cparams(("arbitrary",)), name="moe_combine",
    )(d1, d2, x, route, y, g.reshape(1, d), b.reshape(1, d))


def _moe_sparse_ln(x, router_w, router_b, w_in, w_out, layer, g, b, tf):
    m, d = x.shape
    route, counts = _route(x, router_w, router_b)
    counts = counts[0].astype(jnp.int32)
    padded = (counts + MOE_TM - 1) // MOE_TM * MOE_TM
    ends = jnp.cumsum(padded)
    offs = ends - padded
    n_tiles = -(-TOP_K * m // MOE_TM) + N_EXPERTS
    n_used = (ends[-1] // MOE_TM).reshape(1)
    tile_start = jnp.minimum(jnp.arange(n_tiles, dtype=jnp.int32), n_used[0] - 1) * MOE_TM
    tile_expert = jnp.minimum(jnp.sum((tile_start[:, None] >= ends[None, :]).astype(jnp.int32), axis=1), N_EXPERTS - 1)
    e1 = route[:, 0].astype(jnp.int32)
    e2 = route[:, 1].astype(jnp.int32)
    d1 = offs[e1] + route[:, 2].astype(jnp.int32)
    d2 = offs[e2] + route[:, 3].astype(jnp.int32)
    xs = _dispatch(x, d1, d2, n_tiles * MOE_TM)
    y = _moe_group(xs, tile_expert, n_used, w_in, w_out, layer, tf)
    return _moe_combine(x, route, y, d1, d2, g, b)


def _softplus(x):
    return jnp.maximum(x, 0.0) + jnp.log(1.0 + jnp.exp(-jnp.abs(x)))


def _lru_gates(xc, wa, ba, wx, bx, sp):
    xcb = xc.astype(bf16)
    r = jax.nn.sigmoid(jnp.dot(xcb, wa, preferred_element_type=f32) + ba)
    gi = jax.nn.sigmoid(jnp.dot(xcb, wx, preferred_element_type=f32) + bx)
    log_a = sp * r
    th = jnp.tanh(log_a)
    one_minus_a2 = -2.0 * th / (1.0 - th)
    return jnp.exp(log_a), jnp.sqrt(one_minus_a2) * gi * xc


def _lru_core_kernel(xb_ref, g_ref, cw_ref, cb_ref, wa_ref, ba_ref, wx_ref, bx_ref, lam_ref,
                     y_ref, hlast_ref, ext_ref, a_ref, b_ref, h_ref, *, ts, nb):
    i = pl.program_id(0)
    hist = 8

    @pl.when(i == 0)
    def _():
        ext_ref[:, 0:hist, :] = jnp.zeros((nb, hist, ext_ref.shape[2]), f32)
        h_ref[...] = jnp.zeros_like(h_ref)

    ext_ref[:, hist:hist + ts, :] = xb_ref[...]
    sp = -LRU_C * _softplus(-lam_ref[...])
    cw = cw_ref[...]
    for b in range(nb):
        xc = cb_ref[...]
        for k in range(CONV_W):
            off = hist - (CONV_W - 1) + k
            xc = xc + cw[k:k + 1, :] * ext_ref[b, off:off + ts, :]
        a, bt = _lru_gates(xc, wa_ref[...], ba_ref[...], wx_ref[...], bx_ref[...], sp)
        a_ref[b] = a
        b_ref[b] = bt
    ext_ref[:, 0:hist, :] = ext_ref[:, ts:ts + hist, :]

    def step(t, h):
        h = a_ref[:, t, :] * h + b_ref[:, t, :]
        b_ref[:, t, :] = h
        return h

    h = lax.fori_loop(0, ts, step, h_ref[...], unroll=8)
    h_ref[...] = h
    y_ref[...] = b_ref[...] * g_ref[...]

    @pl.when(i == pl.num_programs(0) - 1)
    def _():
        hlast_ref[...] = h


def _lru_core(xb, g, cw, cb, wa, ba, wx, bx, lam, ts=64):
    nb, s, c = xb.shape
    assert s % ts == 0 and ts % 8 == 0
    blk = pl.BlockSpec((nb, ts, c), lambda i: (0, i, 0))
    row = _const_spec((1, c))
    return pl.pallas_call(
        functools.partial(_lru_core_kernel, ts=ts, nb=nb), grid=(s // ts,),
        in_specs=[blk, blk, _const_spec((CONV_W, c)), row, _const_spec((c, c)), row,
                  _const_spec((c, c)), row, row],
        out_specs=[blk, pl.BlockSpec((nb, c), lambda i: (0, 0))],
        out_shape=[jax.ShapeDtypeStruct((nb, s, c), f32), jax.ShapeDtypeStruct((nb, c), f32)],
        scratch_shapes=[pltpu.VMEM((nb, ts + 8, c), f32), pltpu.VMEM((nb, ts, c), f32),
                        pltpu.VMEM((nb, ts, c), f32), pltpu.VMEM((nb, c), f32)],
        compiler_params=_cparams(("arbitrary",)), name="lru_core",
    )(xb, g, cw, cb.reshape(1, c), wa, ba.reshape(1, c), wx, bx.reshape(1, c), lam.reshape(1, c))


def _lru_short_kernel(xb_ref, g_ref, cs_ref, h0_ref, cw_ref, cb_ref, wa_ref, ba_ref, wx_ref, bx_ref,
                      lam_ref, y_ref, hlast_ref, *, s):
    rows = [cs_ref[k] for k in range(CONV_W - 1)] + [xb_ref[t] for t in range(s)]
    sp = -LRU_C * _softplus(-lam_ref[...])
    cw = cw_ref[...]
    h = h0_ref[...]
    for t in range(s):
        xc = cb_ref[...]
        for k in range(CONV_W):
            xc = xc + cw[k:k + 1, :] * rows[t + k]
        a, bt = _lru_gates(xc, wa_ref[...], ba_ref[...], wx_ref[...], bx_ref[...], sp)
        h = a * h + bt
        y_ref[t] = h * g_ref[t]
    hlast_ref[...] = h


def _lru_short(xb, g, cs, h0, cw, cb, wa, ba, wx, bx, lam):
    s, nb, c = xb.shape
    return pl.pallas_call(
        functools.partial(_lru_short_kernel, s=s),
        out_shape=[jax.ShapeDtypeStruct((s, nb, c), f32), jax.ShapeDtypeStruct((nb, c), f32)],
        compiler_params=pltpu.CompilerParams(vmem_limit_bytes=VMEM_LIMIT_V7X), name="lru_short",
    )(xb, g, cs, h0, cw, cb.reshape(1, c), wa, ba.reshape(1, c), wx, bx.reshape(1, c), lam.reshape(1, c))


def _block_diag(w):
    n, k, _ = w.shape
    eye = jnp.eye(n, dtype=w.dtype)
    return (eye[:, None, :, None] * w[:, :, None, :]).reshape(n * k, n * k)


def _gelu_ep(y):
    return jax.nn.gelu(y)


def _lru_layer(xp, xs, nbp, nbs, conv_s, h_s, ln_g, ln_b, w_in, cw, cb, w_a, b_a, w_x, b_x, lam, w_out):
    c = D_RNN
    w_gate = w_in[:, :c].astype(bf16)
    w_xb = w_in[:, c:].astype(bf16)
    wa = _block_diag(w_a).astype(bf16)
    wx = _block_diag(w_x).astype(bf16)
    wo = w_out.astype(bf16)
    sp_len = xp.shape[0] // nbp
    ss_len = xs.shape[0] // nbs

    gp, xbp = _proj(xp, [w_gate, w_xb], [_gelu_ep, None], tm=256, name="lru_in")
    gs, xbs = _proj(xs, [w_gate, w_xb], [_gelu_ep, None], tm=xs.shape[0], name="lru_in_s")
    xbp3 = xbp.reshape(nbp, sp_len, c)
    yp, hp = _lru_core(xbp3, gp.reshape(nbp, sp_len, c), cw, cb, wa, b_a, wx, b_x, lam)
    xbs3 = xbs.reshape(nbs, ss_len, c)
    ys_t, hs = _lru_short(xbs3.transpose(1, 0, 2), gs.reshape(nbs, ss_len, c).transpose(1, 0, 2),
                          conv_s.transpose(1, 0, 2), h_s, cw, cb, wa, b_a, wx, b_x, lam)
    ys = ys_t.transpose(1, 0, 2).reshape(nbs * ss_len, c)
    xp = _outproj_ln(yp.reshape(nbp * sp_len, c), wo, xp, ln_g, ln_b, name="lru_out")
    xs = _outproj_ln(ys, wo, xs, ln_g, ln_b, tm=xs.shape[0], name="lru_out_s")
    conv_p = xbp3[:, sp_len - (CONV_W - 1):]
    conv_sn = jnp.concatenate([conv_s, xbs3], axis=1)[:, ss_len:]
    return xp, xs, (conv_p, conv_sn, hp, hs)


SSD_GN = SSD_GROUPS * SSD_STATE
SSD_HPG = SSD_HEADS // SSD_GROUPS
SSD_DT_PAD = 128


def _ssd_core_kernel(*refs, q, valid, zero_init):
    if zero_init:
        (xbc_ref, dt_ref, zs_ref, cw_ref, cb_ref, alog_ref, dsk_ref, ng_ref, ex_ref,
         y_ref, sfin_ref, ext_ref, st_ref, yacc_ref) = refs
    else:
        (xbc_ref, dt_ref, zs_ref, hist_ref, s0_ref, cw_ref, cb_ref, alog_ref, dsk_ref, ng_ref, ex_ref,
         y_ref, sfin_ref, ext_ref, st_ref, yacc_ref) = refs
    ci = pl.program_id(1)
    hist = 8
    hp = SSD_HPG * SSD_HEAD_DIM
    pairs = [(g, j) for g in range(SSD_GROUPS) for j in range(SSD_HPG // 2)]

    @pl.when(ci == 0)
    def _():
        if zero_init:
            ext_ref[0:hist, :] = jnp.zeros((hist, ext_ref.shape[1]), f32)
            st_ref[...] = jnp.zeros_like(st_ref)
        else:
            ext_ref[0:hist, :] = hist_ref[...]
            for g, j in pairs:
                h = g * SSD_HPG + 2 * j
                two = jnp.concatenate([s0_ref[h], s0_ref[h + 1]], axis=0)
                st_ref[g, :, 2 * SSD_HEAD_DIM * j:2 * SSD_HEAD_DIM * (j + 1)] = two.T

    ext_ref[hist:hist + q, :] = xbc_ref[...]
    cw = cw_ref[...]
    xc = cb_ref[...]
    for k in range(CONV_W):
        off = hist - (CONV_W - 1) + k
        xc = xc + cw[k:k + 1, :] * ext_ref[off:off + q, :]
    xc = xc * jax.nn.sigmoid(xc)
    if q >= hist:
        ext_ref[0:hist, :] = ext_ref[q:q + hist, :]

    row = lax.broadcasted_iota(jnp.int32, (q, q), 0)
    col = lax.broadcasted_iota(jnp.int32, (q, q), 1)
    causal = col <= row
    dt = dt_ref[...]
    if valid < q:
        dt = jnp.where(lax.broadcasted_iota(jnp.int32, dt.shape, 0) < valid, dt, 0.0)
    a_neg = -jnp.exp(alog_ref[...])
    tril = jnp.where(causal, 1.0, 0.0)
    acum = jnp.dot(tril, dt * a_neg, precision=lax.Precision.HIGHEST, preferred_element_type=f32)
    acum_t = acum.T
    last = acum[q - 1:q, :]

    def per_channel(v):
        hi = v.astype(bf16)
        r1 = v - hi.astype(f32)
        mid = r1.astype(bf16)
        lo = (r1 - mid.astype(f32)).astype(bf16)
        ex = ex_ref[...]
        return (jnp.dot(hi, ex, preferred_element_type=f32) + jnp.dot(mid, ex, preferred_element_type=f32)
                + jnp.dot(lo, ex, preferred_element_type=f32))

    e_acum_x = per_channel(jnp.exp(acum))
    xdt_all = xc[:, :SSD_INNER] * per_channel(dt)
    xdt_b = xdt_all.astype(bf16)
    xdd_b = (xdt_all * per_channel(jnp.exp(last - acum))).astype(bf16)
    dec_chunk_x = e_acum_x[q - 1:q, :]
    two = 2 * SSD_HEAD_DIM
    first_head = lax.broadcasted_iota(jnp.int32, (q, two), 1) < SSD_HEAD_DIM
    nt = (((1,), (1,)), ((), ()))

    for g in range(SSD_GROUPS):
        bm = xc[:, SSD_INNER + g * SSD_STATE:SSD_INNER + (g + 1) * SSD_STATE]
        cm = xc[:, SSD_INNER + SSD_GN + g * SSD_STATE:SSD_INNER + SSD_GN + (g + 1) * SSD_STATE].astype(bf16)
        cb = lax.dot_general(cm, bm.astype(bf16), nt, preferred_element_type=f32)
        ch = slice(g * hp, (g + 1) * hp)
        st = st_ref[g]
        y_off = jnp.dot(cm, st.astype(bf16), preferred_element_type=f32) * e_acum_x[:, ch]
        st_ref[g] = dec_chunk_x[:, ch] * st + jnp.dot(bm.T.astype(bf16), xdd_b[:, ch], preferred_element_type=f32)
        for j in range(SSD_HPG // 2):
            h = g * SSD_HPG + 2 * j
            lanes = slice(h * SSD_HEAD_DIM, (h + 2) * SSD_HEAD_DIM)
            m = []
            for hh in (h, h + 1):
                seg = acum[:, hh:hh + 1] - acum_t[hh:hh + 1, :]
                m.append((cb * jnp.exp(jnp.where(causal, seg, NEG_INF))).astype(bf16))
            if q % 128 == 0:
                x2 = xdt_b[:, lanes]
                zero = jnp.zeros_like(x2)
                rhs = jnp.concatenate([jnp.where(first_head, x2, zero), jnp.where(first_head, zero, x2)], axis=0)
                y_diag = jnp.dot(jnp.concatenate(m, axis=1), rhs, preferred_element_type=f32)
            else:
                y_diag = jnp.concatenate(
                    [jnp.dot(m[n], xdt_b[:, (h + n) * SSD_HEAD_DIM:(h + n + 1) * SSD_HEAD_DIM],
                             preferred_element_type=f32) for n in range(2)], axis=1)
            yacc_ref[:, lanes] = y_diag + y_off[:, 2 * j * SSD_HEAD_DIM:(2 * j + 2) * SSD_HEAD_DIM]

    y = yacc_ref[...] + dsk_ref[...] * xc[:, :SSD_INNER]
    yg = y * zs_ref[...]
    gw = SSD_INNER // SSD_GROUPS
    for g in range(SSD_GROUPS):
        v = yg[:, g * gw:(g + 1) * gw]
        v = v * lax.rsqrt(jnp.mean(v * v, -1, keepdims=True) + RMS_EPS)
        y_ref[:, g * gw:(g + 1) * gw] = v * ng_ref[:, g * gw:(g + 1) * gw]

    @pl.when(ci == pl.num_programs(1) - 1)
    def _():
        for g, j in pairs:
            h = g * SSD_HPG + 2 * j
            both = st_ref[g, :, two * j:two * (j + 1)].T
            sfin_ref[h] = both[:SSD_HEAD_DIM]
            sfin_ref[h + 1] = both[SSD_HEAD_DIM:]


def _ssd_core(xbc, dt, zs, hist, s0, cw, cb, a_log, d_skip, norm_g, q, valid):
    nb, s, cd = xbc.shape
    assert s % q == 0
    zero_init = s0 is None
    blk = lambda w: pl.BlockSpec((None, q, w), lambda b, c: (b, c, 0))
    st_spec = pl.BlockSpec((None, SSD_HEADS, SSD_HEAD_DIM, SSD_STATE), lambda b, c: (b, 0, 0, 0))
    in_specs = [blk(cd), blk(SSD_DT_PAD), blk(SSD_INNER)]
    args = [xbc, dt, zs]
    if not zero_init:
        in_specs += [pl.BlockSpec((None, 8, cd), lambda b, c: (b, 0, 0)), st_spec]
        args += [hist, s0]
    in_specs += [_const_spec((CONV_W, cd)), _const_spec((1, cd)), _const_spec((1, SSD_DT_PAD)),
                 _const_spec((1, SSD_INNER)), _const_spec((1, SSD_INNER)), _const_spec((SSD_DT_PAD, SSD_INNER))]
    pad_h = SSD_DT_PAD - SSD_HEADS
    head_to_channels = (jnp.arange(SSD_DT_PAD, dtype=jnp.int32)[:, None]
                        == jnp.arange(SSD_INNER, dtype=jnp.int32)[None, :] // SSD_HEAD_DIM).astype(bf16)
    args += [cw, cb.reshape(1, cd), jnp.pad(a_log, (0, pad_h)).reshape(1, SSD_DT_PAD),
             jnp.repeat(d_skip, SSD_HEAD_DIM).reshape(1, SSD_INNER), norm_g.reshape(1, SSD_INNER),
             head_to_channels]
    return pl.pallas_call(
        functools.partial(_ssd_core_kernel, q=q, valid=valid, zero_init=zero_init),
        grid=(nb, s // q), in_specs=in_specs,
        out_specs=[blk(SSD_INNER), st_spec],
        out_shape=[jax.ShapeDtypeStruct((nb, s, SSD_INNER), f32),
                   jax.ShapeDtypeStruct((nb, SSD_HEADS, SSD_HEAD_DIM, SSD_STATE), f32)],
        scratch_shapes=[pltpu.VMEM((q + 8, cd), f32),
                        pltpu.VMEM((SSD_GROUPS, SSD_STATE, SSD_HPG * SSD_HEAD_DIM), f32),
                        pltpu.VMEM((q, SSD_INNER), f32)],
        compiler_params=_cparams(("parallel", "arbitrary")), name="ssd_core",
    )(*args)


def _silu_ep(y, *_):
    return y * jax.nn.sigmoid(y)


def _dt_ep(y, bias):
    return _softplus(y + bias)


def _pass_ep(y, *_):
    return y


def _ssd_layer(xp, xs, nbp, nbs, conv_s, st_s, ln_g, ln_b, w_in, cw, cb, dt_bias, a_log, d_skip, norm_g, w_out):
    cd = SSD_CONV_DIM
    pad_h = SSD_DT_PAD - SSD_HEADS
    w_z = w_in[:, :SSD_INNER].astype(bf16)
    w_xbc = w_in[:, SSD_INNER:SSD_INNER + cd].astype(bf16)
    w_dt = jnp.pad(w_in[:, SSD_INNER + cd:], ((0, 0), (0, pad_h))).astype(bf16)
    bias = jnp.pad(dt_bias, (0, pad_h)).reshape(1, SSD_DT_PAD)
    wo = w_out.astype(bf16)
    sp_len = xp.shape[0] // nbp
    ss_len = xs.shape[0] // nbs
    eps = [_silu_ep, _pass_ep, _dt_ep]

    zsp, xbcp, dtp = _proj(xp, [w_z, w_xbc, w_dt], eps, aux=(bias,), aux_period=(0,), tm=256, name="ssd_in")
    zss, xbcs, dts = _proj(xs, [w_z, w_xbc, w_dt], eps, aux=(bias,), aux_period=(0,), tm=xs.shape[0], name="ssd_in_s")
    xbcp3 = xbcp.reshape(nbp, sp_len, cd)
    yp, sp_fin = _ssd_core(xbcp3, dtp.reshape(nbp, sp_len, SSD_DT_PAD), zsp.reshape(nbp, sp_len, SSD_INNER),
                           None, None, cw, cb, a_log, d_skip, norm_g, q=math.gcd(sp_len, SSD_CHUNK), valid=SSD_CHUNK)
    qs = 8
    assert ss_len <= qs
    tpad = lambda a: jnp.pad(a, ((0, 0), (0, qs - ss_len), (0, 0)))
    xbcs3 = xbcs.reshape(nbs, ss_len, cd)
    hist = jnp.pad(conv_s, ((0, 0), (8 - (CONV_W - 1), 0), (0, 0)))
    ys, ss_fin = _ssd_core(tpad(xbcs3), tpad(dts.reshape(nbs, ss_len, SSD_DT_PAD)),
                           tpad(zss.reshape(nbs, ss_len, SSD_INNER)), hist, st_s,
                           cw, cb, a_log, d_skip, norm_g, q=qs, valid=ss_len)
    ys = ys[:, :ss_len].reshape(nbs * ss_len, SSD_INNER)
    xp = _outproj_ln(yp.reshape(nbp * sp_len, SSD_INNER), wo, xp, ln_g, ln_b, name="ssd_out")
    xs = _outproj_ln(ys, wo, xs, ln_g, ln_b, tm=xs.shape[0], name="ssd_out_s")
    conv_p = xbcp3[:, sp_len - (CONV_W - 1):]
    conv_sn = jnp.concatenate([conv_s, xbcs3], axis=1)[:, ss_len:]
    return xp, xs, (conv_p, conv_sn, sp_fin, ss_fin)


NSA_TILE = 128
NSA_SCALE = NSA_HD ** -0.5


def _rope_tables(pos):
    inv = ROPE_THETA ** (-jnp.arange(0, NSA_HD, 2, dtype=f32) / NSA_HD)
    ang = pos[:, None] * inv[None, :]
    cos, sin = jnp.cos(ang), jnp.sin(ang)
    return jnp.tile(cos, (1, 4)), jnp.tile(jnp.concatenate([-sin, sin], axis=1), (1, 2))


def _rope_cols(y, cos, sgn_sin, blocks):
    lane = lax.broadcasted_iota(jnp.int32, (y.shape[0], 128), 1)
    first_half = (lane % NSA_HD) < NSA_HD // 2
    out = []
    for c in range(y.shape[1] // 128):
        blk = y[:, 128 * c:128 * (c + 1)]
        if c in blocks:
            partner = jnp.where(first_half, pltpu.roll(blk, 128 - NSA_HD // 2, 1), pltpu.roll(blk, NSA_HD // 2, 1))
            blk = blk * cos + partner * sgn_sin
        out.append(blk)
    return jnp.concatenate(out, axis=1)


def _q_ep(y, cos, sin):
    return _rope_cols(y, cos, sin, range(NSA_HQ // 128)) * NSA_SCALE


def _rows_ep(y, cos, sin):
    return _rope_cols(y, cos, sin, (4, 5))


def _wrows_ep(y, cos, sin):
    return _rope_cols(y, cos, sin, (0, 1))


def _sigmoid_ep(y, *_):
    return jax.nn.sigmoid(y)


def _nsa_cmp_kernel(pt_ref, pa_ref, pb_ref, cos_ref, sin_ref, o_ref):
    del pt_ref
    per_page = NSA_TILE // CMP_BLOCK
    for n, ref in enumerate((pa_ref, pb_ref)):
        m = ref[...].reshape(per_page, CMP_BLOCK, ref.shape[1]).sum(1) * (1.0 / CMP_BLOCK)
        lo, hi = n * per_page, (n + 1) * per_page
        o_ref[lo:hi, :] = _rope_cols(m, cos_ref[lo:hi, :], sin_ref[lo:hi, :], (0, 1))


def _nsa_cmp(pool, page_table, cos, sin):
    nb, n_pages = page_table.shape
    assert n_pages % 2 == 0
    w = 2 * NSA_HK
    per_step = 2 * NSA_TILE // CMP_BLOCK
    grid_spec = pltpu.PrefetchScalarGridSpec(
        num_scalar_prefetch=1, grid=(nb, n_pages // 2),
        in_specs=[pl.BlockSpec((None, NSA_TILE, w), lambda b, p, pt: (pt[b, 2 * p], 0, 0)),
                  pl.BlockSpec((None, NSA_TILE, w), lambda b, p, pt: (pt[b, 2 * p + 1], 0, 0)),
                  pl.BlockSpec((per_step, 128), lambda b, p, pt: (p, 0)),
                  pl.BlockSpec((per_step, 128), lambda b, p, pt: (p, 0))],
        out_specs=pl.BlockSpec((None, per_step, w), lambda b, p, pt: (b, p, 0)))
    return pl.pallas_call(
        _nsa_cmp_kernel, grid_spec=grid_spec,
        out_shape=jax.ShapeDtypeStruct((nb, n_pages * NSA_TILE // CMP_BLOCK, w), f32),
        compiler_params=_cparams(("parallel", "arbitrary")), name="nsa_cmp",
    )(page_table, pool, pool, cos, sin)


def _even_odd(cmp):
    nb, nc, w = cmp.shape
    return cmp.reshape(nb, nc // 2, 2, w).transpose(0, 2, 1, 3).reshape(nb, nc, w)


def _topk_mask_rows(score_t, k):
    n = score_t.shape[0]
    assert n % 8 == 0
    pieces = [score_t[8 * v:8 * v + 8] for v in range(n // 8)]
    ridx = lax.broadcasted_iota(jnp.int32, pieces[0].shape, 0)
    cnts = [jnp.zeros(pieces[0].shape, f32) for _ in pieces]
    for i in range(n):
        si = score_t[i:i + 1, :]
        for v, pc in enumerate(pieces):
            if 8 * v > i:
                cnts[v] = cnts[v] + jnp.where(si >= pc, 1.0, 0.0)
            elif 8 * v + 7 < i:
                cnts[v] = cnts[v] + jnp.where(si > pc, 1.0, 0.0)
            else:
                tie_wins = jnp.where(ridx + 8 * v > i, 1.0, 0.0)
                cnts[v] = cnts[v] + jnp.where(si > pc, 1.0, 0.0) + jnp.where(si == pc, tie_wins, 0.0)
    return jnp.concatenate([jnp.where(c < k, 1.0, 0.0) for c in cnts], axis=0)


def _topk_mask_lanes(score, k, n_real):
    lane = lax.broadcasted_iota(jnp.int32, score.shape, 1)
    cnt = jnp.zeros(score.shape, f32)
    for i in range(n_real):
        si = score[:, i:i + 1]
        cnt = cnt + jnp.where(si > score, 1.0, 0.0) + jnp.where(si == score, jnp.where(lane > i, 1.0, 0.0), 0.0)
    return jnp.where(cnt < k, 1.0, 0.0)


def _softmax_rows(s):
    e = jnp.exp(s - jnp.max(s, -1, keepdims=True))
    return e / jnp.sum(e, -1, keepdims=True)


def _flash_step(qk, kt, vt, ok, carry):
    m, l, acc = carry
    s = lax.dot_general(qk, kt, (((1,), (1,)), ((), ())), preferred_element_type=f32)
    s = jnp.where(ok, s, NEG_INF)
    m_new = jnp.maximum(m, jnp.max(s, -1, keepdims=True))
    alpha = jnp.exp(m - m_new)
    p = jnp.exp(s - m_new)
    l = alpha * l + jnp.sum(p, -1, keepdims=True)
    acc = alpha * acc + jnp.dot(p.astype(bf16), vt, preferred_element_type=f32)
    return m_new, l, acc


def _flash_init(rows):
    return (jnp.full((rows, 1), NEG_INF, f32), jnp.zeros((rows, 1), f32), jnp.zeros((rows, NSA_HD), f32))


def _rope_rows(y, cos_t, sin_t, heads):
    half = NSA_HD // 2
    out = []
    for h in range(heads):
        top = y[NSA_HD * h:NSA_HD * h + half]
        bot = y[NSA_HD * h + half:NSA_HD * (h + 1)]
        out += [top * cos_t - bot * sin_t, bot * cos_t + top * sin_t]
    return jnp.concatenate(out, axis=0)


def _nsa_in_kernel(x_ref, wn_ref, wq_ref, wr_ref, ww_ref, wg_ref, cos_ref, sin_ref, cost_ref, sint_ref,
                   nat_ref, qt_ref, rt_ref, wt_ref, gt_ref):
    x = x_ref[...].astype(bf16)
    nat = jnp.dot(x, wn_ref[...], preferred_element_type=f32)
    nat_ref[...] = _rope_cols(nat, cos_ref[...], sin_ref[...], (4, 5, 6, 7))
    nt = (((1,), (1,)), ((), ()))
    cos_t, sin_t = cost_ref[...], sint_ref[...]
    q_t = lax.dot_general(wq_ref[...], x, nt, preferred_element_type=f32)
    qt_ref[...] = _rope_rows(q_t, cos_t, sin_t, NSA_HEADS) * NSA_SCALE
    r_t = lax.dot_general(wr_ref[...], x, nt, preferred_element_type=f32)
    rt_ref[0:2 * NSA_HK, :] = r_t[0:2 * NSA_HK]
    rt_ref[2 * NSA_HK:3 * NSA_HK, :] = _rope_rows(r_t[2 * NSA_HK:3 * NSA_HK], cos_t, sin_t, NSA_KV_HEADS)
    rt_ref[3 * NSA_HK:4 * NSA_HK, :] = r_t[3 * NSA_HK:4 * NSA_HK]
    w_t = lax.dot_general(ww_ref[...], x, nt, preferred_element_type=f32)
    wt_ref[0:NSA_HK, :] = _rope_rows(w_t[0:NSA_HK], cos_t, sin_t, NSA_KV_HEADS)
    wt_ref[NSA_HK:2 * NSA_HK, :] = w_t[NSA_HK:2 * NSA_HK]
    gt_ref[...] = jax.nn.sigmoid(lax.dot_general(wg_ref[...], x, nt, preferred_element_type=f32))


def _nsa_in(x, nb, w_in, tm=256):
    m, d = x.shape
    s_len = m // nb
    assert s_len % tm == 0
    hq, hk = NSA_HQ, NSA_HK
    w_q = w_in[:, :hq]
    w_kv = w_in[:, hq:hq + 6 * hk].reshape(d, 6, hk)
    w_g = w_in[:, hq + 6 * hk:]
    w_nat = w_kv[:, jnp.array([0, 1, 2, 4])].reshape(d, 4 * hk).astype(bf16)
    w_q_t = w_q.T.astype(bf16)
    w_rows_t = w_kv[:, 0:4].reshape(d, 4 * hk).T.astype(bf16)
    w_win_t = w_kv[:, 4:6].reshape(d, 2 * hk).T.astype(bf16)
    w_g_t = w_g.T.astype(bf16)
    pos = jnp.arange(s_len, dtype=f32)
    cos, sin = _rope_tables(pos)
    inv = ROPE_THETA ** (-jnp.arange(0, NSA_HD, 2, dtype=f32) / NSA_HD)
    ang_t = inv[:, None] * pos[None, :]
    nt = s_len // tm
    ng = 3 * NSA_HEADS
    t_spec = lambda r: pl.BlockSpec((None, r, tm), lambda b, i: (b, 0, i))
    tab = pl.BlockSpec((tm, 128), lambda b, i: (i, 0))
    tab_t = pl.BlockSpec((NSA_HD // 2, tm), lambda b, i: (0, i))
    return pl.pallas_call(
        _nsa_in_kernel, grid=(nb, nt),
        in_specs=[pl.BlockSpec((tm, d), lambda b, i: (b * nt + i, 0)),
                  _const_spec(w_nat.shape), _const_spec(w_q_t.shape), _const_spec(w_rows_t.shape),
                  _const_spec(w_win_t.shape), _const_spec(w_g_t.shape), tab, tab, tab_t, tab_t],
        out_specs=[pl.BlockSpec((tm, 4 * hk), lambda b, i: (b * nt + i, 0)), t_spec(hq), t_spec(4 * hk),
                   t_spec(2 * hk), t_spec(ng)],
        out_shape=[jax.ShapeDtypeStruct((m, 4 * hk), f32), jax.ShapeDtypeStruct((nb, hq, s_len), f32),
                   jax.ShapeDtypeStruct((nb, 4 * hk, s_len), f32), jax.ShapeDtypeStruct((nb, 2 * hk, s_len), f32),
                   jax.ShapeDtypeStruct((nb, ng, s_len), f32)],
        compiler_params=_cparams(("parallel", "parallel")), name="nsa_in",
    )(x, w_nat, w_q_t, w_rows_t, w_win_t, w_g_t, cos, sin, jnp.cos(ang_t), jnp.sin(ang_t))


def _flash_cols(k_tile, v_t, q_ref, bias, m_ref, l_ref, acc_ref):
    qw = NSA_TILE
    for g in range(q_ref.shape[1] // qw):
        c = slice(g * qw, (g + 1) * qw)
        s = jnp.dot(k_tile, q_ref[:, c], preferred_element_type=f32)
        if bias is not None:
            s = s + bias
        m_old = m_ref[:, c]
        m_new = jnp.maximum(m_old, jnp.max(s, 0, keepdims=True))
        alpha = jnp.exp(m_old - m_new)
        p = jnp.exp(s - m_new)
        l_ref[:, c] = alpha * l_ref[:, c] + jnp.sum(p, 0, keepdims=True)
        m_ref[:, c] = m_new
        acc_ref[:, c] = alpha * acc_ref[:, c] + jnp.dot(v_t, p.astype(bf16), preferred_element_type=f32)


def _nsa_seq_kernel(qt_ref, gt_ref, kcb_ref, vcbt_ref, knat_ref, vst_ref, vwt_ref, o_ref,
                    q_sc, oc_sc, os_sc, m_sc, l_sc, acc_sc, *, s_len):
    i = pl.program_id(1)
    qb = NSA_TILE
    nc = s_len // CMP_BLOCK
    nsb = s_len // SEL_BLOCK
    cols = NSA_GROUP * qb
    tile4 = lambda a: jnp.concatenate([a] * NSA_GROUP, axis=1)
    t_q = i * qb + lax.broadcasted_iota(jnp.int32, (1, qb), 1)
    t_cols = tile4(t_q)
    cl = lax.broadcasted_iota(jnp.int32, (nc, 1), 0)
    cblk = jnp.where(cl < nc // 2, 2 * cl, 2 * (cl - nc // 2) + 1)
    cmask = (cblk + 1) * CMP_BLOCK - 1 <= t_cols
    has_cmp = jnp.where(t_cols >= CMP_BLOCK - 1, 1.0, 0.0)
    blk = lax.broadcasted_iota(jnp.int32, (nsb, 1), 0)
    valid = blk * SEL_BLOCK <= t_q
    forced = (blk == 0) | (blk == t_q // SEL_BLOCK)
    key_in = lax.broadcasted_iota(jnp.int32, (NSA_TILE, 1), 0)
    blk_of_key = lax.broadcasted_iota(jnp.int32, (NSA_TILE, 2 * NSA_HD), 0) // SEL_BLOCK
    lane = lax.broadcasted_iota(jnp.int32, (NSA_TILE, 2 * NSA_HD), 1)
    own_lanes = [lane < NSA_HD, lane >= NSA_HD]
    gates = gt_ref[...]
    kcols = [slice(NSA_HD * k, NSA_HD * (k + 1)) for k in range(NSA_KV_HEADS)]
    wcols = [slice(NSA_HK + NSA_HD * k, NSA_HK + NSA_HD * (k + 1)) for k in range(NSA_KV_HEADS)]

    def reset():
        m_sc[...] = jnp.full(m_sc.shape, NEG_INF, f32)
        l_sc[...] = jnp.zeros_like(l_sc)
        acc_sc[...] = jnp.zeros_like(acc_sc)

    for k in range(NSA_KV_HEADS):
        kcol = kcols[k]
        q_t = jnp.concatenate([qt_ref[NSA_HD * (NSA_GROUP * k + g):NSA_HD * (NSA_GROUP * k + g + 1), :]
                               for g in range(NSA_GROUP)], axis=1).astype(bf16)

        s_c = jnp.dot(kcb_ref[:, kcol].astype(bf16), q_t, preferred_element_type=f32)
        s_c = jnp.where(cmask, s_c, NEG_INF)
        e = jnp.exp(s_c - jnp.max(s_c, 0, keepdims=True))
        p_c = e / jnp.sum(e, 0, keepdims=True) * has_cmp
        o_c = jnp.dot(vcbt_ref[kcol, :].astype(bf16), p_c.astype(bf16), preferred_element_type=f32)

        imp = p_c[:, 0:qb]
        for g in range(1, NSA_GROUP):
            imp = imp + p_c[:, g * qb:(g + 1) * qb]
        imp = imp[:nc // 2] + imp[nc // 2:]
        score = jnp.where(forced, FORCE_SCORE, jnp.where(valid, imp, -1.0))
        sel = _topk_mask_rows(score, SEL_TOPK)
        sel_bias = tile4(jnp.where(sel > 0.5, 0.0, NEG_INF)).astype(bf16)
        fill = jnp.zeros((NSA_HD - nsb, cols), bf16)
        q_sc[k] = jnp.concatenate([q_t, sel_bias, fill] if k % 2 == 0 else [sel_bias, fill, q_t], axis=0)
        oc_sc[k] = o_c

    def sel_tile(j, bias):
        off = pl.multiple_of(j * NSA_TILE, NSA_TILE)
        keys = pl.ds(off, NSA_TILE)
        blk_key = blk_of_key + j * (NSA_TILE // SEL_BLOCK)
        onehot = [jnp.where(lane - NSA_HD == blk_key, 1.0, 0.0).astype(bf16),
                  jnp.where(lane == blk_key, 1.0, 0.0).astype(bf16)]
        for pair in range(NSA_KV_HEADS // 2):
            k_both = knat_ref[keys, 2 * NSA_HD * pair:2 * NSA_HD * (pair + 1)].astype(bf16)
            for own in range(2):
                k = 2 * pair + own
                k_aug = jnp.where(own_lanes[own], k_both, onehot[own])
                _flash_cols(k_aug, vst_ref[kcols[k], keys].astype(bf16), q_sc.at[k], bias,
                            m_sc.at[k], l_sc.at[k], acc_sc.at[k])

    def win_tile(j, bias):
        off = pl.multiple_of(j * NSA_TILE, NSA_TILE)
        keys = pl.ds(off, NSA_TILE)
        for k in range(NSA_KV_HEADS):
            _flash_cols(knat_ref[keys, wcols[k]].astype(bf16), vwt_ref[kcols[k], keys].astype(bf16),
                        q_sc.at[k, pl.ds(NSA_HD * (k % 2), NSA_HD)], bias, m_sc.at[k], l_sc.at[k], acc_sc.at[k])

    def causal_bias(j):
        return jnp.where(j * NSA_TILE + key_in <= t_q, 0.0, NEG_INF)

    def window_bias(j):
        kpos = j * NSA_TILE + key_in
        return jnp.where((kpos <= t_q) & (kpos > t_q - WINDOW), 0.0, NEG_INF)

    def loop(tile_fn, lo, hi):
        def body(j, carry):
            tile_fn(j, None)
            return carry
        lax.fori_loop(lo, hi, body, 0)

    reset()
    loop(sel_tile, 0, i)
    sel_tile(i, causal_bias(i))
    os_sc[...] = acc_sc[...] / l_sc[...]

    reset()
    n_back = WINDOW // NSA_TILE

    @pl.when(i >= n_back)
    def _():
        win_tile(i - n_back, window_bias(i - n_back))

    loop(win_tile, jnp.maximum(i - n_back + 1, 0), i)
    win_tile(i, window_bias(i))

    for k in range(NSA_KV_HEADS):
        o_c = oc_sc[k]
        o_s = os_sc[k]
        o_w = acc_sc[k] / l_sc[k]
        for pair in range(NSA_GROUP // 2):
            o_t = []
            for g in (2 * pair, 2 * pair + 1):
                h = NSA_GROUP * k + g
                c = slice(g * qb, (g + 1) * qb)
                o_t.append(gates[3 * h:3 * h + 1, :] * o_c[:, c] + gates[3 * h + 1:3 * h + 2, :] * o_s[:, c]
                           + gates[3 * h + 2:3 * h + 3, :] * o_w[:, c])
            lane0 = NSA_HD * (NSA_GROUP * k + 2 * pair)
            o_ref[:, lane0:lane0 + 2 * NSA_HD] = jnp.concatenate(o_t, axis=0).T


def _nsa_seq(q_t, gates_t, kcb, vcb_t, nat, rows_t, wrows_t):
    nb, _, s_len = q_t.shape
    assert s_len % NSA_TILE == 0 and (s_len // CMP_BLOCK) % 2 == 0 and (s_len // SEL_BLOCK) % 8 == 0
    assert s_len // SEL_BLOCK <= NSA_HD and 2 * NSA_HD == NSA_TILE
    nc = s_len // CMP_BLOCK
    hk = NSA_HK
    cols = NSA_GROUP * NSA_TILE
    per_b = lambda shape, idx: pl.BlockSpec((None,) + shape, lambda b, i: (b,) + idx)
    return pl.pallas_call(
        functools.partial(_nsa_seq_kernel, s_len=s_len), grid=(nb, s_len // NSA_TILE),
        in_specs=[pl.BlockSpec((None, NSA_HQ, NSA_TILE), lambda b, i: (b, 0, i)),
                  pl.BlockSpec((None, 3 * NSA_HEADS, NSA_TILE), lambda b, i: (b, 0, i)),
                  per_b((nc, hk), (0, 0)), per_b((hk, nc), (0, 0)),
                  per_b((s_len, 2 * hk), (0, 1)),
                  per_b((hk, s_len), (3, 0)),
                  per_b((hk, s_len), (1, 0))],
        out_specs=pl.BlockSpec((None, NSA_TILE, NSA_HQ), lambda b, i: (b, i, 0)),
        out_shape=jax.ShapeDtypeStruct((nb, s_len, NSA_HQ), f32),
        scratch_shapes=[pltpu.VMEM((NSA_KV_HEADS, 2 * NSA_HD, cols), bf16),
                        pltpu.VMEM((NSA_KV_HEADS, NSA_HD, cols), f32), pltpu.VMEM((NSA_KV_HEADS, NSA_HD, cols), f32),
                        pltpu.VMEM((NSA_KV_HEADS, 1, cols), f32), pltpu.VMEM((NSA_KV_HEADS, 1, cols), f32),
                        pltpu.VMEM((NSA_KV_HEADS, NSA_HD, cols), f32)],
        compiler_params=_cparams(("parallel", "arbitrary")), name="nsa_seq",
    )(q_t, gates_t, kcb, vcb_t, nat, rows_t, wrows_t)


DEC_CHUNK_PAGES = 8


def _flash_step_t(qk, k_t, v_t, ok, carry):
    m, l, acc = carry
    s = jnp.dot(qk, k_t, preferred_element_type=f32)
    s = jnp.where(ok, s, NEG_INF)
    m_new = jnp.maximum(m, jnp.max(s, -1, keepdims=True))
    alpha = jnp.exp(m - m_new)
    p = jnp.exp(s - m_new)
    l = alpha * l + jnp.sum(p, -1, keepdims=True)
    acc = alpha * acc + lax.dot_general(p.astype(bf16), v_t, (((1,), (1,)), ((), ())), preferred_element_type=f32)
    return m_new, l, acc


def _nsa_dec_kernel(pt_ref, q_ref, g_ref, cos_ref, sin_ref, avg_ref, pool_hbm, new_ref, win_ref, wnew_ref, o_ref,
                    buf, sems, cmp_sc, bias_sc, m_sc, l_sc, acc_sc, *, past_len, n_new):
    b = pl.program_id(0)
    rows = NSA_GROUP * n_new
    t_len = past_len + n_new
    nc = t_len // CMP_BLOCK
    nsb = -(-t_len // SEL_BLOCK)
    n_pages = past_len // NSA_TILE
    cpg = DEC_CHUNK_PAGES
    n_chunks = n_pages // cpg
    ckeys = cpg * NSA_TILE
    new_pad = new_ref.shape[0]

    def page_copy(c, i):
        half = c // n_chunks
        page = pt_ref[b, (c % n_chunks) * cpg + i]
        return pltpu.make_async_copy(pool_hbm.at[page, half], buf.at[c % 2, :, pl.ds(i * NSA_TILE, NSA_TILE)],
                                     sems.at[c % 2])

    def start_chunk(c):
        for i in range(cpg):
            page_copy(c, i).start()

    def wait_chunk(c):
        for i in range(cpg):
            page_copy(c, i).wait()

    start_chunk(0)
    cmp_sc[...] = jnp.zeros_like(cmp_sc)

    def pass1(c, carry):
        start_chunk(c + 1)
        wait_chunk(c)
        x = buf[c % 2]
        avg = avg_ref[c]
        hi = x.astype(bf16)
        r1 = x - hi.astype(f32)
        mid = r1.astype(bf16)
        lo = (r1 - mid.astype(f32)).astype(bf16)
        cmp_sc[...] += (jnp.dot(hi, avg, preferred_element_type=f32) + jnp.dot(mid, avg, preferred_element_type=f32)
                        + jnp.dot(lo, avg, preferred_element_type=f32))
        return carry

    lax.fori_loop(0, n_chunks, pass1, 0)

    nr = NSA_KV_HEADS * rows
    rid = lax.broadcasted_iota(jnp.int32, (nr, 1), 0)
    t_all = past_len + rid % n_new
    zero_q = jnp.zeros((rows, NSA_HD), bf16)
    q_bd = jnp.concatenate(
        [jnp.concatenate([q_ref[k].astype(bf16) if kk == k else zero_q for kk in range(NSA_KV_HEADS)], axis=1)
         for k in range(NSA_KV_HEADS)], axis=0)
    nt = (((1,), (1,)), ((), ()))

    cl = lax.broadcasted_iota(jnp.int32, (1, nc), 1)
    cblk = jnp.where(cl < nc // 2, 2 * cl, 2 * (cl - nc // 2) + 1)
    cmask = (cblk + 1) * CMP_BLOCK - 1 <= t_all
    rr = lax.broadcasted_iota(jnp.int32, (nr, nr), 0)
    rc = lax.broadcasted_iota(jnp.int32, (nr, nr), 1)
    group_sum = jnp.where((rr % n_new == rc % n_new) & (rr // rows == rc // rows), 1.0, 0.0)
    blk = lax.broadcasted_iota(jnp.int32, (nr, nc), 1)
    valid = blk * SEL_BLOCK <= t_all
    forced = (blk == 0) | (blk == t_all // SEL_BLOCK)
    cos, sin = cos_ref[...], sin_ref[...]
    kcb_t = _rope_rows(cmp_sc[0:NSA_HK, :], cos, sin, NSA_KV_HEADS).astype(bf16)
    vcb_t = cmp_sc[NSA_HK:2 * NSA_HK, :].astype(bf16)
    s_c = jnp.dot(q_bd, kcb_t, preferred_element_type=f32)
    p_c = _softmax_rows(jnp.where(cmask, s_c, NEG_INF))
    p_c = p_c * jnp.where(t_all >= CMP_BLOCK - 1, 1.0, 0.0)
    o_c = lax.dot_general(p_c.astype(bf16), vcb_t, nt, preferred_element_type=f32)
    imp = jnp.dot(group_sum, p_c, precision=lax.Precision.HIGHEST, preferred_element_type=f32)
    imp = imp[:, :nc // 2] + imp[:, nc // 2:]
    imp = jnp.concatenate([imp, jnp.zeros((nr, nc - nc // 2), f32)], axis=1)
    score = jnp.where(forced, FORCE_SCORE, jnp.where(valid, imp, -1.0))
    score = jnp.where(blk < nsb, score, -2.0)
    sel = _topk_mask_lanes(score, SEL_TOPK, nsb)
    bias_sc[...] = jnp.where(sel > 0.5, 0.0, NEG_INF).astype(bf16)
    in_last = sel[:, nsb - 1:nsb] > 0.5
    m_sc[...] = jnp.full(m_sc.shape, NEG_INF, f32)
    l_sc[...] = jnp.zeros_like(l_sc)
    acc_sc[...] = jnp.zeros_like(acc_sc)

    def flash(s, v, v_transposed):
        m_new = jnp.maximum(m_sc[...], jnp.max(s, -1, keepdims=True))
        alpha = jnp.exp(m_sc[...] - m_new)
        p = jnp.exp(s - m_new)
        l_sc[...] = alpha * l_sc[...] + jnp.sum(p, -1, keepdims=True)
        m_sc[...] = m_new
        if v_transposed:
            pv = lax.dot_general(p.astype(bf16), v, nt, preferred_element_type=f32)
        else:
            pv = jnp.dot(p.astype(bf16), v, preferred_element_type=f32)
        acc_sc[...] = alpha * acc_sc[...] + pv

    def pass2(c, carry):
        @pl.when(c + 1 < 2 * n_chunks)
        def _():
            start_chunk(c + 1)

        wait_chunk(c)
        key0 = (c - n_chunks) * ckeys
        eb = lax.broadcasted_iota(jnp.int32, (nc, ckeys), 0)
        ek = (key0 + lax.broadcasted_iota(jnp.int32, (nc, ckeys), 1)) // SEL_BLOCK
        onehot = jnp.where(eb == ek, 1.0, 0.0).astype(bf16)
        s = (jnp.dot(q_bd, buf[c % 2, 0:NSA_HK, :].astype(bf16), preferred_element_type=f32)
             + jnp.dot(bias_sc[...], onehot, preferred_element_type=f32))
        flash(s, buf[c % 2, NSA_HK:2 * NSA_HK, :].astype(bf16), True)
        return carry

    lax.fori_loop(n_chunks, 2 * n_chunks, pass2, 0)

    jn = lax.broadcasted_iota(jnp.int32, (1, new_pad), 1)
    npos = past_len + jn
    wpos = past_len - WINDOW + lax.broadcasted_iota(jnp.int32, (1, WINDOW), 1)
    ok = in_last & (npos <= t_all) & (jn < n_new)
    s = lax.dot_general(q_bd, new_ref[:, 0:NSA_HK].astype(bf16), nt, preferred_element_type=f32)
    flash(jnp.where(ok, s, NEG_INF), new_ref[:, NSA_HK:2 * NSA_HK].astype(bf16), False)
    o_s = acc_sc[...] / l_sc[...]

    m_sc[...] = jnp.full(m_sc.shape, NEG_INF, f32)
    l_sc[...] = jnp.zeros_like(l_sc)
    acc_sc[...] = jnp.zeros_like(acc_sc)
    ok = (wpos <= t_all) & (wpos > t_all - WINDOW)
    s = jnp.dot(q_bd, win_ref[0:NSA_HK, :].astype(bf16), preferred_element_type=f32)
    flash(jnp.where(ok, s, NEG_INF), win_ref[NSA_HK:2 * NSA_HK, :].astype(bf16), True)
    ok = (npos <= t_all) & (npos > t_all - WINDOW) & (jn < n_new)
    s = lax.dot_general(q_bd, wnew_ref[:, 0:NSA_HK].astype(bf16), nt, preferred_element_type=f32)
    flash(jnp.where(ok, s, NEG_INF), wnew_ref[:, NSA_HK:2 * NSA_HK].astype(bf16), False)
    o_w = acc_sc[...] / l_sc[...]

    g = jnp.concatenate([g_ref[k] for k in range(NSA_KV_HEADS)], axis=0)
    o = g[:, 0:1] * o_c + g[:, 1:2] * o_s + g[:, 2:3] * o_w
    for k in range(NSA_KV_HEADS):
        o_ref[k] = o[rows * k:rows * (k + 1), NSA_HD * k:NSA_HD * (k + 1)]


def _nsa_dec(q, gates, cos_t, sin_t, pool_v, page_table, new_rows, win_t, new_wrows, past_len, n_new):
    nb, n_pages = page_table.shape
    rows = NSA_GROUP * n_new
    w = 2 * NSA_HK
    nc = (past_len + n_new) // CMP_BLOCK
    assert past_len == n_pages * NSA_TILE and past_len % SEL_BLOCK == 0 and n_new <= CMP_BLOCK
    assert win_t.shape[2] == WINDOW and past_len >= WINDOW and n_pages % DEC_CHUNK_PAGES == 0
    assert nc == past_len // CMP_BLOCK and nc % 2 == 0 and -(-(past_len + n_new) // SEL_BLOCK) <= nc
    assert pool_v.shape[1:] == (2, w, NSA_TILE) and cos_t.shape == (NSA_HD // 2, nc)
    ckeys = DEC_CHUNK_PAGES * NSA_TILE
    blk_of_key = jnp.arange(n_pages * NSA_TILE, dtype=jnp.int32).reshape(-1, ckeys, 1) // CMP_BLOCK
    lane_of_blk = jnp.where(blk_of_key % 2 == 0, blk_of_key // 2, nc // 2 + blk_of_key // 2)
    avg = jnp.where(jnp.arange(nc, dtype=jnp.int32) == lane_of_blk, 1.0 / CMP_BLOCK, 0.0).astype(bf16)
    nr = NSA_KV_HEADS * rows
    per_b = lambda shape: pl.BlockSpec((None,) + shape, lambda b, pt: (b,) + (0,) * len(shape))
    const = lambda shape: pl.BlockSpec(shape, lambda b, pt: (0,) * len(shape), pipeline_mode=pl.Buffered(1))
    grid_spec = pltpu.PrefetchScalarGridSpec(
        num_scalar_prefetch=1, grid=(nb,),
        in_specs=[per_b((NSA_KV_HEADS, rows, NSA_HD)), per_b((NSA_KV_HEADS, rows, 3)),
                  const(cos_t.shape), const(sin_t.shape), const(avg.shape), pl.BlockSpec(memory_space=pl.ANY),
                  pl.BlockSpec((None, new_rows.shape[1], w), lambda b, pt: (b, 0, 1)),
                  per_b((w, WINDOW)), per_b((new_wrows.shape[1], w))],
        out_specs=per_b((NSA_KV_HEADS, rows, NSA_HD)),
        scratch_shapes=[pltpu.VMEM((2, w, ckeys), f32), pltpu.SemaphoreType.DMA((2,)),
                        pltpu.VMEM((w, nc), f32), pltpu.VMEM((nr, nc), bf16),
                        pltpu.VMEM((nr, 1), f32), pltpu.VMEM((nr, 1), f32), pltpu.VMEM((nr, NSA_HK), f32)])
    return pl.pallas_call(
        functools.partial(_nsa_dec_kernel, past_len=past_len, n_new=n_new), grid_spec=grid_spec,
        out_shape=jax.ShapeDtypeStruct((nb, NSA_KV_HEADS, rows, NSA_HD), f32),
        compiler_params=_cparams(("arbitrary",)), name="nsa_dec",
    )(page_table, q, gates, cos_t, sin_t, avg, pool_v, new_rows, win_t, new_wrows)


def _nsa_layer(xp, xs, nbp, nbs, pool, win, page_table, ln_g, ln_b, w_in, w_out):
    hq, hk = NSA_HQ, NSA_HK
    ws = [w_in[:, :hq].astype(bf16), w_in[:, hq:hq + 4 * hk].astype(bf16),
          w_in[:, hq + 4 * hk:hq + 6 * hk].astype(bf16), w_in[:, hq + 6 * hk:].astype(bf16)]
    wo = w_out.astype(bf16)
    eps = [_q_ep, _rows_ep, _wrows_ep, _sigmoid_ep]
    sp_len = xp.shape[0] // nbp
    ss_len = xs.shape[0] // nbs
    n_pages = page_table.shape[1]
    page = pool.shape[1]
    assert page == NSA_TILE
    past_len = n_pages * page
    ms = xs.shape[0]

    cos_s, sin_s = _rope_tables(jnp.tile(past_len + jnp.arange(ss_len, dtype=f32), nbs))
    nat_p, q_t, rows_t, wrows_t, g_t = _nsa_in(xp, nbp, w_in)
    qs, rows_s, wrows_s, gs = _proj(xs, ws, eps, aux=(cos_s, sin_s), aux_period=(ms, ms), tm=ms, name="nsa_in_s")

    def cmp_tables(nc):
        return _rope_tables(jnp.arange(nc, dtype=f32) * CMP_BLOCK + (CMP_BLOCK - 1) / 2.0)

    pages_p = sp_len // page
    pt_p = jnp.arange(nbp * pages_p, dtype=jnp.int32).reshape(nbp, pages_p)
    cmp_p = _even_odd(_nsa_cmp(nat_p.reshape(nbp * pages_p, page, 4 * hk), pt_p, *cmp_tables(sp_len // CMP_BLOCK)))
    op = _nsa_seq(q_t, g_t, cmp_p[:, :, :hk], cmp_p[:, :, hk:].transpose(0, 2, 1),
                  nat_p.reshape(nbp, sp_len, 4 * hk), rows_t, wrows_t)

    pool_v = pool.transpose(0, 2, 3, 4, 1).reshape(pool.shape[0], 2, 2 * hk, page)
    win_t = win.transpose(0, 2, 3, 4, 1).reshape(nbs, 2 * hk, win.shape[1])
    nc_s = past_len // CMP_BLOCK
    cpos = jnp.arange(nc_s, dtype=f32) * CMP_BLOCK + (CMP_BLOCK - 1) / 2.0
    cpos = jnp.concatenate([cpos[0::2], cpos[1::2]])
    inv = ROPE_THETA ** (-jnp.arange(0, NSA_HD, 2, dtype=f32) / NSA_HD)
    ang_t = inv[:, None] * cpos[None, :]
    rows_s3 = rows_s.reshape(nbs, ss_len, 4 * hk)
    wrows_s3 = wrows_s.reshape(nbs, ss_len, 2 * hk)
    new_pad = 8
    tpad = lambda a: jnp.pad(a, ((0, 0), (0, new_pad - ss_len), (0, 0)))
    to_heads = lambda a, last: (a.reshape(nbs, ss_len, NSA_KV_HEADS, NSA_GROUP, last).transpose(0, 2, 3, 1, 4)
                                .reshape(nbs, NSA_KV_HEADS, NSA_GROUP * ss_len, last))
    os_ = _nsa_dec(to_heads(qs, NSA_HD), to_heads(gs, 3), jnp.cos(ang_t), jnp.sin(ang_t), pool_v, page_table,
                   tpad(rows_s3), win_t, tpad(wrows_s3), past_len, ss_len)
    os_ = (os_.reshape(nbs, NSA_KV_HEADS, NSA_GROUP, ss_len, NSA_HD).transpose(0, 3, 1, 2, 4).reshape(ms, hq))

    xp = _outproj_ln(op.reshape(nbp * sp_len, hq), wo, xp, ln_g, ln_b, name="nsa_out")
    xs = _outproj_ln(os_, wo, xs, ln_g, ln_b, tm=ms, name="nsa_out_s")
    kv_shape = (4, NSA_KV_HEADS, NSA_HD)
    win_shape = (2, NSA_KV_HEADS, NSA_HD)
    from_t = lambda a, shape: a.reshape((a.shape[0],) + shape + (a.shape[2],)).transpose(0, 4, 1, 2, 3)
    rp = from_t(rows_t, kv_shape)
    rs = rows_s.reshape((nbs, ss_len) + kv_shape)
    wp = from_t(wrows_t[:, :, sp_len - min(WINDOW, sp_len):], win_shape)
    wsn_t = jnp.concatenate([win_t, wrows_s3.transpose(0, 2, 1)], axis=2)
    wsn = from_t(wsn_t[:, :, wsn_t.shape[2] - min(WINDOW, wsn_t.shape[2]):], win_shape)
    return xp, xs, (rp, rs, wp, wsn)


FFN_TF = 256
MOE_TF = 512
ROW_TILE = 1024


def kernel(x_prompt, x_sample, state_l0_lru_conv, state_l0_lru_h, cache_l1_nsa_kv, cache_l1_nsa_win, page_table,
           state_l2_ssd_conv, state_l2_ssd_ssm, state_l3_lru_conv, state_l3_lru_h, ln_g, ln_b, lru_w_in,
           lru_conv_w, lru_conv_b, lru_w_a, lru_b_a, lru_w_x, lru_b_x, lru_lam, lru_w_out, nsa_w_in, nsa_w_out,
           ssd_w_in, ssd_conv_w, ssd_conv_b, ssd_dt_bias, ssd_a_log, ssd_d, ssd_norm_g, ssd_w_out, ffn_w_in,
           ffn_w_out, moe_router_w, moe_router_b, moe_w_in, moe_w_out):
    nbp, sp_len, d = x_prompt.shape
    nbs, ss_len, _ = x_sample.shape
    xp = x_prompt.reshape(nbp * sp_len, d)
    xs = x_sample.reshape(nbs * ss_len, d)
    ms = xs.shape[0]
    lru_state = {0: (state_l0_lru_conv, state_l0_lru_h), 3: (state_l3_lru_conv, state_l3_lru_h)}
    new = {}
    for i in range(DEPTH):
        kind, j = i % 3, i // 3
        g0, b0 = ln_g[i, 0], ln_b[i, 0]
        if kind == 0:
            conv_s, h_s = lru_state[i]
            xp, xs, new[i] = _lru_layer(xp, xs, nbp, nbs, conv_s, h_s, g0, b0, lru_w_in[j], lru_conv_w[j],
                                        lru_conv_b[j], lru_w_a[j], lru_b_a[j], lru_w_x[j], lru_b_x[j],
                                        lru_lam[j], lru_w_out[j])
        elif kind == 1:
            xp, xs, new[i] = _nsa_layer(xp, xs, nbp, nbs, cache_l1_nsa_kv, cache_l1_nsa_win, page_table, g0, b0,
                                        nsa_w_in[j], nsa_w_out[j])
        else:
            xp, xs, new[i] = _ssd_layer(xp, xs, nbp, nbs, state_l2_ssd_conv, state_l2_ssd_ssm, g0, b0,
                                        ssd_w_in[j], ssd_conv_w[j], ssd_conv_b[j], ssd_dt_bias[j], ssd_a_log[j],
                                        ssd_d[j], ssd_norm_g[j], ssd_w_out[j])
        g1, b1 = ln_g[i, 1], ln_b[i, 1]
        k = i // 2
        if i % 2 == 0:
            xp = _ffn_ln(xp, ffn_w_in, ffn_w_out, k, g1, b1, ROW_TILE, FFN_TF, name="ffn")
            xs = _ffn_ln(xs, ffn_w_in, ffn_w_out, k, g1, b1, ms, FFN_TF, name="ffn_s")
        else:
            xp = _moe_sparse_ln(xp, moe_router_w[k], moe_router_b[k], moe_w_in, moe_w_out, k, g1, b1, MOE_TF)
            cs = _router(xs, moe_router_w[k], moe_router_b[k], tm=ms)
            xs = _moe_ln(xs, cs, moe_w_in, moe_w_out, k, g1, b1, ms, MOE_TF)
    out = [xp.reshape(nbp, sp_len, d), xs.reshape(nbs, ss_len, d)]
    for i in range(DEPTH):
        out.extend(new[i])
    return tuple(out)
```

```python
import functools
import math

import jax
import jax.numpy as jnp
from jax import lax
from jax.experimental import pallas as pl
from jax.experimental.pallas import tpu as pltpu

f32 = jnp.float32
bf16 = jnp.bfloat16

D_MODEL = 1024
DEPTH = 4
ALPHA = (2 * DEPTH) ** 0.25
LN_EPS = 1e-5
RMS_EPS = 1e-5
CONV_W = 4
NEG_INF = -1e30

D_RNN = 1344
LRU_BLOCKS = 16
LRU_BS = D_RNN // LRU_BLOCKS
LRU_C = 8.0

NSA_HEADS = 16
NSA_KV_HEADS = 4
NSA_HD = 64
NSA_GROUP = NSA_HEADS // NSA_KV_HEADS
CMP_BLOCK = 32
SEL_BLOCK = 64
SEL_TOPK = 16
WINDOW = 512
FORCE_SCORE = 1e4
ROPE_THETA = 10000.0
NSA_HQ = NSA_HEADS * NSA_HD
NSA_HK = NSA_KV_HEADS * NSA_HD

SSD_INNER = 2 * D_MODEL
SSD_HEAD_DIM = 64
SSD_HEADS = SSD_INNER // SSD_HEAD_DIM
SSD_GROUPS = 4
SSD_STATE = 128
SSD_CHUNK = 128
SSD_CONV_DIM = SSD_INNER + 2 * SSD_GROUPS * SSD_STATE

N_EXPERTS = 8
TOP_K = 2

VMEM_LIMIT_V7X = 56 * 1024 * 1024


def _cparams(sem):
    return pltpu.CompilerParams(dimension_semantics=sem, vmem_limit_bytes=VMEM_LIMIT_V7X)


def _const_spec(shape):
    nd = len(shape)
    return pl.BlockSpec(shape, lambda *_: (0,) * nd, pipeline_mode=pl.Buffered(1))


def _layer_norm(v, g, b):
    mu = jnp.mean(v, -1, keepdims=True)
    d = v - mu
    var = jnp.mean(d * d, -1, keepdims=True)
    return d * lax.rsqrt(var + LN_EPS) * g + b


def _proj_kernel(*refs, epilogues, n_aux):
    n_out = len(epilogues)
    x_ref = refs[0]
    w_refs = refs[1:1 + n_out]
    aux_refs = refs[1 + n_out:1 + n_out + n_aux]
    o_refs = refs[1 + n_out + n_aux:]
    x = x_ref[...].astype(bf16)
    aux = [a[...] for a in aux_refs]
    for w_ref, o_ref, ep in zip(w_refs, o_refs, epilogues):
        y = jnp.dot(x, w_ref[...], preferred_element_type=f32)
        if ep is not None:
            y = ep(y, *aux)
        o_ref[...] = y.astype(o_ref.dtype)


def _proj(x, ws, epilogues, aux=(), aux_period=(), tm=256, name="proj"):
    m, k = x.shape
    assert m % tm == 0
    in_specs = [pl.BlockSpec((tm, k), lambda i: (i, 0))]
    in_specs += [_const_spec(w.shape) for w in ws]
    for a, p in zip(aux, aux_period):
        if p:
            assert p % tm == 0 and a.shape[0] == p
            in_specs.append(pl.BlockSpec((tm, a.shape[1]), functools.partial(lambda i, n: (i % n, 0), n=p // tm)))
        else:
            in_specs.append(_const_spec(a.shape))
    out_shape = [jax.ShapeDtypeStruct((m, w.shape[1]), f32) for w in ws]
    out_specs = [pl.BlockSpec((tm, w.shape[1]), lambda i: (i, 0)) for w in ws]
    return pl.pallas_call(
        functools.partial(_proj_kernel, epilogues=tuple(epilogues), n_aux=len(aux)),
        grid=(m // tm,), in_specs=in_specs, out_specs=out_specs, out_shape=out_shape,
        compiler_params=_cparams(("parallel",)), name=name,
    )(x, *ws, *aux)


def _outproj_ln_kernel(a_ref, w_ref, x_ref, g_ref, b_ref, o_ref):
    y = jnp.dot(a_ref[...].astype(bf16), w_ref[...], preferred_element_type=f32)
    o_ref[...] = _layer_norm(ALPHA * x_ref[...] + y, g_ref[...], b_ref[...])


def _outproj_ln(a, w, x, g, b, tm=256, name="outproj_ln"):
    m, k = a.shape
    d = w.shape[1]
    assert m % tm == 0
    return pl.pallas_call(
        _outproj_ln_kernel, grid=(m // tm,),
        in_specs=[pl.BlockSpec((tm, k), lambda i: (i, 0)), _const_spec(w.shape),
                  pl.BlockSpec((tm, d), lambda i: (i, 0)), _const_spec((1, d)), _const_spec((1, d))],
        out_specs=pl.BlockSpec((tm, d), lambda i: (i, 0)),
        out_shape=jax.ShapeDtypeStruct((m, d), f32),
        compiler_params=_cparams(("parallel",)), name=name,
    )(a, w, x, g.reshape(1, d), b.reshape(1, d))


def _ffn_ln_kernel(x_ref, wg_ref, wu_ref, wo_ref, g_ref, b_ref, o_ref, h_ref):
    j = pl.program_id(1)
    tf = wg_ref.shape[1]
    xb = x_ref[...].astype(bf16)
    hg = jnp.dot(xb, wg_ref[...].astype(bf16), preferred_element_type=f32)
    hu = jnp.dot(xb, wu_ref[...].astype(bf16), preferred_element_type=f32)
    h_ref[:, pl.ds(pl.multiple_of(j * tf, tf), tf)] = (hg * jax.nn.sigmoid(hg) * hu).astype(bf16)

    @pl.when(j == pl.num_programs(1) - 1)
    def _():
        y = jnp.dot(h_ref[...], wo_ref[...].astype(bf16), preferred_element_type=f32)
        o_ref[...] = _layer_norm(ALPHA * x_ref[...] + y, g_ref[...], b_ref[...])


def _ffn_ln(x, w_in, w_out, layer, g, b, tm, tf, name="ffn_ln"):
    m, d = x.shape
    f = w_out.shape[1]
    assert m % tm == 0 and f % tf == 0 and tf % 128 == 0
    nf = f // tf
    return pl.pallas_call(
        _ffn_ln_kernel, grid=(m // tm, nf),
        in_specs=[pl.BlockSpec((tm, d), lambda i, j: (i, 0)),
                  pl.BlockSpec((None, d, tf), lambda i, j: (layer, 0, j)),
                  pl.BlockSpec((None, d, tf), lambda i, j: (layer, 0, j + nf)),
                  pl.BlockSpec((None, f, d), lambda i, j: (layer, 0, 0), pipeline_mode=pl.Buffered(1)),
                  _const_spec((1, d)), _const_spec((1, d))],
        out_specs=pl.BlockSpec((tm, d), lambda i, j: (i, 0)),
        out_shape=jax.ShapeDtypeStruct((m, d), f32),
        scratch_shapes=[pltpu.VMEM((tm, f), bf16)],
        compiler_params=_cparams(("parallel", "arbitrary")), name=name,
    )(x, w_in, w_in, w_out, g.reshape(1, d), b.reshape(1, d))


def _router_kernel(x_ref, w_ref, b_ref, c_ref):
    logits = jnp.dot(x_ref[...], w_ref[...], precision=lax.Precision.HIGHEST,
                     preferred_element_type=f32) + b_ref[...]
    e_idx, i1, i2, w1, w2 = _top2(logits)
    c_ref[...] = jnp.where(e_idx == i1, w1, 0.0) + jnp.where(e_idx == i2, w2, 0.0)


def _router(x, w, b, tm=512):
    m, d = x.shape
    assert m % tm == 0
    return pl.pallas_call(
        _router_kernel, grid=(m // tm,),
        in_specs=[pl.BlockSpec((tm, d), lambda i: (i, 0)), _const_spec(w.shape),
                  _const_spec((1, N_EXPERTS))],
        out_specs=pl.BlockSpec((tm, N_EXPERTS), lambda i: (i, 0)),
        out_shape=jax.ShapeDtypeStruct((m, N_EXPERTS), f32),
        compiler_params=_cparams(("parallel",)), name="router",
    )(x, w, b.reshape(1, N_EXPERTS))


def _moe_ln_kernel(x_ref, c_ref, wg_ref, wu_ref, wo_ref, g_ref, b_ref, o_ref, acc_ref):
    e = pl.program_id(1)
    j = pl.program_id(2)
    x = x_ref[...]
    xb = x.astype(bf16)
    c = c_ref[...]
    ce = jnp.sum(jnp.where(lax.broadcasted_iota(jnp.int32, c.shape, 1) == e, c, 0.0), -1, keepdims=True)
    hg = jnp.dot(xb, wg_ref[...].astype(bf16), preferred_element_type=f32)
    hu = jnp.dot(xb, wu_ref[...].astype(bf16), preferred_element_type=f32)
    act = (hg * jax.nn.sigmoid(hg) * hu).astype(bf16)
    part = ce * jnp.dot(act, wo_ref[...].astype(bf16), preferred_element_type=f32)
    first = jnp.logical_and(e == 0, j == 0)

    @pl.when(first)
    def _():
        acc_ref[...] = part

    @pl.when(jnp.logical_not(first))
    def _():
        acc_ref[...] += part

    @pl.when(jnp.logical_and(e == pl.num_programs(1) - 1, j == pl.num_programs(2) - 1))
    def _():
        o_ref[...] = _layer_norm(ALPHA * x + acc_ref[...], g_ref[...], b_ref[...])


def _moe_ln(x, comb, w_in, w_out, layer, g, b, tm, tf):
    m, d = x.shape
    _, ne, f, _ = w_out.shape
    assert m % tm == 0 and f % tf == 0
    nf = f // tf
    return pl.pallas_call(
        _moe_ln_kernel, grid=(m // tm, ne, nf),
        in_specs=[pl.BlockSpec((tm, d), lambda i, e, j: (i, 0)),
                  pl.BlockSpec((tm, ne), lambda i, e, j: (i, 0)),
                  pl.BlockSpec((None, None, d, tf), lambda i, e, j: (layer, e, 0, j)),
                  pl.BlockSpec((None, None, d, tf), lambda i, e, j: (layer, e, 0, j + nf)),
                  pl.BlockSpec((None, None, tf, d), lambda i, e, j: (layer, e, j, 0)),
                  _const_spec((1, d)), _const_spec((1, d))],
        out_specs=pl.BlockSpec((tm, d), lambda i, e, j: (i, 0)),
        out_shape=jax.ShapeDtypeStruct((m, d), f32),
        scratch_shapes=[pltpu.VMEM((tm, d), f32)],
        compiler_params=_cparams(("parallel", "arbitrary", "arbitrary")), name="moe_ln",
    )(x, comb, w_in, w_in, w_out, g.reshape(1, d), b.reshape(1, d))


MOE_TM = 512
MOE_ROUTE_TM = 512
MOE_DISPATCH_TM = 512
MOE_COMBINE_TM = 256
ROUTE_COLS = 8
ROW_DMA_UNROLL = 8


def _top2(logits):
    e_idx = lax.broadcasted_iota(jnp.int32, logits.shape, 1)
    v1 = jnp.max(logits, -1, keepdims=True)
    i1 = jnp.min(jnp.where(logits == v1, e_idx, N_EXPERTS), -1, keepdims=True)
    rest = jnp.where(e_idx == i1, -jnp.inf, logits)
    v2 = jnp.max(rest, -1, keepdims=True)
    i2 = jnp.min(jnp.where(rest == v2, e_idx, N_EXPERTS), -1, keepdims=True)
    e2 = jnp.exp(v2 - v1)
    den = 1.0 + e2
    return e_idx, i1, i2, 1.0 / den, e2 / den


def _route_kernel(x_ref, w_ref, b_ref, r_ref, cnt_ref, carry_ref):
    i = pl.program_id(0)
    tm = x_ref.shape[0]

    @pl.when(i == 0)
    def _():
        carry_ref[...] = jnp.zeros_like(carry_ref)

    logits = jnp.dot(x_ref[...], w_ref[...], precision=lax.Precision.HIGHEST,
                     preferred_element_type=f32) + b_ref[...]
    e_idx, i1, i2, w1, w2 = _top2(logits)
    hit1 = e_idx == i1
    hit2 = e_idx == i2
    a = jnp.where(hit1, 1.0, 0.0) + jnp.where(hit2, 1.0, 0.0)
    row = lax.broadcasted_iota(jnp.int32, (tm, tm), 0)
    col = lax.broadcasted_iota(jnp.int32, (tm, tm), 1)
    before = jnp.where(col < row, 1.0, 0.0).astype(bf16)
    c = carry_ref[...] + jnp.dot(before, a.astype(bf16), preferred_element_type=f32)
    pos1 = jnp.sum(jnp.where(hit1, c, 0.0), -1, keepdims=True)
    pos2 = jnp.sum(jnp.where(hit2, c, 0.0), -1, keepdims=True)
    carry_ref[...] += jnp.sum(a, 0, keepdims=True)
    cols = (i1.astype(f32), i2.astype(f32), pos1, pos2, w1, w2)
    rec = jnp.zeros((tm, ROUTE_COLS), f32)
    for n, v in enumerate(cols):
        rec = jnp.where(e_idx == n, v, rec)
    r_ref[...] = rec

    @pl.when(i == pl.num_programs(0) - 1)
    def _():
        cnt_ref[...] = carry_ref[...]


def _route(x, w, b):
    m, d = x.shape
    tm = MOE_ROUTE_TM
    assert m % tm == 0 and ROUTE_COLS == N_EXPERTS
    return pl.pallas_call(
        _route_kernel, grid=(m // tm,),
        in_specs=[pl.BlockSpec((tm, d), lambda i: (i, 0)), _const_spec(w.shape), _const_spec((1, N_EXPERTS))],
        out_specs=[pl.BlockSpec((tm, ROUTE_COLS), lambda i: (i, 0)), pl.BlockSpec((1, N_EXPERTS), lambda i: (0, 0))],
        out_shape=[jax.ShapeDtypeStruct((m, ROUTE_COLS), f32), jax.ShapeDtypeStruct((1, N_EXPERTS), f32)],
        scratch_shapes=[pltpu.VMEM((1, N_EXPERTS), f32)],
        compiler_params=_cparams(("arbitrary",)), name="moe_route",
    )(x, w, b.reshape(1, N_EXPERTS))


def _row_copy(src, src_row, dst, dst_row, sem):
    return pltpu.make_async_copy(src.at[pl.ds(src_row, 1)], dst.at[pl.ds(dst_row, 1)], sem)


def _dispatch_kernel(d1_ref, d2_ref, x_ref, init_hbm, o_hbm, sem):
    del init_hbm
    base = pl.program_id(0) * MOE_DISPATCH_TM

    def copies(t):
        tok = base + t
        return (_row_copy(x_ref, t, o_hbm, d1_ref[tok], sem), _row_copy(x_ref, t, o_hbm, d2_ref[tok], sem))

    def issue(t, carry):
        for cp in copies(t):
            cp.start()
        return carry

    def drain(t, carry):
        for cp in copies(t):
            cp.wait()
        return carry

    lax.fori_loop(0, MOE_DISPATCH_TM, issue, 0, unroll=ROW_DMA_UNROLL)
    lax.fori_loop(0, MOE_DISPATCH_TM, drain, 0, unroll=ROW_DMA_UNROLL)


def _dispatch(x, d1, d2, n_rows):
    m, d = x.shape
    assert m % MOE_DISPATCH_TM == 0
    any_spec = pl.BlockSpec(memory_space=pl.ANY)
    grid_spec = pltpu.PrefetchScalarGridSpec(
        num_scalar_prefetch=2, grid=(m // MOE_DISPATCH_TM,),
        in_specs=[pl.BlockSpec((MOE_DISPATCH_TM, d), lambda i, *_: (i, 0)), any_spec], out_specs=any_spec,
        scratch_shapes=[pltpu.SemaphoreType.DMA(())])
    return pl.pallas_call(
        _dispatch_kernel, grid_spec=grid_spec, out_shape=jax.ShapeDtypeStruct((n_rows, d), x.dtype),
        input_output_aliases={3: 0},
        compiler_params=_cparams(("arbitrary",)), name="moe_dispatch",
    )(d1, d2, x, jnp.zeros((n_rows, d), x.dtype))


def _moe_group_kernel(te_ref, nu_ref, x_ref, wg_ref, wu_ref, wo_ref, o_ref, h_ref):
    del te_ref
    t = pl.program_id(0)
    j = pl.program_id(1)
    tf = wg_ref.shape[1]
    used = t < nu_ref[0]
    last = j == pl.num_programs(1) - 1

    @pl.when(used)
    def _():
        xb = x_ref[...].astype(bf16)
        hg = jnp.dot(xb, wg_ref[...].astype(bf16), preferred_element_type=f32)
        hu = jnp.dot(xb, wu_ref[...].astype(bf16), preferred_element_type=f32)
        h_ref[:, pl.ds(pl.multiple_of(j * tf, tf), tf)] = (hg * jax.nn.sigmoid(hg) * hu).astype(bf16)

        @pl.when(last)
        def _():
            o_ref[...] = jnp.dot(h_ref[...], wo_ref[...].astype(bf16), preferred_element_type=f32)

    @pl.when(jnp.logical_and(jnp.logical_not(used), last))
    def _():
        o_ref[...] = jnp.zeros_like(o_ref)


def _moe_group(xs, tile_expert, n_used, w_in, w_out, layer, tf):
    r, d = xs.shape
    _, ne, f, _ = w_out.shape
    assert r % MOE_TM == 0 and f % tf == 0
    nf = f // tf
    n_tiles = r // MOE_TM

    def jj(t, j, nu):
        return jnp.where(t < nu[0], j, nf - 1)

    grid_spec = pltpu.PrefetchScalarGridSpec(
        num_scalar_prefetch=2, grid=(n_tiles, nf),
        in_specs=[pl.BlockSpec((MOE_TM, d), lambda t, j, te, nu: (jnp.maximum(jnp.minimum(t, nu[0] - 1), 0), 0)),
                  pl.BlockSpec((None, None, d, tf), lambda t, j, te, nu: (layer, te[t], 0, jj(t, j, nu))),
                  pl.BlockSpec((None, None, d, tf), lambda t, j, te, nu: (layer, te[t], 0, jj(t, j, nu) + nf)),
                  pl.BlockSpec((None, None, f, d), lambda t, j, te, nu: (layer, te[t], 0, 0),
                               pipeline_mode=pl.Buffered(1))],
        out_specs=pl.BlockSpec((MOE_TM, d), lambda t, j, te, nu: (t, 0)),
        scratch_shapes=[pltpu.VMEM((MOE_TM, f), bf16)])
    return pl.pallas_call(
        _moe_group_kernel, grid_spec=grid_spec, out_shape=jax.ShapeDtypeStruct((r, d), f32),
        compiler_params=_cparams(("arbitrary", "arbitrary")), name="moe_group",
    )(tile_expert, n_used, xs, w_in, w_in, w_out)


def _moe_combine_kernel(d1_ref, d2_ref, x_ref, r_ref, y_hbm, g_ref, b_ref, o_ref, ya_ref, yb_ref, sem):
    tm = x_ref.shape[0]
    base = pl.program_id(0) * tm

    def copies(t):
        tok = base + t
        return (_row_copy(y_hbm, d1_ref[tok], ya_ref, t, sem), _row_copy(y_hbm, d2_ref[tok], yb_ref, t, sem))

    def issue(t, carry):
        for cp in copies(t):
            cp.start()
        return carry

    def drain(t, carry):
        for cp in copies(t):
            cp.wait()
        return carry

    lax.fori_loop(0, tm, issue, 0, unroll=ROW_DMA_UNROLL)
    lax.fori_loop(0, tm, drain, 0, unroll=ROW_DMA_UNROLL)
    r = r_ref[...]
    mix = r[:, 4:5] * ya_ref[...] + r[:, 5:6] * yb_ref[...]
    o_ref[...] = _layer_norm(ALPHA * x_ref[...] + mix, g_ref[...], b_ref[...])


def _moe_combine(x, route, y, d1, d2, g, b):
    m, d = x.shape
    tm = MOE_COMBINE_TM
    assert m % tm == 0
    grid_spec = pltpu.PrefetchScalarGridSpec(
        num_scalar_prefetch=2, grid=(m // tm,),
        in_specs=[pl.BlockSpec((tm, d), lambda i, *_: (i, 0)),
                  pl.BlockSpec((tm, ROUTE_COLS), lambda i, *_: (i, 0)),
                  pl.BlockSpec(memory_space=pl.ANY),
                  pl.BlockSpec((1, d), lambda i, *_: (0, 0)), pl.BlockSpec((1, d), lambda i, *_: (0, 0))],
        out_specs=pl.BlockSpec((tm, d), lambda i, *_: (i, 0)),
        scratch_shapes=[pltpu.VMEM((tm, d), f32), pltpu.VMEM((tm, d), f32), pltpu.SemaphoreType.DMA(())])
    return pl.pallas_call(
        _moe_combine_kernel, grid_spec=grid_spec, out_shape=jax.ShapeDtypeStruct((m, d), f32),
        compiler_params=_cparams(("arbitrary",)), name="moe_combine",
    )(d1, d2, x, route, y, g.reshape(1, d), b.reshape(1, d))


def _moe_sparse_ln(x, router_w, router_b, w_in, w_out, layer, g, b, tf):
    m, d = x.shape
    route, counts = _route(x, router_w, router_b)
    counts = counts[0].astype(jnp.int32)
    padded = (counts + MOE_TM - 1) // MOE_TM * MOE_TM
    ends = jnp.cumsum(padded)
    offs = ends - padded
    n_tiles = -(-TOP_K * m // MOE_TM) + N_EXPERTS
    n_used = (ends[-1] // MOE_TM).reshape(1)
    tile_start = jnp.minimum(jnp.arange(n_tiles, dtype=jnp.int32), n_used[0] - 1) * MOE_TM
    tile_expert = jnp.minimum(jnp.sum((tile_start[:, None] >= ends[None, :]).astype(jnp.int32), axis=1), N_EXPERTS - 1)
    e1 = route[:, 0].astype(jnp.int32)
    e2 = route[:, 1].astype(jnp.int32)
    d1 = offs[e1] + route[:, 2].astype(jnp.int32)
    d2 = offs[e2] + route[:, 3].astype(jnp.int32)
    xs = _dispatch(x, d1, d2, n_tiles * MOE_TM)
    y = _moe_group(xs, tile_expert, n_used, w_in, w_out, layer, tf)
    return _moe_combine(x, route, y, d1, d2, g, b)


def _softplus(x):
    return jnp.maximum(x, 0.0) + jnp.log(1.0 + jnp.exp(-jnp.abs(x)))


LRU_BAND_OUT = 256


def _lru_bands():
    bands = []
    for out_lo in range(0, D_RNN, LRU_BAND_OUT):
        n = min(LRU_BAND_OUT, D_RNN - out_lo)
        in_lo = out_lo // LRU_BS * LRU_BS // 128 * 128
        in_hi = min(-(-(((out_lo + n - 1) // LRU_BS + 1) * LRU_BS) // 128) * 128, D_RNN)
        bands.append((out_lo, n, in_lo, in_hi - in_lo))
    return bands


def _band_weights(w):
    dense = _block_diag(w)
    bands = _lru_bands()
    kmax = max(b[3] for b in bands)
    out = []
    for out_lo, n, in_lo, k in bands:
        out.append(jnp.pad(dense[in_lo:in_lo + k, out_lo:out_lo + n], ((0, kmax - k), (0, LRU_BAND_OUT - n))))
    return jnp.stack(out).astype(bf16)


def _sigmoid_tanh(x):
    return 0.5 * jnp.tanh(0.5 * x) + 0.5


def _lru_gates(xc, wa_ref, ba, wx_ref, bx, sp):
    xcb = xc.astype(bf16)
    ra, rx = [], []
    for g, (_, n, in_lo, k) in enumerate(_lru_bands()):
        xin = xcb[:, in_lo:in_lo + k]
        ra.append(jnp.dot(xin, wa_ref[g, 0:k, 0:n], preferred_element_type=f32))
        rx.append(jnp.dot(xin, wx_ref[g, 0:k, 0:n], preferred_element_type=f32))
    r = _sigmoid_tanh(jnp.concatenate(ra, axis=1) + ba)
    gi = _sigmoid_tanh(jnp.concatenate(rx, axis=1) + bx)
    log_a = sp * r
    th = jnp.tanh(log_a)
    one_minus_a2 = -2.0 * th / (1.0 - th)
    return jnp.exp(log_a), jnp.sqrt(one_minus_a2) * gi * xc


def _lru_core_kernel(xb_ref, g_ref, cw_ref, cb_ref, wa_ref, ba_ref, wx_ref, bx_ref, lam_ref,
                     y_ref, hlast_ref, ext_ref, a_ref, b_ref, h_ref, *, ts, nb):
    i = pl.program_id(0)
    hist = 8

    @pl.when(i == 0)
    def _():
        ext_ref[:, 0:hist, :] = jnp.zeros((nb, hist, ext_ref.shape[2]), f32)
        h_ref[...] = jnp.zeros_like(h_ref)

    ext_ref[:, hist:hist + ts, :] = xb_ref[...]
    sp = -LRU_C * _softplus(-lam_ref[...])
    cw = cw_ref[...]
    for b in range(nb):
        xc = cb_ref[...]
        for k in range(CONV_W):
            off = hist - (CONV_W - 1) + k
            xc = xc + cw[k:k + 1, :] * ext_ref[b, off:off + ts, :]
        a, bt = _lru_gates(xc, wa_ref, ba_ref[...], wx_ref, bx_ref[...], sp)
        a_ref[b] = a
        b_ref[b] = bt
    ext_ref[:, 0:hist, :] = ext_ref[:, ts:ts + hist, :]

    def step(t, h):
        h = a_ref[:, t, :] * h + b_ref[:, t, :]
        b_ref[:, t, :] = h
        return h

    h = lax.fori_loop(0, ts, step, h_ref[...], unroll=8)
    h_ref[...] = h
    y_ref[...] = b_ref[...] * g_ref[...]

    @pl.when(i == pl.num_programs(0) - 1)
    def _():
        hlast_ref[...] = h


def _lru_core(xb, g, cw, cb, wa, ba, wx, bx, lam, ts=64):
    nb, s, c = xb.shape
    assert s % ts == 0 and ts % 8 == 0
    blk = pl.BlockSpec((nb, ts, c), lambda i: (0, i, 0))
    row = _const_spec((1, c))
    return pl.pallas_call(
        functools.partial(_lru_core_kernel, ts=ts, nb=nb), grid=(s // ts,),
        in_specs=[blk, blk, _const_spec((CONV_W, c)), row, _const_spec(wa.shape), row,
                  _const_spec(wx.shape), row, row],
        out_specs=[blk, pl.BlockSpec((nb, c), lambda i: (0, 0))],
        out_shape=[jax.ShapeDtypeStruct((nb, s, c), f32), jax.ShapeDtypeStruct((nb, c), f32)],
        scratch_shapes=[pltpu.VMEM((nb, ts + 8, c), f32), pltpu.VMEM((nb, ts, c), f32),
                        pltpu.VMEM((nb, ts, c), f32), pltpu.VMEM((nb, c), f32)],
        compiler_params=_cparams(("arbitrary",)), name="lru_core",
    )(xb, g, cw, cb.reshape(1, c), wa, ba.reshape(1, c), wx, bx.reshape(1, c), lam.reshape(1, c))


def _lru_short_kernel(xb_ref, g_ref, cs_ref, h0_ref, cw_ref, cb_ref, wa_ref, ba_ref, wx_ref, bx_ref,
                      lam_ref, y_ref, hlast_ref, *, s):
    rows = [cs_ref[k] for k in range(CONV_W - 1)] + [xb_ref[t] for t in range(s)]
    sp = -LRU_C * _softplus(-lam_ref[...])
    cw = cw_ref[...]
    h = h0_ref[...]
    for t in range(s):
        xc = cb_ref[...]
        for k in range(CONV_W):
            xc = xc + cw[k:k + 1, :] * rows[t + k]
        a, bt = _lru_gates(xc, wa_ref, ba_ref[...], wx_ref, bx_ref[...], sp)
        h = a * h + bt
        y_ref[t] = h * g_ref[t]
    hlast_ref[...] = h


def _lru_short(xb, g, cs, h0, cw, cb, wa, ba, wx, bx, lam):
    s, nb, c = xb.shape
    return pl.pallas_call(
        functools.partial(_lru_short_kernel, s=s),
        out_shape=[jax.ShapeDtypeStruct((s, nb, c), f32), jax.ShapeDtypeStruct((nb, c), f32)],
        compiler_params=pltpu.CompilerParams(vmem_limit_bytes=VMEM_LIMIT_V7X), name="lru_short",
    )(xb, g, cs, h0, cw, cb.reshape(1, c), wa, ba.reshape(1, c), wx, bx.reshape(1, c), lam.reshape(1, c))


def _block_diag(w):
    n, k, _ = w.shape
    eye = jnp.eye(n, dtype=w.dtype)
    return (eye[:, None, :, None] * w[:, :, None, :]).reshape(n * k, n * k)


def _gelu_ep(y):
    return jax.nn.gelu(y)


def _lru_layer(xp, xs, nbp, nbs, conv_s, h_s, ln_g, ln_b, w_in, cw, cb, w_a, b_a, w_x, b_x, lam, w_out):
    c = D_RNN
    w_gate = w_in[:, :c].astype(bf16)
    w_xb = w_in[:, c:].astype(bf16)
    wa = _band_weights(w_a)
    wx = _band_weights(w_x)
    wo = w_out.astype(bf16)
    sp_len = xp.shape[0] // nbp
    ss_len = xs.shape[0] // nbs

    gp, xbp = _proj(xp, [w_gate, w_xb], [_gelu_ep, None], tm=256, name="lru_in")
    gs, xbs = _proj(xs, [w_gate, w_xb], [_gelu_ep, None], tm=xs.shape[0], name="lru_in_s")
    xbp3 = xbp.reshape(nbp, sp_len, c)
    yp, hp = _lru_core(xbp3, gp.reshape(nbp, sp_len, c), cw, cb, wa, b_a, wx, b_x, lam)
    xbs3 = xbs.reshape(nbs, ss_len, c)
    ys_t, hs = _lru_short(xbs3.transpose(1, 0, 2), gs.reshape(nbs, ss_len, c).transpose(1, 0, 2),
                          conv_s.transpose(1, 0, 2), h_s, cw, cb, wa, b_a, wx, b_x, lam)
    ys = ys_t.transpose(1, 0, 2).reshape(nbs * ss_len, c)
    xp = _outproj_ln(yp.reshape(nbp * sp_len, c), wo, xp, ln_g, ln_b, name="lru_out")
    xs = _outproj_ln(ys, wo, xs, ln_g, ln_b, tm=xs.shape[0], name="lru_out_s")
    conv_p = xbp3[:, sp_len - (CONV_W - 1):]
    conv_sn = jnp.concatenate([conv_s, xbs3], axis=1)[:, ss_len:]
    return xp, xs, (conv_p, conv_sn, hp, hs)


SSD_GN = SSD_GROUPS * SSD_STATE
SSD_HPG = SSD_HEADS // SSD_GROUPS
SSD_DT_PAD = 128


def _ssd_core_kernel(*refs, q, valid, zero_init):
    if zero_init:
        (xbc_ref, dt_ref, zs_ref, cw_ref, cb_ref, alog_ref, dsk_ref, ng_ref, ex_ref,
         y_ref, sfin_ref, ext_ref, st_ref, yacc_ref) = refs
    else:
        (xbc_ref, dt_ref, zs_ref, hist_ref, s0_ref, cw_ref, cb_ref, alog_ref, dsk_ref, ng_ref, ex_ref,
         y_ref, sfin_ref, ext_ref, st_ref, yacc_ref) = refs
    ci = pl.program_id(1)
    hist = 8
    hp = SSD_HPG * SSD_HEAD_DIM
    pairs = [(g, j) for g in range(SSD_GROUPS) for j in range(SSD_HPG // 2)]

    @pl.when(ci == 0)
    def _():
        if zero_init:
            ext_ref[0:hist, :] = jnp.zeros((hist, ext_ref.shape[1]), f32)
            st_ref[...] = jnp.zeros_like(st_ref)
        else:
            ext_ref[0:hist, :] = hist_ref[...]
            for g, j in pairs:
                h = g * SSD_HPG + 2 * j
                two = jnp.concatenate([s0_ref[h], s0_ref[h + 1]], axis=0)
                st_ref[g, :, 2 * SSD_HEAD_DIM * j:2 * SSD_HEAD_DIM * (j + 1)] = two.T

    ext_ref[hist:hist + q, :] = xbc_ref[...]
    cw = cw_ref[...]
    xc = cb_ref[...]
    for k in range(CONV_W):
        off = hist - (CONV_W - 1) + k
        xc = xc + cw[k:k + 1, :] * ext_ref[off:off + q, :]
    xc = xc * jax.nn.sigmoid(xc)
    if q >= hist:
        ext_ref[0:hist, :] = ext_ref[q:q + hist, :]

    row = lax.broadcasted_iota(jnp.int32, (q, q), 0)
    col = lax.broadcasted_iota(jnp.int32, (q, q), 1)
    causal = col <= row
    dt = dt_ref[...]
    if valid < q:
        dt = jnp.where(lax.broadcasted_iota(jnp.int32, dt.shape, 0) < valid, dt, 0.0)
    a_neg = -jnp.exp(alog_ref[...])
    tril = jnp.where(causal, 1.0, 0.0)
    acum = jnp.dot(tril, dt * a_neg, precision=lax.Precision.HIGHEST, preferred_element_type=f32)
    acum_t = acum.T
    last = acum[q - 1:q, :]

    def per_channel(v):
        hi = v.astype(bf16)
        r1 = v - hi.astype(f32)
        mid = r1.astype(bf16)
        lo = (r1 - mid.astype(f32)).astype(bf16)
        ex = ex_ref[...]
        return (jnp.dot(hi, ex, preferred_element_type=f32) + jnp.dot(mid, ex, preferred_element_type=f32)
                + jnp.dot(lo, ex, preferred_element_type=f32))

    e_acum_x = per_channel(jnp.exp(acum))
    xdt_all = xc[:, :SSD_INNER] * per_channel(dt)
    xdt_b = xdt_all.astype(bf16)
    xdd_b = (xdt_all * per_channel(jnp.exp(last - acum))).astype(bf16)
    dec_chunk_x = e_acum_x[q - 1:q, :]
    two = 2 * SSD_HEAD_DIM
    first_head = lax.broadcasted_iota(jnp.int32, (q, two), 1) < SSD_HEAD_DIM
    nt = (((1,), (1,)), ((), ()))

    for g in range(SSD_GROUPS):
        bm = xc[:, SSD_INNER + g * SSD_STATE:SSD_INNER + (g + 1) * SSD_STATE]
        cm = xc[:, SSD_INNER + SSD_GN + g * SSD_STATE:SSD_INNER + SSD_GN + (g + 1) * SSD_STATE].astype(bf16)
        cb = lax.dot_general(cm, bm.astype(bf16), nt, preferred_element_type=f32)
        ch = slice(g * hp, (g + 1) * hp)
        st = st_ref[g]
        y_off = jnp.dot(cm, st.astype(bf16), preferred_element_type=f32) * e_acum_x[:, ch]
        st_ref[g] = dec_chunk_x[:, ch] * st + jnp.dot(bm.T.astype(bf16), xdd_b[:, ch], preferred_element_type=f32)
        for j in range(SSD_HPG // 2):
            h = g * SSD_HPG + 2 * j
            lanes = slice(h * SSD_HEAD_DIM, (h + 2) * SSD_HEAD_DIM)
            m = []
            for hh in (h, h + 1):
                seg = acum[:, hh:hh + 1] - acum_t[hh:hh + 1, :]
                m.append((cb * jnp.exp(jnp.where(causal, seg, NEG_INF))).astype(bf16))
            if q % 128 == 0:
                x2 = xdt_b[:, lanes]
                zero = jnp.zeros_like(x2)
                rhs = jnp.concatenate([jnp.where(first_head, x2, zero), jnp.where(first_head, zero, x2)], axis=0)
                y_diag = jnp.dot(jnp.concatenate(m, axis=1), rhs, preferred_element_type=f32)
            else:
                y_diag = jnp.concatenate(
                    [jnp.dot(m[n], xdt_b[:, (h + n) * SSD_HEAD_DIM:(h + n + 1) * SSD_HEAD_DIM],
                             preferred_element_type=f32) for n in range(2)], axis=1)
            yacc_ref[:, lanes] = y_diag + y_off[:, 2 * j * SSD_HEAD_DIM:(2 * j + 2) * SSD_HEAD_DIM]

    y = yacc_ref[...] + dsk_ref[...] * xc[:, :SSD_INNER]
    yg = y * zs_ref[...]
    gw = SSD_INNER // SSD_GROUPS
    for g in range(SSD_GROUPS):
        v = yg[:, g * gw:(g + 1) * gw]
        v = v * lax.rsqrt(jnp.mean(v * v, -1, keepdims=True) + RMS_EPS)
        y_ref[:, g * gw:(g + 1) * gw] = v * ng_ref[:, g * gw:(g + 1) * gw]

    @pl.when(ci == pl.num_programs(1) - 1)
    def _():
        for g, j in pairs:
            h = g * SSD_HPG + 2 * j
            both = st_ref[g, :, two * j:two * (j + 1)].T
            sfin_ref[h] = both[:SSD_HEAD_DIM]
            sfin_ref[h + 1] = both[SSD_HEAD_DIM:]


def _ssd_core(xbc, dt, zs, hist, s0, cw, cb, a_log, d_skip, norm_g, q, valid):
    nb, s, cd = xbc.shape
    assert s % q == 0
    zero_init = s0 is None
    blk = lambda w: pl.BlockSpec((None, q, w), lambda b, c: (b, c, 0))
    st_spec = pl.BlockSpec((None, SSD_HEADS, SSD_HEAD_DIM, SSD_STATE), lambda b, c: (b, 0, 0, 0))
    in_specs = [blk(cd), blk(SSD_DT_PAD), blk(SSD_INNER)]
    args = [xbc, dt, zs]
    if not zero_init:
        in_specs += [pl.BlockSpec((None, 8, cd), lambda b, c: (b, 0, 0)), st_spec]
        args += [hist, s0]
    in_specs += [_const_spec((CONV_W, cd)), _const_spec((1, cd)), _const_spec((1, SSD_DT_PAD)),
                 _const_spec((1, SSD_INNER)), _const_spec((1, SSD_INNER)), _const_spec((SSD_DT_PAD, SSD_INNER))]
    pad_h = SSD_DT_PAD - SSD_HEADS
    head_to_channels = (jnp.arange(SSD_DT_PAD, dtype=jnp.int32)[:, None]
                        == jnp.arange(SSD_INNER, dtype=jnp.int32)[None, :] // SSD_HEAD_DIM).astype(bf16)
    args += [cw, cb.reshape(1, cd), jnp.pad(a_log, (0, pad_h)).reshape(1, SSD_DT_PAD),
             jnp.repeat(d_skip, SSD_HEAD_DIM).reshape(1, SSD_INNER), norm_g.reshape(1, SSD_INNER),
             head_to_channels]
    return pl.pallas_call(
        functools.partial(_ssd_core_kernel, q=q, valid=valid, zero_init=zero_init),
        grid=(nb, s // q), in_specs=in_specs,
        out_specs=[blk(SSD_INNER), st_spec],
        out_shape=[jax.ShapeDtypeStruct((nb, s, SSD_INNER), f32),
                   jax.ShapeDtypeStruct((nb, SSD_HEADS, SSD_HEAD_DIM, SSD_STATE), f32)],
        scratch_shapes=[pltpu.VMEM((q + 8, cd), f32),
                        pltpu.VMEM((SSD_GROUPS, SSD_STATE, SSD_HPG * SSD_HEAD_DIM), f32),
                        pltpu.VMEM((q, SSD_INNER), f32)],
        compiler_params=_cparams(("parallel", "arbitrary")), name="ssd_core",
    )(*args)


def _silu_ep(y, *_):
    return y * jax.nn.sigmoid(y)


def _dt_ep(y, bias):
    return _softplus(y + bias)


def _pass_ep(y, *_):
    return y


def _ssd_layer(xp, xs, nbp, nbs, conv_s, st_s, ln_g, ln_b, w_in, cw, cb, dt_bias, a_log, d_skip, norm_g, w_out):
    cd = SSD_CONV_DIM
    pad_h = SSD_DT_PAD - SSD_HEADS
    w_z = w_in[:, :SSD_INNER].astype(bf16)
    w_xbc = w_in[:, SSD_INNER:SSD_INNER + cd].astype(bf16)
    w_dt = jnp.pad(w_in[:, SSD_INNER + cd:], ((0, 0), (0, pad_h))).astype(bf16)
    bias = jnp.pad(dt_bias, (0, pad_h)).reshape(1, SSD_DT_PAD)
    wo = w_out.astype(bf16)
    sp_len = xp.shape[0] // nbp
    ss_len = xs.shape[0] // nbs
    eps = [_silu_ep, _pass_ep, _dt_ep]

    zsp, xbcp, dtp = _proj(xp, [w_z, w_xbc, w_dt], eps, aux=(bias,), aux_period=(0,), tm=256, name="ssd_in")
    zss, xbcs, dts = _proj(xs, [w_z, w_xbc, w_dt], eps, aux=(bias,), aux_period=(0,), tm=xs.shape[0], name="ssd_in_s")
    xbcp3 = xbcp.reshape(nbp, sp_len, cd)
    yp, sp_fin = _ssd_core(xbcp3, dtp.reshape(nbp, sp_len, SSD_DT_PAD), zsp.reshape(nbp, sp_len, SSD_INNER),
                           None, None, cw, cb, a_log, d_skip, norm_g, q=math.gcd(sp_len, SSD_CHUNK), valid=SSD_CHUNK)
    qs = 8
    assert ss_len <= qs
    tpad = lambda a: jnp.pad(a, ((0, 0), (0, qs - ss_len), (0, 0)))
    xbcs3 = xbcs.reshape(nbs, ss_len, cd)
    hist = jnp.pad(conv_s, ((0, 0), (8 - (CONV_W - 1), 0), (0, 0)))
    ys, ss_fin = _ssd_core(tpad(xbcs3), tpad(dts.reshape(nbs, ss_len, SSD_DT_PAD)),
                           tpad(zss.reshape(nbs, ss_len, SSD_INNER)), hist, st_s,
                           cw, cb, a_log, d_skip, norm_g, q=qs, valid=ss_len)
    ys = ys[:, :ss_len].reshape(nbs * ss_len, SSD_INNER)
    xp = _outproj_ln(yp.reshape(nbp * sp_len, SSD_INNER), wo, xp, ln_g, ln_b, name="ssd_out")
    xs = _outproj_ln(ys, wo, xs, ln_g, ln_b, tm=xs.shape[0], name="ssd_out_s")
    conv_p = xbcp3[:, sp_len - (CONV_W - 1):]
    conv_sn = jnp.concatenate([conv_s, xbcs3], axis=1)[:, ss_len:]
    return xp, xs, (conv_p, conv_sn, sp_fin, ss_fin)


NSA_TILE = 128
NSA_SCALE = NSA_HD ** -0.5


def _rope_tables(pos):
    inv = ROPE_THETA ** (-jnp.arange(0, NSA_HD, 2, dtype=f32) / NSA_HD)
    ang = pos[:, None] * inv[None, :]
    cos, sin = jnp.cos(ang), jnp.sin(ang)
    return jnp.tile(cos, (1, 4)), jnp.tile(jnp.concatenate([-sin, sin], axis=1), (1, 2))


def _rope_cols(y, cos, sgn_sin, blocks):
    lane = lax.broadcasted_iota(jnp.int32, (y.shape[0], 128), 1)
    first_half = (lane % NSA_HD) < NSA_HD // 2
    out = []
    for c in range(y.shape[1] // 128):
        blk = y[:, 128 * c:128 * (c + 1)]
        if c in blocks:
            partner = jnp.where(first_half, pltpu.roll(blk, 128 - NSA_HD // 2, 1), pltpu.roll(blk, NSA_HD // 2, 1))
            blk = blk * cos + partner * sgn_sin
        out.append(blk)
    return jnp.concatenate(out, axis=1)


def _q_ep(y, cos, sin):
    return _rope_cols(y, cos, sin, range(NSA_HQ // 128)) * NSA_SCALE


def _rows_ep(y, cos, sin):
    return _rope_cols(y, cos, sin, (4, 5))


def _wrows_ep(y, cos, sin):
    return _rope_cols(y, cos, sin, (0, 1))


def _sigmoid_ep(y, *_):
    return jax.nn.sigmoid(y)


def _nsa_cmp_kernel(pt_ref, pa_ref, pb_ref, cos_ref, sin_ref, o_ref):
    del pt_ref
    per_page = NSA_TILE // CMP_BLOCK
    for n, ref in enumerate((pa_ref, pb_ref)):
        m = ref[...].reshape(per_page, CMP_BLOCK, ref.shape[1]).sum(1) * (1.0 / CMP_BLOCK)
        lo, hi = n * per_page, (n + 1) * per_page
        o_ref[lo:hi, :] = _rope_cols(m, cos_ref[lo:hi, :], sin_ref[lo:hi, :], (0, 1))


def _nsa_cmp(pool, page_table, cos, sin):
    nb, n_pages = page_table.shape
    assert n_pages % 2 == 0
    w = 2 * NSA_HK
    per_step = 2 * NSA_TILE // CMP_BLOCK
    grid_spec = pltpu.PrefetchScalarGridSpec(
        num_scalar_prefetch=1, grid=(nb, n_pages // 2),
        in_specs=[pl.BlockSpec((None, NSA_TILE, w), lambda b, p, pt: (pt[b, 2 * p], 0, 0)),
                  pl.BlockSpec((None, NSA_TILE, w), lambda b, p, pt: (pt[b, 2 * p + 1], 0, 0)),
                  pl.BlockSpec((per_step, 128), lambda b, p, pt: (p, 0)),
                  pl.BlockSpec((per_step, 128), lambda b, p, pt: (p, 0))],
        out_specs=pl.BlockSpec((None, per_step, w), lambda b, p, pt: (b, p, 0)))
    return pl.pallas_call(
        _nsa_cmp_kernel, grid_spec=grid_spec,
        out_shape=jax.ShapeDtypeStruct((nb, n_pages * NSA_TILE // CMP_BLOCK, w), f32),
        compiler_params=_cparams(("parallel", "arbitrary")), name="nsa_cmp",
    )(page_table, pool, pool, cos, sin)


def _even_odd(cmp):
    nb, nc, w = cmp.shape
    return cmp.reshape(nb, nc // 2, 2, w).transpose(0, 2, 1, 3).reshape(nb, nc, w)


def _topk_mask_rows(score_t, k):
    n = score_t.shape[0]
    assert n % 8 == 0
    pieces = [score_t[8 * v:8 * v + 8] for v in range(n // 8)]
    ridx = lax.broadcasted_iota(jnp.int32, pieces[0].shape, 0)
    cnts = [jnp.zeros(pieces[0].shape, f32) for _ in pieces]
    for i in range(n):
        si = score_t[i:i + 1, :]
        for v, pc in enumerate(pieces):
            if 8 * v > i:
                cnts[v] = cnts[v] + jnp.where(si >= pc, 1.0, 0.0)
            elif 8 * v + 7 < i:
                cnts[v] = cnts[v] + jnp.where(si > pc, 1.0, 0.0)
            else:
                tie_wins = jnp.where(ridx + 8 * v > i, 1.0, 0.0)
                cnts[v] = cnts[v] + jnp.where(si > pc, 1.0, 0.0) + jnp.where(si == pc, tie_wins, 0.0)
    return jnp.concatenate([jnp.where(c < k, 1.0, 0.0) for c in cnts], axis=0)


def _topk_mask_lanes(score, k, n_real):
    lane = lax.broadcasted_iota(jnp.int32, score.shape, 1)
    cnt = jnp.zeros(score.shape, f32)
    for i in range(n_real):
        si = score[:, i:i + 1]
        cnt = cnt + jnp.where(si > score, 1.0, 0.0) + jnp.where(si == score, jnp.where(lane > i, 1.0, 0.0), 0.0)
    return jnp.where(cnt < k, 1.0, 0.0)


def _softmax_rows(s):
    e = jnp.exp(s - jnp.max(s, -1, keepdims=True))
    return e / jnp.sum(e, -1, keepdims=True)


def _flash_step(qk, kt, vt, ok, carry):
    m, l, acc = carry
    s = lax.dot_general(qk, kt, (((1,), (1,)), ((), ())), preferred_element_type=f32)
    s = jnp.where(ok, s, NEG_INF)
    m_new = jnp.maximum(m, jnp.max(s, -1, keepdims=True))
    alpha = jnp.exp(m - m_new)
    p = jnp.exp(s - m_new)
    l = alpha * l + jnp.sum(p, -1, keepdims=True)
    acc = alpha * acc + jnp.dot(p.astype(bf16), vt, preferred_element_type=f32)
    return m_new, l, acc


def _flash_init(rows):
    return (jnp.full((rows, 1), NEG_INF, f32), jnp.zeros((rows, 1), f32), jnp.zeros((rows, NSA_HD), f32))


def _rope_rows(y, cos_t, sin_t, heads):
    half = NSA_HD // 2
    out = []
    for h in range(heads):
        top = y[NSA_HD * h:NSA_HD * h + half]
        bot = y[NSA_HD * h + half:NSA_HD * (h + 1)]
        out += [top * cos_t - bot * sin_t, bot * cos_t + top * sin_t]
    return jnp.concatenate(out, axis=0)


def _nsa_in_kernel(x_ref, wn_ref, wq_ref, wr_ref, ww_ref, wg_ref, cos_ref, sin_ref, cost_ref, sint_ref,
                   nat_ref, qt_ref, rt_ref, wt_ref, gt_ref):
    x = x_ref[...].astype(bf16)
    nat = jnp.dot(x, wn_ref[...], preferred_element_type=f32)
    nat_ref[...] = _rope_cols(nat, cos_ref[...], sin_ref[...], (4, 5, 6, 7))
    nt = (((1,), (1,)), ((), ()))
    cos_t, sin_t = cost_ref[...], sint_ref[...]
    q_t = lax.dot_general(wq_ref[...], x, nt, preferred_element_type=f32)
    qt_ref[...] = _rope_rows(q_t, cos_t, sin_t, NSA_HEADS) * NSA_SCALE
    r_t = lax.dot_general(wr_ref[...], x, nt, preferred_element_type=f32)
    rt_ref[0:2 * NSA_HK, :] = r_t[0:2 * NSA_HK]
    rt_ref[2 * NSA_HK:3 * NSA_HK, :] = _rope_rows(r_t[2 * NSA_HK:3 * NSA_HK], cos_t, sin_t, NSA_KV_HEADS)
    rt_ref[3 * NSA_HK:4 * NSA_HK, :] = r_t[3 * NSA_HK:4 * NSA_HK]
    w_t = lax.dot_general(ww_ref[...], x, nt, preferred_element_type=f32)
    wt_ref[0:NSA_HK, :] = _rope_rows(w_t[0:NSA_HK], cos_t, sin_t, NSA_KV_HEADS)
    wt_ref[NSA_HK:2 * NSA_HK, :] = w_t[NSA_HK:2 * NSA_HK]
    gt_ref[...] = jax.nn.sigmoid(lax.dot_general(wg_ref[...], x, nt, preferred_element_type=f32))


def _nsa_in(x, nb, w_in, tm=256):
    m, d = x.shape
    s_len = m // nb
    assert s_len % tm == 0
    hq, hk = NSA_HQ, NSA_HK
    w_q = w_in[:, :hq]
    w_kv = w_in[:, hq:hq + 6 * hk].reshape(d, 6, hk)
    w_g = w_in[:, hq + 6 * hk:]
    w_nat = w_kv[:, jnp.array([0, 1, 2, 4])].reshape(d, 4 * hk).astype(bf16)
    w_q_t = w_q.T.astype(bf16)
    w_rows_t = w_kv[:, 0:4].reshape(d, 4 * hk).T.astype(bf16)
    w_win_t = w_kv[:, 4:6].reshape(d, 2 * hk).T.astype(bf16)
    w_g_t = w_g.T.astype(bf16)
    pos = jnp.arange(s_len, dtype=f32)
    cos, sin = _rope_tables(pos)
    inv = ROPE_THETA ** (-jnp.arange(0, NSA_HD, 2, dtype=f32) / NSA_HD)
    ang_t = inv[:, None] * pos[None, :]
    nt = s_len // tm
    ng = 3 * NSA_HEADS
    t_spec = lambda r: pl.BlockSpec((None, r, tm), lambda b, i: (b, 0, i))
    tab = pl.BlockSpec((tm, 128), lambda b, i: (i, 0))
    tab_t = pl.BlockSpec((NSA_HD // 2, tm), lambda b, i: (0, i))
    return pl.pallas_call(
        _nsa_in_kernel, grid=(nb, nt),
        in_specs=[pl.BlockSpec((tm, d), lambda b, i: (b * nt + i, 0)),
                  _const_spec(w_nat.shape), _const_spec(w_q_t.shape), _const_spec(w_rows_t.shape),
                  _const_spec(w_win_t.shape), _const_spec(w_g_t.shape), tab, tab, tab_t, tab_t],
        out_specs=[pl.BlockSpec((tm, 4 * hk), lambda b, i: (b * nt + i, 0)), t_spec(hq), t_spec(4 * hk),
                   t_spec(2 * hk), t_spec(ng)],
        out_shape=[jax.ShapeDtypeStruct((m, 4 * hk), f32), jax.ShapeDtypeStruct((nb, hq, s_len), f32),
                   jax.ShapeDtypeStruct((nb, 4 * hk, s_len), f32), jax.ShapeDtypeStruct((nb, 2 * hk, s_len), f32),
                   jax.ShapeDtypeStruct((nb, ng, s_len), f32)],
        compiler_params=_cparams(("parallel", "parallel")), name="nsa_in",
    )(x, w_nat, w_q_t, w_rows_t, w_win_t, w_g_t, cos, sin, jnp.cos(ang_t), jnp.sin(ang_t))


def _flash_cols(k_tile, v_t, q_ref, bias, m_ref, l_ref, acc_ref):
    qw = NSA_TILE
    for g in range(q_ref.shape[1] // qw):
        c = slice(g * qw, (g + 1) * qw)
        s = jnp.dot(k_tile, q_ref[:, c], preferred_element_type=f32)
        if bias is not None:
            s = s + bias
        m_old = m_ref[:, c]
        m_new = jnp.maximum(m_old, jnp.max(s, 0, keepdims=True))
        alpha = jnp.exp(m_old - m_new)
        p = jnp.exp(s - m_new)
        l_ref[:, c] = alpha * l_ref[:, c] + jnp.sum(p, 0, keepdims=True)
        m_ref[:, c] = m_new
        acc_ref[:, c] = alpha * acc_ref[:, c] + jnp.dot(v_t, p.astype(bf16), preferred_element_type=f32)


def _nsa_seq_kernel(qt_ref, gt_ref, kcb_ref, vcbt_ref, knat_ref, vst_ref, vwt_ref, o_ref,
                    q_sc, oc_sc, os_sc, m_sc, l_sc, acc_sc, *, s_len):
    i = pl.program_id(1)
    qb = NSA_TILE
    nc = s_len // CMP_BLOCK
    nsb = s_len // SEL_BLOCK
    cols = NSA_GROUP * qb
    tile4 = lambda a: jnp.concatenate([a] * NSA_GROUP, axis=1)
    t_q = i * qb + lax.broadcasted_iota(jnp.int32, (1, qb), 1)
    t_cols = tile4(t_q)
    cl = lax.broadcasted_iota(jnp.int32, (nc, 1), 0)
    cblk = jnp.where(cl < nc // 2, 2 * cl, 2 * (cl - nc // 2) + 1)
    cmask = (cblk + 1) * CMP_BLOCK - 1 <= t_cols
    has_cmp = jnp.where(t_cols >= CMP_BLOCK - 1, 1.0, 0.0)
    blk = lax.broadcasted_iota(jnp.int32, (nsb, 1), 0)
    valid = blk * SEL_BLOCK <= t_q
    forced = (blk == 0) | (blk == t_q // SEL_BLOCK)
    key_in = lax.broadcasted_iota(jnp.int32, (NSA_TILE, 1), 0)
    blk_of_key = lax.broadcasted_iota(jnp.int32, (NSA_TILE, 2 * NSA_HD), 0) // SEL_BLOCK
    lane = lax.broadcasted_iota(jnp.int32, (NSA_TILE, 2 * NSA_HD), 1)
    own_lanes = [lane < NSA_HD, lane >= NSA_HD]
    gates = gt_ref[...]
    kcols = [slice(NSA_HD * k, NSA_HD * (k + 1)) for k in range(NSA_KV_HEADS)]
    wcols = [slice(NSA_HK + NSA_HD * k, NSA_HK + NSA_HD * (k + 1)) for k in range(NSA_KV_HEADS)]

    def reset():
        m_sc[...] = jnp.full(m_sc.shape, NEG_INF, f32)
        l_sc[...] = jnp.zeros_like(l_sc)
        acc_sc[...] = jnp.zeros_like(acc_sc)

    for k in range(NSA_KV_HEADS):
        kcol = kcols[k]
        q_t = jnp.concatenate([qt_ref[NSA_HD * (NSA_GROUP * k + g):NSA_HD * (NSA_GROUP * k + g + 1), :]
                               for g in range(NSA_GROUP)], axis=1).astype(bf16)

        s_c = jnp.dot(kcb_ref[:, kcol].astype(bf16), q_t, preferred_element_type=f32)
        s_c = jnp.where(cmask, s_c, NEG_INF)
        e = jnp.exp(s_c - jnp.max(s_c, 0, keepdims=True))
        p_c = e / jnp.sum(e, 0, keepdims=True) * has_cmp
        o_c = jnp.dot(vcbt_ref[kcol, :].astype(bf16), p_c.astype(bf16), preferred_element_type=f32)

        imp = p_c[:, 0:qb]
        for g in range(1, NSA_GROUP):
            imp = imp + p_c[:, g * qb:(g + 1) * qb]
        imp = imp[:nc // 2] + imp[nc // 2:]
        score = jnp.where(forced, FORCE_SCORE, jnp.where(valid, imp, -1.0))
        sel = _topk_mask_rows(score, SEL_TOPK)
        sel_bias = tile4(jnp.where(sel > 0.5, 0.0, NEG_INF)).astype(bf16)
        fill = jnp.zeros((NSA_HD - nsb, cols), bf16)
        q_sc[k] = jnp.concatenate([q_t, sel_bias, fill] if k % 2 == 0 else [sel_bias, fill, q_t], axis=0)
        oc_sc[k] = o_c

    def sel_tile(j, bias):
        off = pl.multiple_of(j * NSA_TILE, NSA_TILE)
        keys = pl.ds(off, NSA_TILE)
        blk_key = blk_of_key + j * (NSA_TILE // SEL_BLOCK)
        onehot = [jnp.where(lane - NSA_HD == blk_key, 1.0, 0.0).astype(bf16),
                  jnp.where(lane == blk_key, 1.0, 0.0).astype(bf16)]
        for pair in range(NSA_KV_HEADS // 2):
            k_both = knat_ref[keys, 2 * NSA_HD * pair:2 * NSA_HD * (pair + 1)].astype(bf16)
            for own in range(2):
                k = 2 * pair + own
                k_aug = jnp.where(own_lanes[own], k_both, onehot[own])
                _flash_cols(k_aug, vst_ref[kcols[k], keys].astype(bf16), q_sc.at[k], bias,
                            m_sc.at[k], l_sc.at[k], acc_sc.at[k])

    def win_tile(j, bias):
        off = pl.multiple_of(j * NSA_TILE, NSA_TILE)
        keys = pl.ds(off, NSA_TILE)
        for k in range(NSA_KV_HEADS):
            _flash_cols(knat_ref[keys, wcols[k]].astype(bf16), vwt_ref[kcols[k], keys].astype(bf16),
                        q_sc.at[k, pl.ds(NSA_HD * (k % 2), NSA_HD)], bias, m_sc.at[k], l_sc.at[k], acc_sc.at[k])

    def causal_bias(j):
        return jnp.where(j * NSA_TILE + key_in <= t_q, 0.0, NEG_INF)

    def window_bias(j):
        kpos = j * NSA_TILE + key_in
        return jnp.where((kpos <= t_q) & (kpos > t_q - WINDOW), 0.0, NEG_INF)

    def loop(tile_fn, lo, hi):
        def body(j, carry):
            tile_fn(j, None)
            return carry
        lax.fori_loop(lo, hi, body, 0)

    reset()
    loop(sel_tile, 0, i)
    sel_tile(i, causal_bias(i))
    os_sc[...] = acc_sc[...] / l_sc[...]

    reset()
    n_back = WINDOW // NSA_TILE

    @pl.when(i >= n_back)
    def _():
        win_tile(i - n_back, window_bias(i - n_back))

    loop(win_tile, jnp.maximum(i - n_back + 1, 0), i)
    win_tile(i, window_bias(i))

    for k in range(NSA_KV_HEADS):
        o_c = oc_sc[k]
        o_s = os_sc[k]
        o_w = acc_sc[k] / l_sc[k]
        for pair in range(NSA_GROUP // 2):
            o_t = []
            for g in (2 * pair, 2 * pair + 1):
                h = NSA_GROUP * k + g
                c = slice(g * qb, (g + 1) * qb)
                o_t.append(gates[3 * h:3 * h + 1, :] * o_c[:, c] + gates[3 * h + 1:3 * h + 2, :] * o_s[:, c]
                           + gates[3 * h + 2:3 * h + 3, :] * o_w[:, c])
            lane0 = NSA_HD * (NSA_GROUP * k + 2 * pair)
            o_ref[:, lane0:lane0 + 2 * NSA_HD] = jnp.concatenate(o_t, axis=0).T


def _nsa_seq(q_t, gates_t, kcb, vcb_t, nat, rows_t, wrows_t):
    nb, _, s_len = q_t.shape
    assert s_len % NSA_TILE == 0 and (s_len // CMP_BLOCK) % 2 == 0 and (s_len // SEL_BLOCK) % 8 == 0
    assert s_len // SEL_BLOCK <= NSA_HD and 2 * NSA_HD == NSA_TILE
    nc = s_len // CMP_BLOCK
    hk = NSA_HK
    cols = NSA_GROUP * NSA_TILE
    per_b = lambda shape, idx: pl.BlockSpec((None,) + shape, lambda b, i: (b,) + idx)
    return pl.pallas_call(
        functools.partial(_nsa_seq_kernel, s_len=s_len), grid=(nb, s_len // NSA_TILE),
        in_specs=[pl.BlockSpec((None, NSA_HQ, NSA_TILE), lambda b, i: (b, 0, i)),
                  pl.BlockSpec((None, 3 * NSA_HEADS, NSA_TILE), lambda b, i: (b, 0, i)),
                  per_b((nc, hk), (0, 0)), per_b((hk, nc), (0, 0)),
                  per_b((s_len, 2 * hk), (0, 1)),
                  per_b((hk, s_len), (3, 0)),
                  per_b((hk, s_len), (1, 0))],
        out_specs=pl.BlockSpec((None, NSA_TILE, NSA_HQ), lambda b, i: (b, i, 0)),
        out_shape=jax.ShapeDtypeStruct((nb, s_len, NSA_HQ), f32),
        scratch_shapes=[pltpu.VMEM((NSA_KV_HEADS, 2 * NSA_HD, cols), bf16),
                        pltpu.VMEM((NSA_KV_HEADS, NSA_HD, cols), f32), pltpu.VMEM((NSA_KV_HEADS, NSA_HD, cols), f32),
                        pltpu.VMEM((NSA_KV_HEADS, 1, cols), f32), pltpu.VMEM((NSA_KV_HEADS, 1, cols), f32),
                        pltpu.VMEM((NSA_KV_HEADS, NSA_HD, cols), f32)],
        compiler_params=_cparams(("parallel", "arbitrary")), name="nsa_seq",
    )(q_t, gates_t, kcb, vcb_t, nat, rows_t, wrows_t)


DEC_CHUNK_PAGES = 8


def _flash_step_t(qk, k_t, v_t, ok, carry):
    m, l, acc = carry
    s = jnp.dot(qk, k_t, preferred_element_type=f32)
    s = jnp.where(ok, s, NEG_INF)
    m_new = jnp.maximum(m, jnp.max(s, -1, keepdims=True))
    alpha = jnp.exp(m - m_new)
    p = jnp.exp(s - m_new)
    l = alpha * l + jnp.sum(p, -1, keepdims=True)
    acc = alpha * acc + lax.dot_general(p.astype(bf16), v_t, (((1,), (1,)), ((), ())), preferred_element_type=f32)
    return m_new, l, acc


def _nsa_dec_kernel(pt_ref, q_ref, g_ref, cos_ref, sin_ref, avg_ref, pool_hbm, new_ref, win_ref, wnew_ref, o_ref,
                    buf, sems, cmp_sc, bias_sc, m_sc, l_sc, acc_sc, *, past_len, n_new):
    b = pl.program_id(0)
    rows = NSA_GROUP * n_new
    t_len = past_len + n_new
    nc = t_len // CMP_BLOCK
    nsb = -(-t_len // SEL_BLOCK)
    n_pages = past_len // NSA_TILE
    cpg = DEC_CHUNK_PAGES
    n_chunks = n_pages // cpg
    ckeys = cpg * NSA_TILE
    new_pad = new_ref.shape[0]

    def page_copy(c, i):
        half = c // n_chunks
        page = pt_ref[b, (c % n_chunks) * cpg + i]
        return pltpu.make_async_copy(pool_hbm.at[page, half], buf.at[c % 2, :, pl.ds(i * NSA_TILE, NSA_TILE)],
                                     sems.at[c % 2])

    def start_chunk(c):
        for i in range(cpg):
            page_copy(c, i).start()

    def wait_chunk(c):
        for i in range(cpg):
            page_copy(c, i).wait()

    start_chunk(0)
    cmp_sc[...] = jnp.zeros_like(cmp_sc)

    def pass1(c, carry):
        start_chunk(c + 1)
        wait_chunk(c)
        x = buf[c % 2]
        avg = avg_ref[c]
        hi = x.astype(bf16)
        r1 = x - hi.astype(f32)
        mid = r1.astype(bf16)
        lo = (r1 - mid.astype(f32)).astype(bf16)
        cmp_sc[...] += (jnp.dot(hi, avg, preferred_element_type=f32) + jnp.dot(mid, avg, preferred_element_type=f32)
                        + jnp.dot(lo, avg, preferred_element_type=f32))
        return carry

    lax.fori_loop(0, n_chunks, pass1, 0)

    nr = NSA_KV_HEADS * rows
    rid = lax.broadcasted_iota(jnp.int32, (nr, 1), 0)
    t_all = past_len + rid % n_new
    zero_q = jnp.zeros((rows, NSA_HD), bf16)
    q_bd = jnp.concatenate(
        [jnp.concatenate([q_ref[k].astype(bf16) if kk == k else zero_q for kk in range(NSA_KV_HEADS)], axis=1)
         for k in range(NSA_KV_HEADS)], axis=0)
    nt = (((1,), (1,)), ((), ()))

    cl = lax.broadcasted_iota(jnp.int32, (1, nc), 1)
    cblk = jnp.where(cl < nc // 2, 2 * cl, 2 * (cl - nc // 2) + 1)
    cmask = (cblk + 1) * CMP_BLOCK - 1 <= t_all
    rr = lax.broadcasted_iota(jnp.int32, (nr, nr), 0)
    rc = lax.broadcasted_iota(jnp.int32, (nr, nr), 1)
    group_sum = jnp.where((rr % n_new == rc % n_new) & (rr // rows == rc // rows), 1.0, 0.0)
    blk = lax.broadcasted_iota(jnp.int32, (nr, nc), 1)
    valid = blk * SEL_BLOCK <= t_all
    forced = (blk == 0) | (blk == t_all // SEL_BLOCK)
    cos, sin = cos_ref[...], sin_ref[...]
    kcb_t = _rope_rows(cmp_sc[0:NSA_HK, :], cos, sin, NSA_KV_HEADS).astype(bf16)
    vcb_t = cmp_sc[NSA_HK:2 * NSA_HK, :].astype(bf16)
    s_c = jnp.dot(q_bd, kcb_t, preferred_element_type=f32)
    p_c = _softmax_rows(jnp.where(cmask, s_c, NEG_INF))
    p_c = p_c * jnp.where(t_all >= CMP_BLOCK - 1, 1.0, 0.0)
    o_c = lax.dot_general(p_c.astype(bf16), vcb_t, nt, preferred_element_type=f32)
    imp = jnp.dot(group_sum, p_c, precision=lax.Precision.HIGHEST, preferred_element_type=f32)
    imp = imp[:, :nc // 2] + imp[:, nc // 2:]
    imp = jnp.concatenate([imp, jnp.zeros((nr, nc - nc // 2), f32)], axis=1)
    score = jnp.where(forced, FORCE_SCORE, jnp.where(valid, imp, -1.0))
    score = jnp.where(blk < nsb, score, -2.0)
    sel = _topk_mask_lanes(score, SEL_TOPK, nsb)
    bias_sc[...] = jnp.where(sel > 0.5, 0.0, NEG_INF).astype(bf16)
    in_last = sel[:, nsb - 1:nsb] > 0.5
    m_sc[...] = jnp.full(m_sc.shape, NEG_INF, f32)
    l_sc[...] = jnp.zeros_like(l_sc)
    acc_sc[...] = jnp.zeros_like(acc_sc)

    def flash(s, v, v_transposed):
        m_new = jnp.maximum(m_sc[...], jnp.max(s, -1, keepdims=True))
        alpha = jnp.exp(m_sc[...] - m_new)
        p = jnp.exp(s - m_new)
        l_sc[...] = alpha * l_sc[...] + jnp.sum(p, -1, keepdims=True)
        m_sc[...] = m_new
        if v_transposed:
            pv = lax.dot_general(p.astype(bf16), v, nt, preferred_element_type=f32)
        else:
            pv = jnp.dot(p.astype(bf16), v, preferred_element_type=f32)
        acc_sc[...] = alpha * acc_sc[...] + pv

    def pass2(c, carry):
        @pl.when(c + 1 < 2 * n_chunks)
        def _():
            start_chunk(c + 1)

        wait_chunk(c)
        key0 = (c - n_chunks) * ckeys
        eb = lax.broadcasted_iota(jnp.int32, (nc, ckeys), 0)
        ek = (key0 + lax.broadcasted_iota(jnp.int32, (nc, ckeys), 1)) // SEL_BLOCK
        onehot = jnp.where(eb == ek, 1.0, 0.0).astype(bf16)
        s = (jnp.dot(q_bd, buf[c % 2, 0:NSA_HK, :].astype(bf16), preferred_element_type=f32)
             + jnp.dot(bias_sc[...], onehot, preferred_element_type=f32))
        flash(s, buf[c % 2, NSA_HK:2 * NSA_HK, :].astype(bf16), True)
        return carry

    lax.fori_loop(n_chunks, 2 * n_chunks, pass2, 0)

    jn = lax.broadcasted_iota(jnp.int32, (1, new_pad), 1)
    npos = past_len + jn
    wpos = past_len - WINDOW + lax.broadcasted_iota(jnp.int32, (1, WINDOW), 1)
    ok = in_last & (npos <= t_all) & (jn < n_new)
    s = lax.dot_general(q_bd, new_ref[:, 0:NSA_HK].astype(bf16), nt, preferred_element_type=f32)
    flash(jnp.where(ok, s, NEG_INF), new_ref[:, NSA_HK:2 * NSA_HK].astype(bf16), False)
    o_s = acc_sc[...] / l_sc[...]

    m_sc[...] = jnp.full(m_sc.shape, NEG_INF, f32)
    l_sc[...] = jnp.zeros_like(l_sc)
    acc_sc[...] = jnp.zeros_like(acc_sc)
    ok = (wpos <= t_all) & (wpos > t_all - WINDOW)
    s = jnp.dot(q_bd, win_ref[0:NSA_HK, :].astype(bf16), preferred_element_type=f32)
    flash(jnp.where(ok, s, NEG_INF), win_ref[NSA_HK:2 * NSA_HK, :].astype(bf16), True)
    ok = (npos <= t_all) & (npos > t_all - WINDOW) & (jn < n_new)
    s = lax.dot_general(q_bd, wnew_ref[:, 0:NSA_HK].astype(bf16), nt, preferred_element_type=f32)
    flash(jnp.where(ok, s, NEG_INF), wnew_ref[:, NSA_HK:2 * NSA_HK].astype(bf16), False)
    o_w = acc_sc[...] / l_sc[...]

    g = jnp.concatenate([g_ref[k] for k in range(NSA_KV_HEADS)], axis=0)
    o = g[:, 0:1] * o_c + g[:, 1:2] * o_s + g[:, 2:3] * o_w
    for k in range(NSA_KV_HEADS):
        o_ref[k] = o[rows * k:rows * (k + 1), NSA_HD * k:NSA_HD * (k + 1)]


def _nsa_dec(q, gates, cos_t, sin_t, pool_v, page_table, new_rows, win_t, new_wrows, past_len, n_new):
    nb, n_pages = page_table.shape
    rows = NSA_GROUP * n_new
    w = 2 * NSA_HK
    nc = (past_len + n_new) // CMP_BLOCK
    assert past_len == n_pages * NSA_TILE and past_len % SEL_BLOCK == 0 and n_new <= CMP_BLOCK
    assert win_t.shape[2] == WINDOW and past_len >= WINDOW and n_pages % DEC_CHUNK_PAGES == 0
    assert nc == past_len // CMP_BLOCK and nc % 2 == 0 and -(-(past_len + n_new) // SEL_BLOCK) <= nc
    assert pool_v.shape[1:] == (2, w, NSA_TILE) and cos_t.shape == (NSA_HD // 2, nc)
    ckeys = DEC_CHUNK_PAGES * NSA_TILE
    blk_of_key = jnp.arange(n_pages * NSA_TILE, dtype=jnp.int32).reshape(-1, ckeys, 1) // CMP_BLOCK
    lane_of_blk = jnp.where(blk_of_key % 2 == 0, blk_of_key // 2, nc // 2 + blk_of_key // 2)
    avg = jnp.where(jnp.arange(nc, dtype=jnp.int32) == lane_of_blk, 1.0 / CMP_BLOCK, 0.0).astype(bf16)
    nr = NSA_KV_HEADS * rows
    per_b = lambda shape: pl.BlockSpec((None,) + shape, lambda b, pt: (b,) + (0,) * len(shape))
    const = lambda shape: pl.BlockSpec(shape, lambda b, pt: (0,) * len(shape), pipeline_mode=pl.Buffered(1))
    grid_spec = pltpu.PrefetchScalarGridSpec(
        num_scalar_prefetch=1, grid=(nb,),
        in_specs=[per_b((NSA_KV_HEADS, rows, NSA_HD)), per_b((NSA_KV_HEADS, rows, 3)),
                  const(cos_t.shape), const(sin_t.shape), const(avg.shape), pl.BlockSpec(memory_space=pl.ANY),
                  pl.BlockSpec((None, new_rows.shape[1], w), lambda b, pt: (b, 0, 1)),
                  per_b((w, WINDOW)), per_b((new_wrows.shape[1], w))],
        out_specs=per_b((NSA_KV_HEADS, rows, NSA_HD)),
        scratch_shapes=[pltpu.VMEM((2, w, ckeys), f32), pltpu.SemaphoreType.DMA((2,)),
                        pltpu.VMEM((w, nc), f32), pltpu.VMEM((nr, nc), bf16),
                        pltpu.VMEM((nr, 1), f32), pltpu.VMEM((nr, 1), f32), pltpu.VMEM((nr, NSA_HK), f32)])
    return pl.pallas_call(
        functools.partial(_nsa_dec_kernel, past_len=past_len, n_new=n_new), grid_spec=grid_spec,
        out_shape=jax.ShapeDtypeStruct((nb, NSA_KV_HEADS, rows, NSA_HD), f32),
        compiler_params=_cparams(("arbitrary",)), name="nsa_dec",
    )(page_table, q, gates, cos_t, sin_t, avg, pool_v, new_rows, win_t, new_wrows)


def _nsa_layer(xp, xs, nbp, nbs, pool, win, page_table, ln_g, ln_b, w_in, w_out):
    hq, hk = NSA_HQ, NSA_HK
    ws = [w_in[:, :hq].astype(bf16), w_in[:, hq:hq + 4 * hk].astype(bf16),
          w_in[:, hq + 4 * hk:hq + 6 * hk].astype(bf16), w_in[:, hq + 6 * hk:].astype(bf16)]
    wo = w_out.astype(bf16)
    eps = [_q_ep, _rows_ep, _wrows_ep, _sigmoid_ep]
    sp_len = xp.shape[0] // nbp
    ss_len = xs.shape[0] // nbs
    n_pages = page_table.shape[1]
    page = pool.shape[1]
    assert page == NSA_TILE
    past_len = n_pages * page
    ms = xs.shape[0]

    cos_s, sin_s = _rope_tables(jnp.tile(past_len + jnp.arange(ss_len, dtype=f32), nbs))
    nat_p, q_t, rows_t, wrows_t, g_t = _nsa_in(xp, nbp, w_in)
    qs, rows_s, wrows_s, gs = _proj(xs, ws, eps, aux=(cos_s, sin_s), aux_period=(ms, ms), tm=ms, name="nsa_in_s")

    def cmp_tables(nc):
        return _rope_tables(jnp.arange(nc, dtype=f32) * CMP_BLOCK + (CMP_BLOCK - 1) / 2.0)

    pages_p = sp_len // page
    pt_p = jnp.arange(nbp * pages_p, dtype=jnp.int32).reshape(nbp, pages_p)
    cmp_p = _even_odd(_nsa_cmp(nat_p.reshape(nbp * pages_p, page, 4 * hk), pt_p, *cmp_tables(sp_len // CMP_BLOCK)))
    op = _nsa_seq(q_t, g_t, cmp_p[:, :, :hk], cmp_p[:, :, hk:].transpose(0, 2, 1),
                  nat_p.reshape(nbp, sp_len, 4 * hk), rows_t, wrows_t)

    pool_v = pool.transpose(0, 2, 3, 4, 1).reshape(pool.shape[0], 2, 2 * hk, page)
    win_t = win.transpose(0, 2, 3, 4, 1).reshape(nbs, 2 * hk, win.shape[1])
    nc_s = past_len // CMP_BLOCK
    cpos = jnp.arange(nc_s, dtype=f32) * CMP_BLOCK + (CMP_BLOCK - 1) / 2.0
    cpos = jnp.concatenate([cpos[0::2], cpos[1::2]])
    inv = ROPE_THETA ** (-jnp.arange(0, NSA_HD, 2, dtype=f32) / NSA_HD)
    ang_t = inv[:, None] * cpos[None, :]
    rows_s3 = rows_s.reshape(nbs, ss_len, 4 * hk)
    wrows_s3 = wrows_s.reshape(nbs, ss_len, 2 * hk)
    new_pad = 8
    tpad = lambda a: jnp.pad(a, ((0, 0), (0, new_pad - ss_len), (0, 0)))
    to_heads = lambda a, last: (a.reshape(nbs, ss_len, NSA_KV_HEADS, NSA_GROUP, last).transpose(0, 2, 3, 1, 4)
                                .reshape(nbs, NSA_KV_HEADS, NSA_GROUP * ss_len, last))
    os_ = _nsa_dec(to_heads(qs, NSA_HD), to_heads(gs, 3), jnp.cos(ang_t), jnp.sin(ang_t), pool_v, page_table,
                   tpad(rows_s3), win_t, tpad(wrows_s3), past_len, ss_len)
    os_ = (os_.reshape(nbs, NSA_KV_HEADS, NSA_GROUP, ss_len, NSA_HD).transpose(0, 3, 1, 2, 4).reshape(ms, hq))

    xp = _outproj_ln(op.reshape(nbp * sp_len, hq), wo, xp, ln_g, ln_b, name="nsa_out")
    xs = _outproj_ln(os_, wo, xs, ln_g, ln_b, tm=ms, name="nsa_out_s")
    kv_shape = (4, NSA_KV_HEADS, NSA_HD)
    win_shape = (2, NSA_KV_HEADS, NSA_HD)
    from_t = lambda a, shape: a.reshape((a.shape[0],) + shape + (a.shape[2],)).transpose(0, 4, 1, 2, 3)
    rp = from_t(rows_t, kv_shape)
    rs = rows_s.reshape((nbs, ss_len) + kv_shape)
    wp = from_t(wrows_t[:, :, sp_len - min(WINDOW, sp_len):], win_shape)
    wsn_t = jnp.concatenate([win_t, wrows_s3.transpose(0, 2, 1)], axis=2)
    wsn = from_t(wsn_t[:, :, wsn_t.shape[2] - min(WINDOW, wsn_t.shape[2]):], win_shape)
    return xp, xs, (rp, rs, wp, wsn)


FFN_TF = 256
MOE_TF = 512
ROW_TILE = 1024


def kernel(x_prompt, x_sample, state_l0_lru_conv, state_l0_lru_h, cache_l1_nsa_kv, cache_l1_nsa_win, page_table,
           state_l2_ssd_conv, state_l2_ssd_ssm, state_l3_lru_conv, state_l3_lru_h, ln_g, ln_b, lru_w_in,
           lru_conv_w, lru_conv_b, lru_w_a, lru_b_a, lru_w_x, lru_b_x, lru_lam, lru_w_out, nsa_w_in, nsa_w_out,
           ssd_w_in, ssd_conv_w, ssd_conv_b, ssd_dt_bias, ssd_a_log, ssd_d, ssd_norm_g, ssd_w_out, ffn_w_in,
           ffn_w_out, moe_router_w, moe_router_b, moe_w_in, moe_w_out):
    nbp, sp_len, d = x_prompt.shape
    nbs, ss_len, _ = x_sample.shape
    xp = x_prompt.reshape(nbp * sp_len, d)
    xs = x_sample.reshape(nbs * ss_len, d)
    ms = xs.shape[0]
    lru_state = {0: (state_l0_lru_conv, state_l0_lru_h), 3: (state_l3_lru_conv, state_l3_lru_h)}
    new = {}
    for i in range(DEPTH):
        kind, j = i % 3, i // 3
        g0, b0 = ln_g[i, 0], ln_b[i, 0]
        if kind == 0:
            conv_s, h_s = lru_state[i]
            xp, xs, new[i] = _lru_layer(xp, xs, nbp, nbs, conv_s, h_s, g0, b0, lru_w_in[j], lru_conv_w[j],
                                        lru_conv_b[j], lru_w_a[j], lru_b_a[j], lru_w_x[j], lru_b_x[j],
                                        lru_lam[j], lru_w_out[j])
        elif kind == 1:
            xp, xs, new[i] = _nsa_layer(xp, xs, nbp, nbs, cache_l1_nsa_kv, cache_l1_nsa_win, page_table, g0, b0,
                                        nsa_w_in[j], nsa_w_out[j])
        else:
            xp, xs, new[i] = _ssd_layer(xp, xs, nbp, nbs, state_l2_ssd_conv, state_l2_ssd_ssm, g0, b0,
                                        ssd_w_in[j], ssd_conv_w[j], ssd_conv_b[j], ssd_dt_bias[j], ssd_a_log[j],
                                        ssd_d[j], ssd_norm_g[j], ssd_w_out[j])
        g1, b1 = ln_g[i, 1], ln_b[i, 1]
        k = i // 2
        if i % 2 == 0:
            xp = _ffn_ln(xp, ffn_w_in, ffn_w_out, k, g1, b1, ROW_TILE, FFN_TF, name="ffn")
            xs = _ffn_ln(xs, ffn_w_in, ffn_w_out, k, g1, b1, ms, FFN_TF, name="ffn_s")
        else:
            xp = _moe_sparse_ln(xp, moe_router_w[k], moe_router_b[k], moe_w_in, moe_w_out, k, g1, b1, MOE_TF)
            cs = _router(xs, moe_router_w[k], moe_router_b[k], tm=ms)
            xs = _moe_ln(xs, cs, moe_w_in, moe_w_out, k, g1, b1, ms, MOE_TF)
    out = [xp.reshape(nbp, sp_len, d), xs.reshape(nbs, ss_len, d)]
    for i in range(DEPTH):
        out.extend(new[i])
    return tuple(out)
```

```python
import functools
import math

import jax
import jax.numpy as jnp
from jax import lax
from jax.experimental import pallas as pl
from jax.experimental.pallas import tpu as pltpu

f32 = jnp.float32
bf16 = jnp.bfloat16

D_MODEL = 1024
DEPTH = 4
ALPHA = (2 * DEPTH) ** 0.25
LN_EPS = 1e-5
RMS_EPS = 1e-5
CONV_W = 4
NEG_INF = -1e30

D_RNN = 1344
LRU_BLOCKS = 16
LRU_BS = D_RNN // LRU_BLOCKS
LRU_C = 8.0

NSA_HEADS = 16
NSA_KV_HEADS = 4
NSA_HD = 64
NSA_GROUP = NSA_HEADS // NSA_KV_HEADS
CMP_BLOCK = 32
SEL_BLOCK = 64
SEL_TOPK = 16
WINDOW = 512
FORCE_SCORE = 1e4
ROPE_THETA = 10000.0
NSA_HQ = NSA_HEADS * NSA_HD
NSA_HK = NSA_KV_HEADS * NSA_HD

SSD_INNER = 2 * D_MODEL
SSD_HEAD_DIM = 64
SSD_HEADS = SSD_INNER // SSD_HEAD_DIM
SSD_GROUPS = 4
SSD_STATE = 128
SSD_CHUNK = 128
SSD_CONV_DIM = SSD_INNER + 2 * SSD_GROUPS * SSD_STATE

N_EXPERTS = 8
TOP_K = 2

VMEM_LIMIT_V7X = 56 * 1024 * 1024
PROJ_TM = 512


def _cparams(sem):
    return pltpu.CompilerParams(dimension_semantics=sem, vmem_limit_bytes=VMEM_LIMIT_V7X)


def _const_spec(shape):
    nd = len(shape)
    return pl.BlockSpec(shape, lambda *_: (0,) * nd, pipeline_mode=pl.Buffered(1))


def _layer_norm(v, g, b):
    mu = jnp.mean(v, -1, keepdims=True)
    d = v - mu
    var = jnp.mean(d * d, -1, keepdims=True)
    return d * lax.rsqrt(var + LN_EPS) * g + b


def _proj_kernel(*refs, epilogues, n_aux):
    n_out = len(epilogues)
    x_ref = refs[0]
    w_refs = refs[1:1 + n_out]
    aux_refs = refs[1 + n_out:1 + n_out + n_aux]
    o_refs = refs[1 + n_out + n_aux:]
    x = x_ref[...].astype(bf16)
    aux = [a[...] for a in aux_refs]
    for w_ref, o_ref, ep in zip(w_refs, o_refs, epilogues):
        y = jnp.dot(x, w_ref[...], preferred_element_type=f32)
        if ep is not None:
            y = ep(y, *aux)
        o_ref[...] = y.astype(o_ref.dtype)


def _proj(x, ws, epilogues, aux=(), aux_period=(), tm=PROJ_TM, name="proj"):
    m, k = x.shape
    assert m % tm == 0
    in_specs = [pl.BlockSpec((tm, k), lambda i: (i, 0))]
    in_specs += [_const_spec(w.shape) for w in ws]
    for a, p in zip(aux, aux_period):
        if p:
            assert p % tm == 0 and a.shape[0] == p
            in_specs.append(pl.BlockSpec((tm, a.shape[1]), functools.partial(lambda i, n: (i % n, 0), n=p // tm)))
        else:
            in_specs.append(_const_spec(a.shape))
    out_shape = [jax.ShapeDtypeStruct((m, w.shape[1]), f32) for w in ws]
    out_specs = [pl.BlockSpec((tm, w.shape[1]), lambda i: (i, 0)) for w in ws]
    return pl.pallas_call(
        functools.partial(_proj_kernel, epilogues=tuple(epilogues), n_aux=len(aux)),
        grid=(m // tm,), in_specs=in_specs, out_specs=out_specs, out_shape=out_shape,
        compiler_params=_cparams(("parallel",)), name=name,
    )(x, *ws, *aux)


def _outproj_ln_kernel(a_ref, w_ref, x_ref, g_ref, b_ref, o_ref):
    y = jnp.dot(a_ref[...].astype(bf16), w_ref[...], preferred_element_type=f32)
    o_ref[...] = _layer_norm(ALPHA * x_ref[...] + y, g_ref[...], b_ref[...])


def _outproj_ln(a, w, x, g, b, tm=PROJ_TM, name="outproj_ln"):
    m, k = a.shape
    d = w.shape[1]
    assert m % tm == 0
    return pl.pallas_call(
        _outproj_ln_kernel, grid=(m // tm,),
        in_specs=[pl.BlockSpec((tm, k), lambda i: (i, 0)), _const_spec(w.shape),
                  pl.BlockSpec((tm, d), lambda i: (i, 0)), _const_spec((1, d)), _const_spec((1, d))],
        out_specs=pl.BlockSpec((tm, d), lambda i: (i, 0)),
        out_shape=jax.ShapeDtypeStruct((m, d), f32),
        compiler_params=_cparams(("parallel",)), name=name,
    )(a, w, x, g.reshape(1, d), b.reshape(1, d))


def _ffn_ln_kernel(x_ref, wg_ref, wu_ref, wo_ref, g_ref, b_ref, o_ref, h_ref):
    j = pl.program_id(1)
    tf = wg_ref.shape[1]
    xb = x_ref[...].astype(bf16)
    hg = jnp.dot(xb, wg_ref[...].astype(bf16), preferred_element_type=f32)
    hu = jnp.dot(xb, wu_ref[...].astype(bf16), preferred_element_type=f32)
    h_ref[:, pl.ds(pl.multiple_of(j * tf, tf), tf)] = (hg * jax.nn.sigmoid(hg) * hu).astype(bf16)

    @pl.when(j == pl.num_programs(1) - 1)
    def _():
        y = jnp.dot(h_ref[...], wo_ref[...].astype(bf16), preferred_element_type=f32)
        o_ref[...] = _layer_norm(ALPHA * x_ref[...] + y, g_ref[...], b_ref[...])


def _ffn_ln(x, w_in, w_out, layer, g, b, tm, tf, name="ffn_ln"):
    m, d = x.shape
    f = w_out.shape[1]
    assert m % tm == 0 and f % tf == 0 and tf % 128 == 0
    nf = f // tf
    return pl.pallas_call(
        _ffn_ln_kernel, grid=(m // tm, nf),
        in_specs=[pl.BlockSpec((tm, d), lambda i, j: (i, 0)),
                  pl.BlockSpec((None, d, tf), lambda i, j: (layer, 0, j)),
                  pl.BlockSpec((None, d, tf), lambda i, j: (layer, 0, j + nf)),
                  pl.BlockSpec((None, f, d), lambda i, j: (layer, 0, 0), pipeline_mode=pl.Buffered(1)),
                  _const_spec((1, d)), _const_spec((1, d))],
        out_specs=pl.BlockSpec((tm, d), lambda i, j: (i, 0)),
        out_shape=jax.ShapeDtypeStruct((m, d), f32),
        scratch_shapes=[pltpu.VMEM((tm, f), bf16)],
        compiler_params=_cparams(("parallel", "arbitrary")), name=name,
    )(x, w_in, w_in, w_out, g.reshape(1, d), b.reshape(1, d))


def _router_kernel(x_ref, w_ref, b_ref, c_ref):
    logits = jnp.dot(x_ref[...], w_ref[...], precision=lax.Precision.HIGHEST,
                     preferred_element_type=f32) + b_ref[...]
    e_idx, i1, i2, w1, w2 = _top2(logits)
    c_ref[...] = jnp.where(e_idx == i1, w1, 0.0) + jnp.where(e_idx == i2, w2, 0.0)


def _router(x, w, b, tm=512):
    m, d = x.shape
    assert m % tm == 0
    return pl.pallas_call(
        _router_kernel, grid=(m // tm,),
        in_specs=[pl.BlockSpec((tm, d), lambda i: (i, 0)), _const_spec(w.shape),
                  _const_spec((1, N_EXPERTS))],
        out_specs=pl.BlockSpec((tm, N_EXPERTS), lambda i: (i, 0)),
        out_shape=jax.ShapeDtypeStruct((m, N_EXPERTS), f32),
        compiler_params=_cparams(("parallel",)), name="router",
    )(x, w, b.reshape(1, N_EXPERTS))


def _moe_ln_kernel(x_ref, c_ref, wg_ref, wu_ref, wo_ref, g_ref, b_ref, o_ref, acc_ref):
    e = pl.program_id(1)
    j = pl.program_id(2)
    x = x_ref[...]
    xb = x.astype(bf16)
    c = c_ref[...]
    ce = jnp.sum(jnp.where(lax.broadcasted_iota(jnp.int32, c.shape, 1) == e, c, 0.0), -1, keepdims=True)
    hg = jnp.dot(xb, wg_ref[...].astype(bf16), preferred_element_type=f32)
    hu = jnp.dot(xb, wu_ref[...].astype(bf16), preferred_element_type=f32)
    act = (hg * jax.nn.sigmoid(hg) * hu).astype(bf16)
    part = ce * jnp.dot(act, wo_ref[...].astype(bf16), preferred_element_type=f32)
    first = jnp.logical_and(e == 0, j == 0)

    @pl.when(first)
    def _():
        acc_ref[...] = part

    @pl.when(jnp.logical_not(first))
    def _():
        acc_ref[...] += part

    @pl.when(jnp.logical_and(e == pl.num_programs(1) - 1, j == pl.num_programs(2) - 1))
    def _():
        o_ref[...] = _layer_norm(ALPHA * x + acc_ref[...], g_ref[...], b_ref[...])


def _moe_ln(x, comb, w_in, w_out, layer, g, b, tm, tf):
    m, d = x.shape
    _, ne, f, _ = w_out.shape
    assert m % tm == 0 and f % tf == 0
    nf = f // tf
    return pl.pallas_call(
        _moe_ln_kernel, grid=(m // tm, ne, nf),
        in_specs=[pl.BlockSpec((tm, d), lambda i, e, j: (i, 0)),
                  pl.BlockSpec((tm, ne), lambda i, e, j: (i, 0)),
                  pl.BlockSpec((None, None, d, tf), lambda i, e, j: (layer, e, 0, j)),
                  pl.BlockSpec((None, None, d, tf), lambda i, e, j: (layer, e, 0, j + nf)),
                  pl.BlockSpec((None, None, tf, d), lambda i, e, j: (layer, e, j, 0)),
                  _const_spec((1, d)), _const_spec((1, d))],
        out_specs=pl.BlockSpec((tm, d), lambda i, e, j: (i, 0)),
        out_shape=jax.ShapeDtypeStruct((m, d), f32),
        scratch_shapes=[pltpu.VMEM((tm, d), f32)],
        compiler_params=_cparams(("parallel", "arbitrary", "arbitrary")), name="moe_ln",
    )(x, comb, w_in, w_in, w_out, g.reshape(1, d), b.reshape(1, d))


MOE_TM = 1024
MOE_ROUTE_TM = 512
MOE_DISPATCH_TM = 512
MOE_COMBINE_TM = 256
ROUTE_COLS = 8
ROW_DMA_UNROLL = 8


def _top2(logits):
    e_idx = lax.broadcasted_iota(jnp.int32, logits.shape, 1)
    v1 = jnp.max(logits, -1, keepdims=True)
    i1 = jnp.min(jnp.where(logits == v1, e_idx, N_EXPERTS), -1, keepdims=True)
    rest = jnp.where(e_idx == i1, -jnp.inf, logits)
    v2 = jnp.max(rest, -1, keepdims=True)
    i2 = jnp.min(jnp.where(rest == v2, e_idx, N_EXPERTS), -1, keepdims=True)
    e2 = jnp.exp(v2 - v1)
    den = 1.0 + e2
    return e_idx, i1, i2, 1.0 / den, e2 / den


def _route_kernel(x_ref, wt_ref, b_ref, r_ref, cnt_ref, carry_ref):
    i = pl.program_id(0)
    tm = x_ref.shape[0]

    @pl.when(i == 0)
    def _():
        carry_ref[...] = jnp.zeros_like(carry_ref)

    logits = lax.dot_general(wt_ref[...], x_ref[...], (((1,), (1,)), ((), ())), precision=lax.Precision.HIGHEST,
                             preferred_element_type=f32) + b_ref[...]
    e_idx = lax.broadcasted_iota(jnp.int32, logits.shape, 0)
    v1 = jnp.max(logits, 0, keepdims=True)
    i1 = jnp.min(jnp.where(logits == v1, e_idx, N_EXPERTS), 0, keepdims=True)
    rest = jnp.where(e_idx == i1, -jnp.inf, logits)
    v2 = jnp.max(rest, 0, keepdims=True)
    i2 = jnp.min(jnp.where(rest == v2, e_idx, N_EXPERTS), 0, keepdims=True)
    e2 = jnp.exp(v2 - v1)
    den = 1.0 + e2
    hit1 = e_idx == i1
    hit2 = e_idx == i2
    a = jnp.where(hit1, 1.0, 0.0) + jnp.where(hit2, 1.0, 0.0)
    earlier = lax.broadcasted_iota(jnp.int32, (tm, tm), 0) < lax.broadcasted_iota(jnp.int32, (tm, tm), 1)
    c = carry_ref[...] + jnp.dot(a.astype(bf16), jnp.where(earlier, 1.0, 0.0).astype(bf16),
                                 preferred_element_type=f32)
    pos1 = jnp.sum(jnp.where(hit1, c, 0.0), 0, keepdims=True)
    pos2 = jnp.sum(jnp.where(hit2, c, 0.0), 0, keepdims=True)
    carry_ref[...] += jnp.sum(a, 1, keepdims=True)
    rec = jnp.zeros((ROUTE_COLS, tm), f32)
    for n, v in enumerate((i1.astype(f32), i2.astype(f32), pos1, pos2, 1.0 / den, e2 / den)):
        rec = jnp.where(e_idx == n, v, rec)
    r_ref[...] = rec

    @pl.when(i == pl.num_programs(0) - 1)
    def _():
        cnt_ref[...] = carry_ref[...]


def _route(x, w, b):
    m, d = x.shape
    tm = MOE_ROUTE_TM
    assert m % tm == 0 and ROUTE_COLS == N_EXPERTS
    return pl.pallas_call(
        _route_kernel, grid=(m // tm,),
        in_specs=[pl.BlockSpec((tm, d), lambda i: (i, 0)), _const_spec((N_EXPERTS, d)), _const_spec((N_EXPERTS, 1))],
        out_specs=[pl.BlockSpec((ROUTE_COLS, tm), lambda i: (0, i)), pl.BlockSpec((N_EXPERTS, 1), lambda i: (0, 0))],
        out_shape=[jax.ShapeDtypeStruct((ROUTE_COLS, m), f32), jax.ShapeDtypeStruct((N_EXPERTS, 1), f32)],
        scratch_shapes=[pltpu.VMEM((N_EXPERTS, 1), f32)],
        compiler_params=_cparams(("arbitrary",)), name="moe_route",
    )(x, w.T, b.reshape(N_EXPERTS, 1))


def _row_copy(src, src_row, dst, dst_row, sem):
    return pltpu.make_async_copy(src.at[pl.ds(src_row, 1)], dst.at[pl.ds(dst_row, 1)], sem)


def _dispatch_kernel(d1_ref, d2_ref, x_ref, init_hbm, o_hbm, sem):
    del init_hbm
    base = pl.program_id(0) * MOE_DISPATCH_TM

    def copies(t):
        tok = base + t
        return (_row_copy(x_ref, t, o_hbm, d1_ref[tok], sem), _row_copy(x_ref, t, o_hbm, d2_ref[tok], sem))

    def issue(t, carry):
        for cp in copies(t):
            cp.start()
        return carry

    def drain(t, carry):
        for cp in copies(t):
            cp.wait()
        return carry

    lax.fori_loop(0, MOE_DISPATCH_TM, issue, 0, unroll=ROW_DMA_UNROLL)
    lax.fori_loop(0, MOE_DISPATCH_TM, drain, 0, unroll=ROW_DMA_UNROLL)


def _dispatch(x, d1, d2, n_rows):
    m, d = x.shape
    assert m % MOE_DISPATCH_TM == 0
    any_spec = pl.BlockSpec(memory_space=pl.ANY)
    grid_spec = pltpu.PrefetchScalarGridSpec(
        num_scalar_prefetch=2, grid=(m // MOE_DISPATCH_TM,),
        in_specs=[pl.BlockSpec((MOE_DISPATCH_TM, d), lambda i, *_: (i, 0)), any_spec], out_specs=any_spec,
        scratch_shapes=[pltpu.SemaphoreType.DMA(())])
    return pl.pallas_call(
        _dispatch_kernel, grid_spec=grid_spec, out_shape=jax.ShapeDtypeStruct((n_rows, d), x.dtype),
        input_output_aliases={3: 0},
        compiler_params=_cparams(("arbitrary",)), name="moe_dispatch",
    )(d1, d2, x, jnp.zeros((n_rows, d), x.dtype))


def _moe_group_kernel(te_ref, nu_ref, x_ref, wg_ref, wu_ref, wo_ref, o_ref, acc_ref):
    del te_ref
    t = pl.program_id(0)
    j = pl.program_id(1)
    used = t < nu_ref[0]
    last = j == pl.num_programs(1) - 1

    @pl.when(used)
    def _():
        xb = x_ref[...].astype(bf16)
        hg = jnp.dot(xb, wg_ref[...].astype(bf16), preferred_element_type=f32)
        hu = jnp.dot(xb, wu_ref[...].astype(bf16), preferred_element_type=f32)
        act = (hg * jax.nn.sigmoid(hg) * hu).astype(bf16)
        part = jnp.dot(act, wo_ref[...].astype(bf16), preferred_element_type=f32)

        @pl.when(j == 0)
        def _():
            acc_ref[...] = part

        @pl.when(j > 0)
        def _():
            acc_ref[...] += part

        @pl.when(last)
        def _():
            o_ref[...] = acc_ref[...]

    @pl.when(jnp.logical_and(jnp.logical_not(used), last))
    def _():
        o_ref[...] = jnp.zeros_like(o_ref)


def _moe_group(xs, tile_expert, n_used, w_in, w_out, layer, tf):
    r, d = xs.shape
    _, ne, f, _ = w_out.shape
    assert r % MOE_TM == 0 and f % tf == 0
    nf = f // tf
    n_tiles = r // MOE_TM

    def jj(t, j, nu):
        return jnp.where(t < nu[0], j, nf - 1)

    grid_spec = pltpu.PrefetchScalarGridSpec(
        num_scalar_prefetch=2, grid=(n_tiles, nf),
        in_specs=[pl.BlockSpec((MOE_TM, d), lambda t, j, te, nu: (jnp.maximum(jnp.minimum(t, nu[0] - 1), 0), 0)),
                  pl.BlockSpec((None, None, d, tf), lambda t, j, te, nu: (layer, te[t], 0, jj(t, j, nu))),
                  pl.BlockSpec((None, None, d, tf), lambda t, j, te, nu: (layer, te[t], 0, jj(t, j, nu) + nf)),
                  pl.BlockSpec((None, None, tf, d), lambda t, j, te, nu: (layer, te[t], jj(t, j, nu), 0))],
        out_specs=pl.BlockSpec((MOE_TM, d), lambda t, j, te, nu: (t, 0)),
        scratch_shapes=[pltpu.VMEM((MOE_TM, d), f32)])
    return pl.pallas_call(
        _moe_group_kernel, grid_spec=grid_spec, out_shape=jax.ShapeDtypeStruct((r, d), f32),
        compiler_params=_cparams(("arbitrary", "arbitrary")), name="moe_group",
    )(tile_expert, n_used, xs, w_in, w_in, w_out)


def _moe_combine_kernel(d1_ref, d2_ref, x_ref, r_ref, y_hbm, g_ref, b_ref, o_ref, ya_ref, yb_ref, sem):
    tm = x_ref.shape[0]
    base = pl.program_id(0) * tm

    def copies(t):
        tok = base + t
        return (_row_copy(y_hbm, d1_ref[tok], ya_ref, t, sem), _row_copy(y_hbm, d2_ref[tok], yb_ref, t, sem))

    def issue(t, carry):
        for cp in copies(t):
            cp.start()
        return carry

    def drain(t, carry):
        for cp in copies(t):
            cp.wait()
        return carry

    lax.fori_loop(0, tm, issue, 0, unroll=ROW_DMA_UNROLL)
    lax.fori_loop(0, tm, drain, 0, unroll=ROW_DMA_UNROLL)
    r = r_ref[...]
    mix = r[:, 4:5] * ya_ref[...] + r[:, 5:6] * yb_ref[...]
    o_ref[...] = _layer_norm(ALPHA * x_ref[...] + mix, g_ref[...], b_ref[...])


def _moe_combine(x, route, y, d1, d2, g, b):
    m, d = x.shape
    tm = MOE_COMBINE_TM
    assert m % tm == 0
    grid_spec = pltpu.PrefetchScalarGridSpec(
        num_scalar_prefetch=2, grid=(m // tm,),
        in_specs=[pl.BlockSpec((tm, d), lambda i, *_: (i, 0)),
                  pl.BlockSpec((tm, ROUTE_COLS), lambda i, *_: (i, 0)),
                  pl.BlockSpec(memory_space=pl.ANY),
                  pl.BlockSpec((1, d), lambda i, *_: (0, 0)), pl.BlockSpec((1, d), lambda i, *_: (0, 0))],
        out_specs=pl.BlockSpec((tm, d), lambda i, *_: (i, 0)),
        scratch_shapes=[pltpu.VMEM((tm, d), f32), pltpu.VMEM((tm, d), f32), pltpu.SemaphoreType.DMA(())])
    return pl.pallas_call(
        _moe_combine_kernel, grid_spec=grid_spec, out_shape=jax.ShapeDtypeStruct((m, d), f32),
        compiler_params=_cparams(("arbitrary",)), name="moe_combine",
    )(d1, d2, x, route, y, g.reshape(1, d), b.reshape(1, d))


def _moe_sparse_ln(x, router_w, router_b, w_in, w_out, layer, g, b, tf):
    m, d = x.shape
    route_t, counts = _route(x, router_w, router_b)
    route = route_t.T
    counts = counts[:, 0].astype(jnp.int32)
    padded = (counts + MOE_TM - 1) // MOE_TM * MOE_TM
    ends = jnp.cumsum(padded)
    offs = ends - padded
    n_tiles = -(-TOP_K * m // MOE_TM) + N_EXPERTS
    n_used = (ends[-1] // MOE_TM).reshape(1)
    tile_start = jnp.minimum(jnp.arange(n_tiles, dtype=jnp.int32), n_used[0] - 1) * MOE_TM
    tile_expert = jnp.minimum(jnp.sum((tile_start[:, None] >= ends[None, :]).astype(jnp.int32), axis=1), N_EXPERTS - 1)
    e1 = route_t[0].astype(jnp.int32)
    e2 = route_t[1].astype(jnp.int32)
    d1 = offs[e1] + route_t[2].astype(jnp.int32)
    d2 = offs[e2] + route_t[3].astype(jnp.int32)
    xs = _dispatch(x, d1, d2, n_tiles * MOE_TM)
    y = _moe_group(xs, tile_expert, n_used, w_in, w_out, layer, tf)
    return _moe_combine(x, route, y, d1, d2, g, b)


def _softplus(x):
    return jnp.maximum(x, 0.0) + jnp.log(1.0 + jnp.exp(-jnp.abs(x)))


LRU_BAND_OUT = 256


def _lru_bands():
    bands = []
    for out_lo in range(0, D_RNN, LRU_BAND_OUT):
        n = min(LRU_BAND_OUT, D_RNN - out_lo)
        in_lo = out_lo // LRU_BS * LRU_BS // 128 * 128
        in_hi = min(-(-(((out_lo + n - 1) // LRU_BS + 1) * LRU_BS) // 128) * 128, D_RNN)
        bands.append((out_lo, n, in_lo, in_hi - in_lo))
    return bands


def _band_weights(w):
    dense = _block_diag(w)
    bands = _lru_bands()
    kmax = max(b[3] for b in bands)
    out = []
    for out_lo, n, in_lo, k in bands:
        out.append(jnp.pad(dense[in_lo:in_lo + k, out_lo:out_lo + n], ((0, kmax - k), (0, LRU_BAND_OUT - n))))
    return jnp.stack(out).astype(bf16)


def _sigmoid_tanh(x):
    return 0.5 * jnp.tanh(0.5 * x) + 0.5


def _lru_gates(xc, wa_ref, ba, wx_ref, bx, sp):
    xcb = xc.astype(bf16)
    ra, rx = [], []
    for g, (_, n, in_lo, k) in enumerate(_lru_bands()):
        xin = xcb[:, in_lo:in_lo + k]
        ra.append(jnp.dot(xin, wa_ref[g, 0:k, 0:n], preferred_element_type=f32))
        rx.append(jnp.dot(xin, wx_ref[g, 0:k, 0:n], preferred_element_type=f32))
    r = _sigmoid_tanh(jnp.concatenate(ra, axis=1) + ba)
    gi = _sigmoid_tanh(jnp.concatenate(rx, axis=1) + bx)
    log_a = sp * r
    th = jnp.tanh(log_a)
    one_minus_a2 = -2.0 * th / (1.0 - th)
    return jnp.exp(log_a), jnp.sqrt(one_minus_a2) * gi * xc


def _lru_core_kernel(xb_ref, g_ref, cw_ref, cb_ref, wa_ref, ba_ref, wx_ref, bx_ref, lam_ref,
                     y_ref, hlast_ref, ext_ref, a_ref, b_ref, h_ref, *, ts, nb):
    i = pl.program_id(0)
    hist = 8

    @pl.when(i == 0)
    def _():
        ext_ref[:, 0:hist, :] = jnp.zeros((nb, hist, ext_ref.shape[2]), f32)
        h_ref[...] = jnp.zeros_like(h_ref)

    ext_ref[:, hist:hist + ts, :] = xb_ref[...]
    sp = -LRU_C * _softplus(-lam_ref[...])
    cw = cw_ref[...]
    for b in range(nb):
        xc = cb_ref[...]
        for k in range(CONV_W):
            off = hist - (CONV_W - 1) + k
            xc = xc + cw[k:k + 1, :] * ext_ref[b, off:off + ts, :]
        a, bt = _lru_gates(xc, wa_ref, ba_ref[...], wx_ref, bx_ref[...], sp)
        a_ref[b] = a
        b_ref[b] = bt
    ext_ref[:, 0:hist, :] = ext_ref[:, ts:ts + hist, :]

    def step(t, h):
        h = a_ref[:, t, :] * h + b_ref[:, t, :]
        b_ref[:, t, :] = h
        return h

    h = lax.fori_loop(0, ts, step, h_ref[...], unroll=8)
    h_ref[...] = h
    y_ref[...] = b_ref[...] * g_ref[...]

    @pl.when(i == pl.num_programs(0) - 1)
    def _():
        hlast_ref[...] = h


def _lru_core(xb, g, cw, cb, wa, ba, wx, bx, lam, ts=64):
    nb, s, c = xb.shape
    assert s % ts == 0 and ts % 8 == 0
    blk = pl.BlockSpec((nb, ts, c), lambda i: (0, i, 0))
    row = _const_spec((1, c))
    return pl.pallas_call(
        functools.partial(_lru_core_kernel, ts=ts, nb=nb), grid=(s // ts,),
        in_specs=[blk, blk, _const_spec((CONV_W, c)), row, _const_spec(wa.shape), row,
                  _const_spec(wx.shape), row, row],
        out_specs=[blk, pl.BlockSpec((nb, c), lambda i: (0, 0))],
        out_shape=[jax.ShapeDtypeStruct((nb, s, c), f32), jax.ShapeDtypeStruct((nb, c), f32)],
        scratch_shapes=[pltpu.VMEM((nb, ts + 8, c), f32), pltpu.VMEM((nb, ts, c), f32),
                        pltpu.VMEM((nb, ts, c), f32), pltpu.VMEM((nb, c), f32)],
        compiler_params=_cparams(("arbitrary",)), name="lru_core",
    )(xb, g, cw, cb.reshape(1, c), wa, ba.reshape(1, c), wx, bx.reshape(1, c), lam.reshape(1, c))


def _lru_short_kernel(xb_ref, g_ref, cs_ref, h0_ref, cw_ref, cb_ref, wa_ref, ba_ref, wx_ref, bx_ref,
                      lam_ref, y_ref, hlast_ref, *, s):
    rows = [cs_ref[k] for k in range(CONV_W - 1)] + [xb_ref[t] for t in range(s)]
    sp = -LRU_C * _softplus(-lam_ref[...])
    cw = cw_ref[...]
    h = h0_ref[...]
    for t in range(s):
        xc = cb_ref[...]
        for k in range(CONV_W):
            xc = xc + cw[k:k + 1, :] * rows[t + k]
        a, bt = _lru_gates(xc, wa_ref, ba_ref[...], wx_ref, bx_ref[...], sp)
        h = a * h + bt
        y_ref[t] = h * g_ref[t]
    hlast_ref[...] = h


def _lru_short(xb, g, cs, h0, cw, cb, wa, ba, wx, bx, lam):
    s, nb, c = xb.shape
    return pl.pallas_call(
        functools.partial(_lru_short_kernel, s=s),
        out_shape=[jax.ShapeDtypeStruct((s, nb, c), f32), jax.ShapeDtypeStruct((nb, c), f32)],
        compiler_params=pltpu.CompilerParams(vmem_limit_bytes=VMEM_LIMIT_V7X), name="lru_short",
    )(xb, g, cs, h0, cw, cb.reshape(1, c), wa, ba.reshape(1, c), wx, bx.reshape(1, c), lam.reshape(1, c))


def _block_diag(w):
    n, k, _ = w.shape
    eye = jnp.eye(n, dtype=w.dtype)
    return (eye[:, None, :, None] * w[:, :, None, :]).reshape(n * k, n * k)


def _gelu_ep(y):
    return jax.nn.gelu(y)


def _lru_layer(xp, xs, nbp, nbs, conv_s, h_s, ln_g, ln_b, w_in, cw, cb, w_a, b_a, w_x, b_x, lam, w_out):
    c = D_RNN
    w_gate = w_in[:, :c].astype(bf16)
    w_xb = w_in[:, c:].astype(bf16)
    wa = _band_weights(w_a)
    wx = _band_weights(w_x)
    wo = w_out.astype(bf16)
    sp_len = xp.shape[0] // nbp
    ss_len = xs.shape[0] // nbs

    gp, xbp = _proj(xp, [w_gate, w_xb], [_gelu_ep, None], tm=PROJ_TM, name="lru_in")
    gs, xbs = _proj(xs, [w_gate, w_xb], [_gelu_ep, None], tm=xs.shape[0], name="lru_in_s")
    xbp3 = xbp.reshape(nbp, sp_len, c)
    yp, hp = _lru_core(xbp3, gp.reshape(nbp, sp_len, c), cw, cb, wa, b_a, wx, b_x, lam)
    xbs3 = xbs.reshape(nbs, ss_len, c)
    ys_t, hs = _lru_short(xbs3.transpose(1, 0, 2), gs.reshape(nbs, ss_len, c).transpose(1, 0, 2),
                          conv_s.transpose(1, 0, 2), h_s, cw, cb, wa, b_a, wx, b_x, lam)
    ys = ys_t.transpose(1, 0, 2).reshape(nbs * ss_len, c)
    xp = _outproj_ln(yp.reshape(nbp * sp_len, c), wo, xp, ln_g, ln_b, name="lru_out")
    xs = _outproj_ln(ys, wo, xs, ln_g, ln_b, tm=xs.shape[0], name="lru_out_s")
    conv_p = xbp3[:, sp_len - (CONV_W - 1):]
    conv_sn = jnp.concatenate([conv_s, xbs3], axis=1)[:, ss_len:]
    return xp, xs, (conv_p, conv_sn, hp, hs)


SSD_GN = SSD_GROUPS * SSD_STATE
SSD_HPG = SSD_HEADS // SSD_GROUPS
SSD_DT_PAD = 128


def _ssd_core_kernel(*refs, q, valid, zero_init):
    if zero_init:
        (xbc_ref, dt_ref, zs_ref, cw_ref, cb_ref, alog_ref, dsk_ref, ng_ref, ex_ref,
         y_ref, sfin_ref, ext_ref, st_ref, yacc_ref) = refs
    else:
        (xbc_ref, dt_ref, zs_ref, hist_ref, s0_ref, cw_ref, cb_ref, alog_ref, dsk_ref, ng_ref, ex_ref,
         y_ref, sfin_ref, ext_ref, st_ref, yacc_ref) = refs
    ci = pl.program_id(1)
    hist = 8
    hp = SSD_HPG * SSD_HEAD_DIM
    pairs = [(g, j) for g in range(SSD_GROUPS) for j in range(SSD_HPG // 2)]

    @pl.when(ci == 0)
    def _():
        if zero_init:
            ext_ref[0:hist, :] = jnp.zeros((hist, ext_ref.shape[1]), f32)
            st_ref[...] = jnp.zeros_like(st_ref)
        else:
            ext_ref[0:hist, :] = hist_ref[...]
            for g, j in pairs:
                h = g * SSD_HPG + 2 * j
                two = jnp.concatenate([s0_ref[h], s0_ref[h + 1]], axis=0)
                st_ref[g, :, 2 * SSD_HEAD_DIM * j:2 * SSD_HEAD_DIM * (j + 1)] = two.T

    ext_ref[hist:hist + q, :] = xbc_ref[...]
    cw = cw_ref[...]
    xc = cb_ref[...]
    for k in range(CONV_W):
        off = hist - (CONV_W - 1) + k
        xc = xc + cw[k:k + 1, :] * ext_ref[off:off + q, :]
    xc = xc * jax.nn.sigmoid(xc)
    if q >= hist:
        ext_ref[0:hist, :] = ext_ref[q:q + hist, :]

    row = lax.broadcasted_iota(jnp.int32, (q, q), 0)
    col = lax.broadcasted_iota(jnp.int32, (q, q), 1)
    causal = col <= row
    dt = dt_ref[...]
    if valid < q:
        dt = jnp.where(lax.broadcasted_iota(jnp.int32, dt.shape, 0) < valid, dt, 0.0)
    a_neg = -jnp.exp(alog_ref[...])
    tril = jnp.where(causal, 1.0, 0.0)
    acum = jnp.dot(tril, dt * a_neg, precision=lax.Precision.HIGHEST, preferred_element_type=f32)
    acum_t = acum.T
    last = acum[q - 1:q, :]

    def per_channel(v):
        hi = v.astype(bf16)
        r1 = v - hi.astype(f32)
        mid = r1.astype(bf16)
        lo = (r1 - mid.astype(f32)).astype(bf16)
        ex = ex_ref[...]
        return (jnp.dot(hi, ex, preferred_element_type=f32) + jnp.dot(mid, ex, preferred_element_type=f32)
                + jnp.dot(lo, ex, preferred_element_type=f32))

    e_acum_x = per_channel(jnp.exp(acum))
    xdt_all = xc[:, :SSD_INNER] * per_channel(dt)
    xdt_b = xdt_all.astype(bf16)
    xdd_b = (xdt_all * per_channel(jnp.exp(last - acum))).astype(bf16)
    dec_chunk_x = e_acum_x[q - 1:q, :]
    two = 2 * SSD_HEAD_DIM
    first_head = lax.broadcasted_iota(jnp.int32, (q, two), 1) < SSD_HEAD_DIM
    nt = (((1,), (1,)), ((), ()))

    for g in range(SSD_GROUPS):
        bm = xc[:, SSD_INNER + g * SSD_STATE:SSD_INNER + (g + 1) * SSD_STATE]
        cm = xc[:, SSD_INNER + SSD_GN + g * SSD_STATE:SSD_INNER + SSD_GN + (g + 1) * SSD_STATE].astype(bf16)
        cb = lax.dot_general(cm, bm.astype(bf16), nt, preferred_element_type=f32)
        ch = slice(g * hp, (g + 1) * hp)
        st = st_ref[g]
        y_off = jnp.dot(cm, st.astype(bf16), preferred_element_type=f32) * e_acum_x[:, ch]
        st_ref[g] = dec_chunk_x[:, ch] * st + jnp.dot(bm.T.astype(bf16), xdd_b[:, ch], preferred_element_type=f32)
        for j in range(SSD_HPG // 2):
            h = g * SSD_HPG + 2 * j
            lanes = slice(h * SSD_HEAD_DIM, (h + 2) * SSD_HEAD_DIM)
            m = []
            for hh in (h, h + 1):
                seg = acum[:, hh:hh + 1] - acum_t[hh:hh + 1, :]
                m.append((cb * jnp.exp(jnp.where(causal, seg, NEG_INF))).astype(bf16))
            if q % 128 == 0:
                x2 = xdt_b[:, lanes]
                zero = jnp.zeros_like(x2)
                rhs = jnp.concatenate([jnp.where(first_head, x2, zero), jnp.where(first_head, zero, x2)], axis=0)
                y_diag = jnp.dot(jnp.concatenate(m, axis=1), rhs, preferred_element_type=f32)
            else:
                y_diag = jnp.concatenate(
                    [jnp.dot(m[n], xdt_b[:, (h + n) * SSD_HEAD_DIM:(h + n + 1) * SSD_HEAD_DIM],
                             preferred_element_type=f32) for n in range(2)], axis=1)
            yacc_ref[:, lanes] = y_diag + y_off[:, 2 * j * SSD_HEAD_DIM:(2 * j + 2) * SSD_HEAD_DIM]

    y = yacc_ref[...] + dsk_ref[...] * xc[:, :SSD_INNER]
    yg = y * zs_ref[...]
    gw = SSD_INNER // SSD_GROUPS
    for g in range(SSD_GROUPS):
        v = yg[:, g * gw:(g + 1) * gw]
        v = v * lax.rsqrt(jnp.mean(v * v, -1, keepdims=True) + RMS_EPS)
        y_ref[:, g * gw:(g + 1) * gw] = v * ng_ref[:, g * gw:(g + 1) * gw]

    @pl.when(ci == pl.num_programs(1) - 1)
    def _():
        for g, j in pairs:
            h = g * SSD_HPG + 2 * j
            both = st_ref[g, :, two * j:two * (j + 1)].T
            sfin_ref[h] = both[:SSD_HEAD_DIM]
            sfin_ref[h + 1] = both[SSD_HEAD_DIM:]


def _ssd_core(xbc, dt, zs, hist, s0, cw, cb, a_log, d_skip, norm_g, q, valid):
    nb, s, cd = xbc.shape
    assert s % q == 0
    zero_init = s0 is None
    blk = lambda w: pl.BlockSpec((None, q, w), lambda b, c: (b, c, 0))
    st_spec = pl.BlockSpec((None, SSD_HEADS, SSD_HEAD_DIM, SSD_STATE), lambda b, c: (b, 0, 0, 0))
    in_specs = [blk(cd), blk(SSD_DT_PAD), blk(SSD_INNER)]
    args = [xbc, dt, zs]
    if not zero_init:
        in_specs += [pl.BlockSpec((None, 8, cd), lambda b, c: (b, 0, 0)), st_spec]
        args += [hist, s0]
    in_specs += [_const_spec((CONV_W, cd)), _const_spec((1, cd)), _const_spec((1, SSD_DT_PAD)),
                 _const_spec((1, SSD_INNER)), _const_spec((1, SSD_INNER)), _const_spec((SSD_DT_PAD, SSD_INNER))]
    pad_h = SSD_DT_PAD - SSD_HEADS
    head_to_channels = (jnp.arange(SSD_DT_PAD, dtype=jnp.int32)[:, None]
                        == jnp.arange(SSD_INNER, dtype=jnp.int32)[None, :] // SSD_HEAD_DIM).astype(bf16)
    args += [cw, cb.reshape(1, cd), jnp.pad(a_log, (0, pad_h)).reshape(1, SSD_DT_PAD),
             jnp.repeat(d_skip, SSD_HEAD_DIM).reshape(1, SSD_INNER), norm_g.reshape(1, SSD_INNER),
             head_to_channels]
    return pl.pallas_call(
        functools.partial(_ssd_core_kernel, q=q, valid=valid, zero_init=zero_init),
        grid=(nb, s // q), in_specs=in_specs,
        out_specs=[blk(SSD_INNER), st_spec],
        out_shape=[jax.ShapeDtypeStruct((nb, s, SSD_INNER), f32),
                   jax.ShapeDtypeStruct((nb, SSD_HEADS, SSD_HEAD_DIM, SSD_STATE), f32)],
        scratch_shapes=[pltpu.VMEM((q + 8, cd), f32),
                        pltpu.VMEM((SSD_GROUPS, SSD_STATE, SSD_HPG * SSD_HEAD_DIM), f32),
                        pltpu.VMEM((q, SSD_INNER), f32)],
        compiler_params=_cparams(("parallel", "arbitrary")), name="ssd_core",
    )(*args)


def _silu_ep(y, *_):
    return y * jax.nn.sigmoid(y)


def _dt_ep(y, bias):
    return _softplus(y + bias)


def _pass_ep(y, *_):
    return y


def _ssd_layer(xp, xs, nbp, nbs, conv_s, st_s, ln_g, ln_b, w_in, cw, cb, dt_bias, a_log, d_skip, norm_g, w_out):
    cd = SSD_CONV_DIM
    pad_h = SSD_DT_PAD - SSD_HEADS
    w_z = w_in[:, :SSD_INNER].astype(bf16)
    w_xbc = w_in[:, SSD_INNER:SSD_INNER + cd].astype(bf16)
    w_dt = jnp.pad(w_in[:, SSD_INNER + cd:], ((0, 0), (0, pad_h))).astype(bf16)
    bias = jnp.pad(dt_bias, (0, pad_h)).reshape(1, SSD_DT_PAD)
    wo = w_out.astype(bf16)
    sp_len = xp.shape[0] // nbp
    ss_len = xs.shape[0] // nbs
    eps = [_silu_ep, _pass_ep, _dt_ep]

    zsp, xbcp, dtp = _proj(xp, [w_z, w_xbc, w_dt], eps, aux=(bias,), aux_period=(0,), tm=PROJ_TM, name="ssd_in")
    zss, xbcs, dts = _proj(xs, [w_z, w_xbc, w_dt], eps, aux=(bias,), aux_period=(0,), tm=xs.shape[0], name="ssd_in_s")
    xbcp3 = xbcp.reshape(nbp, sp_len, cd)
    yp, sp_fin = _ssd_core(xbcp3, dtp.reshape(nbp, sp_len, SSD_DT_PAD), zsp.reshape(nbp, sp_len, SSD_INNER),
                           None, None, cw, cb, a_log, d_skip, norm_g, q=math.gcd(sp_len, SSD_CHUNK), valid=SSD_CHUNK)
    qs = 8
    assert ss_len <= qs
    tpad = lambda a: jnp.pad(a, ((0, 0), (0, qs - ss_len), (0, 0)))
    xbcs3 = xbcs.reshape(nbs, ss_len, cd)
    hist = jnp.pad(conv_s, ((0, 0), (8 - (CONV_W - 1), 0), (0, 0)))
    ys, ss_fin = _ssd_core(tpad(xbcs3), tpad(dts.reshape(nbs, ss_len, SSD_DT_PAD)),
                           tpad(zss.reshape(nbs, ss_len, SSD_INNER)), hist, st_s,
                           cw, cb, a_log, d_skip, norm_g, q=qs, valid=ss_len)
    ys = ys[:, :ss_len].reshape(nbs * ss_len, SSD_INNER)
    xp = _outproj_ln(yp.reshape(nbp * sp_len, SSD_INNER), wo, xp, ln_g, ln_b, name="ssd_out")
    xs = _outproj_ln(ys, wo, xs, ln_g, ln_b, tm=xs.shape[0], name="ssd_out_s")
    conv_p = xbcp3[:, sp_len - (CONV_W - 1):]
    conv_sn = jnp.concatenate([conv_s, xbcs3], axis=1)[:, ss_len:]
    return xp, xs, (conv_p, conv_sn, sp_fin, ss_fin)


NSA_TILE = 128
NSA_SCALE = NSA_HD ** -0.5


def _rope_tables(pos):
    inv = ROPE_THETA ** (-jnp.arange(0, NSA_HD, 2, dtype=f32) / NSA_HD)
    ang = pos[:, None] * inv[None, :]
    cos, sin = jnp.cos(ang), jnp.sin(ang)
    return jnp.tile(cos, (1, 4)), jnp.tile(jnp.concatenate([-sin, sin], axis=1), (1, 2))


def _rope_cols(y, cos, sgn_sin, blocks):
    lane = lax.broadcasted_iota(jnp.int32, (y.shape[0], 128), 1)
    first_half = (lane % NSA_HD) < NSA_HD // 2
    out = []
    for c in range(y.shape[1] // 128):
        blk = y[:, 128 * c:128 * (c + 1)]
        if c in blocks:
            partner = jnp.where(first_half, pltpu.roll(blk, 128 - NSA_HD // 2, 1), pltpu.roll(blk, NSA_HD // 2, 1))
            blk = blk * cos + partner * sgn_sin
        out.append(blk)
    return jnp.concatenate(out, axis=1)


def _q_ep(y, cos, sin):
    return _rope_cols(y, cos, sin, range(NSA_HQ // 128)) * NSA_SCALE


def _rows_ep(y, cos, sin):
    return _rope_cols(y, cos, sin, (4, 5))


def _wrows_ep(y, cos, sin):
    return _rope_cols(y, cos, sin, (0, 1))


def _sigmoid_ep(y, *_):
    return jax.nn.sigmoid(y)


def _nsa_cmp_kernel(pt_ref, pa_ref, pb_ref, cos_ref, sin_ref, o_ref):
    del pt_ref
    per_page = NSA_TILE // CMP_BLOCK
    for n, ref in enumerate((pa_ref, pb_ref)):
        m = ref[...].reshape(per_page, CMP_BLOCK, ref.shape[1]).sum(1) * (1.0 / CMP_BLOCK)
        lo, hi = n * per_page, (n + 1) * per_page
        o_ref[lo:hi, :] = _rope_cols(m, cos_ref[lo:hi, :], sin_ref[lo:hi, :], (0, 1))


def _nsa_cmp(pool, page_table, cos, sin):
    nb, n_pages = page_table.shape
    assert n_pages % 2 == 0
    w = 2 * NSA_HK
    per_step = 2 * NSA_TILE // CMP_BLOCK
    grid_spec = pltpu.PrefetchScalarGridSpec(
        num_scalar_prefetch=1, grid=(nb, n_pages // 2),
        in_specs=[pl.BlockSpec((None, NSA_TILE, w), lambda b, p, pt: (pt[b, 2 * p], 0, 0)),
                  pl.BlockSpec((None, NSA_TILE, w), lambda b, p, pt: (pt[b, 2 * p + 1], 0, 0)),
                  pl.BlockSpec((per_step, 128), lambda b, p, pt: (p, 0)),
                  pl.BlockSpec((per_step, 128), lambda b, p, pt: (p, 0))],
        out_specs=pl.BlockSpec((None, per_step, w), lambda b, p, pt: (b, p, 0)))
    return pl.pallas_call(
        _nsa_cmp_kernel, grid_spec=grid_spec,
        out_shape=jax.ShapeDtypeStruct((nb, n_pages * NSA_TILE // CMP_BLOCK, w), f32),
        compiler_params=_cparams(("parallel", "arbitrary")), name="nsa_cmp",
    )(page_table, pool, pool, cos, sin)


def _even_odd(cmp):
    nb, nc, w = cmp.shape
    return cmp.reshape(nb, nc // 2, 2, w).transpose(0, 2, 1, 3).reshape(nb, nc, w)


def _topk_mask_rows(score_t, k):
    n = score_t.shape[0]
    assert n % 8 == 0
    pieces = [score_t[8 * v:8 * v + 8] for v in range(n // 8)]
    ridx = lax.broadcasted_iota(jnp.int32, pieces[0].shape, 0)
    cnts = [jnp.zeros(pieces[0].shape, f32) for _ in pieces]
    for i in range(n):
        si = score_t[i:i + 1, :]
        for v, pc in enumerate(pieces):
            if 8 * v > i:
                cnts[v] = cnts[v] + jnp.where(si >= pc, 1.0, 0.0)
            elif 8 * v + 7 < i:
                cnts[v] = cnts[v] + jnp.where(si > pc, 1.0, 0.0)
            else:
                tie_wins = jnp.where(ridx + 8 * v > i, 1.0, 0.0)
                cnts[v] = cnts[v] + jnp.where(si > pc, 1.0, 0.0) + jnp.where(si == pc, tie_wins, 0.0)
    return jnp.concatenate([jnp.where(c < k, 1.0, 0.0) for c in cnts], axis=0)


def _topk_mask_lanes(score, k, n_real):
    lane = lax.broadcasted_iota(jnp.int32, score.shape, 1)
    cnt = jnp.zeros(score.shape, f32)
    for i in range(n_real):
        si = score[:, i:i + 1]
        cnt = cnt + jnp.where(si > score, 1.0, 0.0) + jnp.where(si == score, jnp.where(lane > i, 1.0, 0.0), 0.0)
    return jnp.where(cnt < k, 1.0, 0.0)


def _softmax_rows(s):
    e = jnp.exp(s - jnp.max(s, -1, keepdims=True))
    return e / jnp.sum(e, -1, keepdims=True)


def _flash_step(qk, kt, vt, ok, carry):
    m, l, acc = carry
    s = lax.dot_general(qk, kt, (((1,), (1,)), ((), ())), preferred_element_type=f32)
    s = jnp.where(ok, s, NEG_INF)
    m_new = jnp.maximum(m, jnp.max(s, -1, keepdims=True))
    alpha = jnp.exp(m - m_new)
    p = jnp.exp(s - m_new)
    l = alpha * l + jnp.sum(p, -1, keepdims=True)
    acc = alpha * acc + jnp.dot(p.astype(bf16), vt, preferred_element_type=f32)
    return m_new, l, acc


def _flash_init(rows):
    return (jnp.full((rows, 1), NEG_INF, f32), jnp.zeros((rows, 1), f32), jnp.zeros((rows, NSA_HD), f32))


def _rope_rows(y, cos_t, sin_t, heads):
    half = NSA_HD // 2
    out = []
    for h in range(heads):
        top = y[NSA_HD * h:NSA_HD * h + half]
        bot = y[NSA_HD * h + half:NSA_HD * (h + 1)]
        out += [top * cos_t - bot * sin_t, bot * cos_t + top * sin_t]
    return jnp.concatenate(out, axis=0)


def _nsa_in_kernel(x_ref, wn_ref, wq_ref, wr_ref, ww_ref, wg_ref, cos_ref, sin_ref, cost_ref, sint_ref,
                   nat_ref, qt_ref, rt_ref, wt_ref, gt_ref):
    x = x_ref[...].astype(bf16)
    nat = jnp.dot(x, wn_ref[...], preferred_element_type=f32)
    nat_ref[...] = _rope_cols(nat, cos_ref[...], sin_ref[...], (4, 5, 6, 7))
    nt = (((1,), (1,)), ((), ()))
    cos_t, sin_t = cost_ref[...], sint_ref[...]
    q_t = lax.dot_general(wq_ref[...], x, nt, preferred_element_type=f32)
    qt_ref[...] = _rope_rows(q_t, cos_t, sin_t, NSA_HEADS) * NSA_SCALE
    r_t = lax.dot_general(wr_ref[...], x, nt, preferred_element_type=f32)
    rt_ref[0:2 * NSA_HK, :] = r_t[0:2 * NSA_HK]
    rt_ref[2 * NSA_HK:3 * NSA_HK, :] = _rope_rows(r_t[2 * NSA_HK:3 * NSA_HK], cos_t, sin_t, NSA_KV_HEADS)
    rt_ref[3 * NSA_HK:4 * NSA_HK, :] = r_t[3 * NSA_HK:4 * NSA_HK]
    w_t = lax.dot_general(ww_ref[...], x, nt, preferred_element_type=f32)
    wt_ref[0:NSA_HK, :] = _rope_rows(w_t[0:NSA_HK], cos_t, sin_t, NSA_KV_HEADS)
    wt_ref[NSA_HK:2 * NSA_HK, :] = w_t[NSA_HK:2 * NSA_HK]
    gt_ref[...] = jax.nn.sigmoid(lax.dot_general(wg_ref[...], x, nt, preferred_element_type=f32))


def _nsa_in(x, nb, w_in, tm=PROJ_TM):
    m, d = x.shape
    s_len = m // nb
    assert s_len % tm == 0
    hq, hk = NSA_HQ, NSA_HK
    w_q = w_in[:, :hq]
    w_kv = w_in[:, hq:hq + 6 * hk].reshape(d, 6, hk)
    w_g = w_in[:, hq + 6 * hk:]
    w_nat = w_kv[:, jnp.array([0, 1, 2, 4])].reshape(d, 4 * hk).astype(bf16)
    w_q_t = w_q.T.astype(bf16)
    w_rows_t = w_kv[:, 0:4].reshape(d, 4 * hk).T.astype(bf16)
    w_win_t = w_kv[:, 4:6].reshape(d, 2 * hk).T.astype(bf16)
    w_g_t = w_g.T.astype(bf16)
    pos = jnp.arange(s_len, dtype=f32)
    cos, sin = _rope_tables(pos)
    inv = ROPE_THETA ** (-jnp.arange(0, NSA_HD, 2, dtype=f32) / NSA_HD)
    ang_t = inv[:, None] * pos[None, :]
    nt = s_len // tm
    ng = 3 * NSA_HEADS
    t_spec = lambda r: pl.BlockSpec((None, r, tm), lambda b, i: (b, 0, i))
    tab = pl.BlockSpec((tm, 128), lambda b, i: (i, 0))
    tab_t = pl.BlockSpec((NSA_HD // 2, tm), lambda b, i: (0, i))
    return pl.pallas_call(
        _nsa_in_kernel, grid=(nb, nt),
        in_specs=[pl.BlockSpec((tm, d), lambda b, i: (b * nt + i, 0)),
                  _const_spec(w_nat.shape), _const_spec(w_q_t.shape), _const_spec(w_rows_t.shape),
                  _const_spec(w_win_t.shape), _const_spec(w_g_t.shape), tab, tab, tab_t, tab_t],
        out_specs=[pl.BlockSpec((tm, 4 * hk), lambda b, i: (b * nt + i, 0)), t_spec(hq), t_spec(4 * hk),
                   t_spec(2 * hk), t_spec(ng)],
        out_shape=[jax.ShapeDtypeStruct((m, 4 * hk), f32), jax.ShapeDtypeStruct((nb, hq, s_len), f32),
                   jax.ShapeDtypeStruct((nb, 4 * hk, s_len), f32), jax.ShapeDtypeStruct((nb, 2 * hk, s_len), f32),
                   jax.ShapeDtypeStruct((nb, ng, s_len), f32)],
        compiler_params=_cparams(("parallel", "parallel")), name="nsa_in",
    )(x, w_nat, w_q_t, w_rows_t, w_win_t, w_g_t, cos, sin, jnp.cos(ang_t), jnp.sin(ang_t))


def _flash_cols(k_tile, v_t, q_ref, bias, m_ref, l_ref, acc_ref):
    qw = NSA_TILE
    for g in range(q_ref.shape[1] // qw):
        c = slice(g * qw, (g + 1) * qw)
        s = jnp.dot(k_tile, q_ref[:, c], preferred_element_type=f32)
        if bias is not None:
            s = s + bias
        m_old = m_ref[:, c]
        m_new = jnp.maximum(m_old, jnp.max(s, 0, keepdims=True))
        alpha = jnp.exp(m_old - m_new)
        p = jnp.exp(s - m_new)
        l_ref[:, c] = alpha * l_ref[:, c] + jnp.sum(p, 0, keepdims=True)
        m_ref[:, c] = m_new
        acc_ref[:, c] = alpha * acc_ref[:, c] + jnp.dot(v_t, p.astype(bf16), preferred_element_type=f32)


def _nsa_seq_kernel(qt_ref, gt_ref, kcb_ref, vcbt_ref, knat_ref, vst_ref, vwt_ref, o_ref,
                    q_sc, oc_sc, os_sc, m_sc, l_sc, acc_sc, *, s_len):
    i = pl.program_id(1)
    qb = NSA_TILE
    nc = s_len // CMP_BLOCK
    nsb = s_len // SEL_BLOCK
    cols = NSA_GROUP * qb
    tile4 = lambda a: jnp.concatenate([a] * NSA_GROUP, axis=1)
    t_q = i * qb + lax.broadcasted_iota(jnp.int32, (1, qb), 1)
    t_cols = tile4(t_q)
    cl = lax.broadcasted_iota(jnp.int32, (nc, 1), 0)
    cblk = jnp.where(cl < nc // 2, 2 * cl, 2 * (cl - nc // 2) + 1)
    cmask = (cblk + 1) * CMP_BLOCK - 1 <= t_cols
    has_cmp = jnp.where(t_cols >= CMP_BLOCK - 1, 1.0, 0.0)
    blk = lax.broadcasted_iota(jnp.int32, (nsb, 1), 0)
    valid = blk * SEL_BLOCK <= t_q
    forced = (blk == 0) | (blk == t_q // SEL_BLOCK)
    key_in = lax.broadcasted_iota(jnp.int32, (NSA_TILE, 1), 0)
    blk_of_key = lax.broadcasted_iota(jnp.int32, (NSA_TILE, 2 * NSA_HD), 0) // SEL_BLOCK
    lane = lax.broadcasted_iota(jnp.int32, (NSA_TILE, 2 * NSA_HD), 1)
    own_lanes = [lane < NSA_HD, lane >= NSA_HD]
    gates = gt_ref[...]
    kcols = [slice(NSA_HD * k, NSA_HD * (k + 1)) for k in range(NSA_KV_HEADS)]
    wcols = [slice(NSA_HK + NSA_HD * k, NSA_HK + NSA_HD * (k + 1)) for k in range(NSA_KV_HEADS)]

    def reset():
        m_sc[...] = jnp.full(m_sc.shape, NEG_INF, f32)
        l_sc[...] = jnp.zeros_like(l_sc)
        acc_sc[...] = jnp.zeros_like(acc_sc)

    for k in range(NSA_KV_HEADS):
        kcol = kcols[k]
        q_t = jnp.concatenate([qt_ref[NSA_HD * (NSA_GROUP * k + g):NSA_HD * (NSA_GROUP * k + g + 1), :]
                               for g in range(NSA_GROUP)], axis=1).astype(bf16)

        s_c = jnp.dot(kcb_ref[:, kcol].astype(bf16), q_t, preferred_element_type=f32)
        s_c = jnp.where(cmask, s_c, NEG_INF)
        e = jnp.exp(s_c - jnp.max(s_c, 0, keepdims=True))
        p_c = e / jnp.sum(e, 0, keepdims=True) * has_cmp
        o_c = jnp.dot(vcbt_ref[kcol, :].astype(bf16), p_c.astype(bf16), preferred_element_type=f32)

        imp = p_c[:, 0:qb]
        for g in range(1, NSA_GROUP):
            imp = imp + p_c[:, g * qb:(g + 1) * qb]
        imp = imp[:nc // 2] + imp[nc // 2:]
        score = jnp.where(forced, FORCE_SCORE, jnp.where(valid, imp, -1.0))
        sel = _topk_mask_rows(score, SEL_TOPK)
        sel_bias = tile4(jnp.where(sel > 0.5, 0.0, NEG_INF)).astype(bf16)
        fill = jnp.zeros((NSA_HD - nsb, cols), bf16)
        q_sc[k] = jnp.concatenate([q_t, sel_bias, fill] if k % 2 == 0 else [sel_bias, fill, q_t], axis=0)
        oc_sc[k] = o_c

    def sel_tile(j, bias):
        off = pl.multiple_of(j * NSA_TILE, NSA_TILE)
        keys = pl.ds(off, NSA_TILE)
        blk_key = blk_of_key + j * (NSA_TILE // SEL_BLOCK)
        onehot = [jnp.where(lane - NSA_HD == blk_key, 1.0, 0.0).astype(bf16),
                  jnp.where(lane == blk_key, 1.0, 0.0).astype(bf16)]
        for pair in range(NSA_KV_HEADS // 2):
            k_both = knat_ref[keys, 2 * NSA_HD * pair:2 * NSA_HD * (pair + 1)].astype(bf16)
            for own in range(2):
                k = 2 * pair + own
                k_aug = jnp.where(own_lanes[own], k_both, onehot[own])
                _flash_cols(k_aug, vst_ref[kcols[k], keys].astype(bf16), q_sc.at[k], bias,
                            m_sc.at[k], l_sc.at[k], acc_sc.at[k])

    def win_tile(j, bias):
        off = pl.multiple_of(j * NSA_TILE, NSA_TILE)
        keys = pl.ds(off, NSA_TILE)
        for k in range(NSA_KV_HEADS):
            _flash_cols(knat_ref[keys, wcols[k]].astype(bf16), vwt_ref[kcols[k], keys].astype(bf16),
                        q_sc.at[k, pl.ds(NSA_HD * (k % 2), NSA_HD)], bias, m_sc.at[k], l_sc.at[k], acc_sc.at[k])

    def causal_bias(j):
        return jnp.where(j * NSA_TILE + key_in <= t_q, 0.0, NEG_INF)

    def window_bias(j):
        kpos = j * NSA_TILE + key_in
        return jnp.where((kpos <= t_q) & (kpos > t_q - WINDOW), 0.0, NEG_INF)

    def loop(tile_fn, lo, hi):
        def body(j, carry):
            tile_fn(j, None)
            return carry
        lax.fori_loop(lo, hi, body, 0)

    reset()
    loop(sel_tile, 0, i)
    sel_tile(i, causal_bias(i))
    os_sc[...] = acc_sc[...] / l_sc[...]

    reset()
    n_back = WINDOW // NSA_TILE

    @pl.when(i >= n_back)
    def _():
        win_tile(i - n_back, window_bias(i - n_back))

    loop(win_tile, jnp.maximum(i - n_back + 1, 0), i)
    win_tile(i, window_bias(i))

    for k in range(NSA_KV_HEADS):
        o_c = oc_sc[k]
        o_s = os_sc[k]
        o_w = acc_sc[k] / l_sc[k]
        for pair in range(NSA_GROUP // 2):
            o_t = []
            for g in (2 * pair, 2 * pair + 1):
                h = NSA_GROUP * k + g
                c = slice(g * qb, (g + 1) * qb)
                o_t.append(gates[3 * h:3 * h + 1, :] * o_c[:, c] + gates[3 * h + 1:3 * h + 2, :] * o_s[:, c]
                           + gates[3 * h + 2:3 * h + 3, :] * o_w[:, c])
            lane0 = NSA_HD * (NSA_GROUP * k + 2 * pair)
            o_ref[:, lane0:lane0 + 2 * NSA_HD] = jnp.concatenate(o_t, axis=0).T


def _nsa_seq(q_t, gates_t, kcb, vcb_t, nat, rows_t, wrows_t):
    nb, _, s_len = q_t.shape
    assert s_len % NSA_TILE == 0 and (s_len // CMP_BLOCK) % 2 == 0 and (s_len // SEL_BLOCK) % 8 == 0
    assert s_len // SEL_BLOCK <= NSA_HD and 2 * NSA_HD == NSA_TILE
    nc = s_len // CMP_BLOCK
    hk = NSA_HK
    cols = NSA_GROUP * NSA_TILE
    per_b = lambda shape, idx: pl.BlockSpec((None,) + shape, lambda b, i: (b,) + idx)
    return pl.pallas_call(
        functools.partial(_nsa_seq_kernel, s_len=s_len), grid=(nb, s_len // NSA_TILE),
        in_specs=[pl.BlockSpec((None, NSA_HQ, NSA_TILE), lambda b, i: (b, 0, i)),
                  pl.BlockSpec((None, 3 * NSA_HEADS, NSA_TILE), lambda b, i: (b, 0, i)),
                  per_b((nc, hk), (0, 0)), per_b((hk, nc), (0, 0)),
                  per_b((s_len, 2 * hk), (0, 1)),
                  per_b((hk, s_len), (3, 0)),
                  per_b((hk, s_len), (1, 0))],
        out_specs=pl.BlockSpec((None, NSA_TILE, NSA_HQ), lambda b, i: (b, i, 0)),
        out_shape=jax.ShapeDtypeStruct((nb, s_len, NSA_HQ), f32),
        scratch_shapes=[pltpu.VMEM((NSA_KV_HEADS, 2 * NSA_HD, cols), bf16),
                        pltpu.VMEM((NSA_KV_HEADS, NSA_HD, cols), f32), pltpu.VMEM((NSA_KV_HEADS, NSA_HD, cols), f32),
                        pltpu.VMEM((NSA_KV_HEADS, 1, cols), f32), pltpu.VMEM((NSA_KV_HEADS, 1, cols), f32),
                        pltpu.VMEM((NSA_KV_HEADS, NSA_HD, cols), f32)],
        compiler_params=_cparams(("parallel", "arbitrary")), name="nsa_seq",
    )(q_t, gates_t, kcb, vcb_t, nat, rows_t, wrows_t)


DEC_CHUNK_PAGES = 8


def _flash_step_t(qk, k_t, v_t, ok, carry):
    m, l, acc = carry
    s = jnp.dot(qk, k_t, preferred_element_type=f32)
    s = jnp.where(ok, s, NEG_INF)
    m_new = jnp.maximum(m, jnp.max(s, -1, keepdims=True))
    alpha = jnp.exp(m - m_new)
    p = jnp.exp(s - m_new)
    l = alpha * l + jnp.sum(p, -1, keepdims=True)
    acc = alpha * acc + lax.dot_general(p.astype(bf16), v_t, (((1,), (1,)), ((), ())), preferred_element_type=f32)
    return m_new, l, acc


def _nsa_dec_kernel(pt_ref, q_ref, g_ref, cos_ref, sin_ref, avg_ref, pool_hbm, new_ref, win_ref, wnew_ref, o_ref,
                    buf, sems, cmp_sc, bias_sc, m_sc, l_sc, acc_sc, *, past_len, n_new):
    b = pl.program_id(0)
    rows = NSA_GROUP * n_new
    t_len = past_len + n_new
    nc = t_len // CMP_BLOCK
    nsb = -(-t_len // SEL_BLOCK)
    n_pages = past_len // NSA_TILE
    cpg = DEC_CHUNK_PAGES
    n_chunks = n_pages // cpg
    ckeys = cpg * NSA_TILE
    new_pad = new_ref.shape[0]

    def page_copy(c, i):
        half = c // n_chunks
        page = pt_ref[b, (c % n_chunks) * cpg + i]
        return pltpu.make_async_copy(pool_hbm.at[page, half], buf.at[c % 2, :, pl.ds(i * NSA_TILE, NSA_TILE)],
                                     sems.at[c % 2])

    def start_chunk(c):
        for i in range(cpg):
            page_copy(c, i).start()

    def wait_chunk(c):
        for i in range(cpg):
            page_copy(c, i).wait()

    start_chunk(0)
    cmp_sc[...] = jnp.zeros_like(cmp_sc)

    def pass1(c, carry):
        start_chunk(c + 1)
        wait_chunk(c)
        x = buf[c % 2]
        avg = avg_ref[c]
        hi = x.astype(bf16)
        r1 = x - hi.astype(f32)
        mid = r1.astype(bf16)
        lo = (r1 - mid.astype(f32)).astype(bf16)
        cmp_sc[...] += (jnp.dot(hi, avg, preferred_element_type=f32) + jnp.dot(mid, avg, preferred_element_type=f32)
                        + jnp.dot(lo, avg, preferred_element_type=f32))
        return carry

    lax.fori_loop(0, n_chunks, pass1, 0)

    nr = NSA_KV_HEADS * rows
    rid = lax.broadcasted_iota(jnp.int32, (nr, 1), 0)
    t_all = past_len + rid % n_new
    zero_q = jnp.zeros((rows, NSA_HD), bf16)
    q_bd = jnp.concatenate(
        [jnp.concatenate([q_ref[k].astype(bf16) if kk == k else zero_q for kk in range(NSA_KV_HEADS)], axis=1)
         for k in range(NSA_KV_HEADS)], axis=0)
    nt = (((1,), (1,)), ((), ()))

    cl = lax.broadcasted_iota(jnp.int32, (1, nc), 1)
    cblk = jnp.where(cl < nc // 2, 2 * cl, 2 * (cl - nc // 2) + 1)
    cmask = (cblk + 1) * CMP_BLOCK - 1 <= t_all
    rr = lax.broadcasted_iota(jnp.int32, (nr, nr), 0)
    rc = lax.broadcasted_iota(jnp.int32, (nr, nr), 1)
    group_sum = jnp.where((rr % n_new == rc % n_new) & (rr // rows == rc // rows), 1.0, 0.0)
    blk = lax.broadcasted_iota(jnp.int32, (nr, nc), 1)
    valid = blk * SEL_BLOCK <= t_all
    forced = (blk == 0) | (blk == t_all // SEL_BLOCK)
    cos, sin = cos_ref[...], sin_ref[...]
    kcb_t = _rope_rows(cmp_sc[0:NSA_HK, :], cos, sin, NSA_KV_HEADS).astype(bf16)
    vcb_t = cmp_sc[NSA_HK:2 * NSA_HK, :].astype(bf16)
    s_c = jnp.dot(q_bd, kcb_t, preferred_element_type=f32)
    p_c = _softmax_rows(jnp.where(cmask, s_c, NEG_INF))
    p_c = p_c * jnp.where(t_all >= CMP_BLOCK - 1, 1.0, 0.0)
    o_c = lax.dot_general(p_c.astype(bf16), vcb_t, nt, preferred_element_type=f32)
    imp = jnp.dot(group_sum, p_c, precision=lax.Precision.HIGHEST, preferred_element_type=f32)
    imp = imp[:, :nc // 2] + imp[:, nc // 2:]
    imp = jnp.concatenate([imp, jnp.zeros((nr, nc - nc // 2), f32)], axis=1)
    score = jnp.where(forced, FORCE_SCORE, jnp.where(valid, imp, -1.0))
    score = jnp.where(blk < nsb, score, -2.0)
    sel = _topk_mask_lanes(score, SEL_TOPK, nsb)
    bias_sc[...] = jnp.where(sel > 0.5, 0.0, NEG_INF).astype(bf16)
    in_last = sel[:, nsb - 1:nsb] > 0.5
    m_sc[...] = jnp.full(m_sc.shape, NEG_INF, f32)
    l_sc[...] = jnp.zeros_like(l_sc)
    acc_sc[...] = jnp.zeros_like(acc_sc)

    def flash(s, v, v_transposed):
        m_new = jnp.maximum(m_sc[...], jnp.max(s, -1, keepdims=True))
        alpha = jnp.exp(m_sc[...] - m_new)
        p = jnp.exp(s - m_new)
        l_sc[...] = alpha * l_sc[...] + jnp.sum(p, -1, keepdims=True)
        m_sc[...] = m_new
        if v_transposed:
            pv = lax.dot_general(p.astype(bf16), v, nt, preferred_element_type=f32)
        else:
            pv = jnp.dot(p.astype(bf16), v, preferred_element_type=f32)
        acc_sc[...] = alpha * acc_sc[...] + pv

    def pass2(c, carry):
        @pl.when(c + 1 < 2 * n_chunks)
        def _():
            start_chunk(c + 1)

        wait_chunk(c)
        key0 = (c - n_chunks) * ckeys
        eb = lax.broadcasted_iota(jnp.int32, (nc, ckeys), 0)
        ek = (key0 + lax.broadcasted_iota(jnp.int32, (nc, ckeys), 1)) // SEL_BLOCK
        onehot = jnp.where(eb == ek, 1.0, 0.0).astype(bf16)
        s = (jnp.dot(q_bd, buf[c % 2, 0:NSA_HK, :].astype(bf16), preferred_element_type=f32)
             + jnp.dot(bias_sc[...], onehot, preferred_element_type=f32))
        flash(s, buf[c % 2, NSA_HK:2 * NSA_HK, :].astype(bf16), True)
        return carry

    lax.fori_loop(n_chunks, 2 * n_chunks, pass2, 0)

    jn = lax.broadcasted_iota(jnp.int32, (1, new_pad), 1)
    npos = past_len + jn
    wpos = past_len - WINDOW + lax.broadcasted_iota(jnp.int32, (1, WINDOW), 1)
    ok = in_last & (npos <= t_all) & (jn < n_new)
    s = lax.dot_general(q_bd, new_ref[:, 0:NSA_HK].astype(bf16), nt, preferred_element_type=f32)
    flash(jnp.where(ok, s, NEG_INF), new_ref[:, NSA_HK:2 * NSA_HK].astype(bf16), False)
    o_s = acc_sc[...] / l_sc[...]

    m_sc[...] = jnp.full(m_sc.shape, NEG_INF, f32)
    l_sc[...] = jnp.zeros_like(l_sc)
    acc_sc[...] = jnp.zeros_like(acc_sc)
    ok = (wpos <= t_all) & (wpos > t_all - WINDOW)
    s = jnp.dot(q_bd, win_ref[0:NSA_HK, :].astype(bf16), preferred_element_type=f32)
    flash(jnp.where(ok, s, NEG_INF), win_ref[NSA_HK:2 * NSA_HK, :].astype(bf16), True)
    ok = (npos <= t_all) & (npos > t_all - WINDOW) & (jn < n_new)
    s = lax.dot_general(q_bd, wnew_ref[:, 0:NSA_HK].astype(bf16), nt, preferred_element_type=f32)
    flash(jnp.where(ok, s, NEG_INF), wnew_ref[:, NSA_HK:2 * NSA_HK].astype(bf16), False)
    o_w = acc_sc[...] / l_sc[...]

    g = jnp.concatenate([g_ref[k] for k in range(NSA_KV_HEADS)], axis=0)
    o = g[:, 0:1] * o_c + g[:, 1:2] * o_s + g[:, 2:3] * o_w
    for k in range(NSA_KV_HEADS):
        o_ref[k] = o[rows * k:rows * (k + 1), NSA_HD * k:NSA_HD * (k + 1)]


def _nsa_dec(q, gates, cos_t, sin_t, pool_v, page_table, new_rows, win_t, new_wrows, past_len, n_new):
    nb, n_pages = page_table.shape
    rows = NSA_GROUP * n_new
    w = 2 * NSA_HK
    nc = (past_len + n_new) // CMP_BLOCK
    assert past_len == n_pages * NSA_TILE and past_len % SEL_BLOCK == 0 and n_new <= CMP_BLOCK
    assert win_t.shape[2] == WINDOW and past_len >= WINDOW and n_pages % DEC_CHUNK_PAGES == 0
    assert nc == past_len // CMP_BLOCK and nc % 2 == 0 and -(-(past_len + n_new) // SEL_BLOCK) <= nc
    assert pool_v.shape[1:] == (2, w, NSA_TILE) and cos_t.shape == (NSA_HD // 2, nc)
    ckeys = DEC_CHUNK_PAGES * NSA_TILE
    blk_of_key = jnp.arange(n_pages * NSA_TILE, dtype=jnp.int32).reshape(-1, ckeys, 1) // CMP_BLOCK
    lane_of_blk = jnp.where(blk_of_key % 2 == 0, blk_of_key // 2, nc // 2 + blk_of_key // 2)
    avg = jnp.where(jnp.arange(nc, dtype=jnp.int32) == lane_of_blk, 1.0 / CMP_BLOCK, 0.0).astype(bf16)
    nr = NSA_KV_HEADS * rows
    per_b = lambda shape: pl.BlockSpec((None,) + shape, lambda b, pt: (b,) + (0,) * len(shape))
    const = lambda shape: pl.BlockSpec(shape, lambda b, pt: (0,) * len(shape), pipeline_mode=pl.Buffered(1))
    grid_spec = pltpu.PrefetchScalarGridSpec(
        num_scalar_prefetch=1, grid=(nb,),
        in_specs=[per_b((NSA_KV_HEADS, rows, NSA_HD)), per_b((NSA_KV_HEADS, rows, 3)),
                  const(cos_t.shape), const(sin_t.shape), const(avg.shape), pl.BlockSpec(memory_space=pl.ANY),
                  pl.BlockSpec((None, new_rows.shape[1], w), lambda b, pt: (b, 0, 1)),
                  per_b((w, WINDOW)), per_b((new_wrows.shape[1], w))],
        out_specs=per_b((NSA_KV_HEADS, rows, NSA_HD)),
        scratch_shapes=[pltpu.VMEM((2, w, ckeys), f32), pltpu.SemaphoreType.DMA((2,)),
                        pltpu.VMEM((w, nc), f32), pltpu.VMEM((nr, nc), bf16),
                        pltpu.VMEM((nr, 1), f32), pltpu.VMEM((nr, 1), f32), pltpu.VMEM((nr, NSA_HK), f32)])
    return pl.pallas_call(
        functools.partial(_nsa_dec_kernel, past_len=past_len, n_new=n_new), grid_spec=grid_spec,
        out_shape=jax.ShapeDtypeStruct((nb, NSA_KV_HEADS, rows, NSA_HD), f32),
        compiler_params=_cparams(("arbitrary",)), name="nsa_dec",
    )(page_table, q, gates, cos_t, sin_t, avg, pool_v, new_rows, win_t, new_wrows)


def _nsa_layer(xp, xs, nbp, nbs, pool, win, page_table, ln_g, ln_b, w_in, w_out):
    hq, hk = NSA_HQ, NSA_HK
    ws = [w_in[:, :hq].astype(bf16), w_in[:, hq:hq + 4 * hk].astype(bf16),
          w_in[:, hq + 4 * hk:hq + 6 * hk].astype(bf16), w_in[:, hq + 6 * hk:].astype(bf16)]
    wo = w_out.astype(bf16)
    eps = [_q_ep, _rows_ep, _wrows_ep, _sigmoid_ep]
    sp_len = xp.shape[0] // nbp
    ss_len = xs.shape[0] // nbs
    n_pages = page_table.shape[1]
    page = pool.shape[1]
    assert page == NSA_TILE
    past_len = n_pages * page
    ms = xs.shape[0]

    cos_s, sin_s = _rope_tables(jnp.tile(past_len + jnp.arange(ss_len, dtype=f32), nbs))
    nat_p, q_t, rows_t, wrows_t, g_t = _nsa_in(xp, nbp, w_in)
    qs, rows_s, wrows_s, gs = _proj(xs, ws, eps, aux=(cos_s, sin_s), aux_period=(ms, ms), tm=ms, name="nsa_in_s")

    def cmp_tables(nc):
        return _rope_tables(jnp.arange(nc, dtype=f32) * CMP_BLOCK + (CMP_BLOCK - 1) / 2.0)

    pages_p = sp_len // page
    pt_p = jnp.arange(nbp * pages_p, dtype=jnp.int32).reshape(nbp, pages_p)
    cmp_p = _even_odd(_nsa_cmp(nat_p.reshape(nbp * pages_p, page, 4 * hk), pt_p, *cmp_tables(sp_len // CMP_BLOCK)))
    op = _nsa_seq(q_t, g_t, cmp_p[:, :, :hk], cmp_p[:, :, hk:].transpose(0, 2, 1),
                  nat_p.reshape(nbp, sp_len, 4 * hk), rows_t, wrows_t)

    pool_v = pool.transpose(0, 2, 3, 4, 1).reshape(pool.shape[0], 2, 2 * hk, page)
    win_t = win.transpose(0, 2, 3, 4, 1).reshape(nbs, 2 * hk, win.shape[1])
    nc_s = past_len // CMP_BLOCK
    cpos = jnp.arange(nc_s, dtype=f32) * CMP_BLOCK + (CMP_BLOCK - 1) / 2.0
    cpos = jnp.concatenate([cpos[0::2], cpos[1::2]])
    inv = ROPE_THETA ** (-jnp.arange(0, NSA_HD, 2, dtype=f32) / NSA_HD)
    ang_t = inv[:, None] * cpos[None, :]
    rows_s3 = rows_s.reshape(nbs, ss_len, 4 * hk)
    wrows_s3 = wrows_s.reshape(nbs, ss_len, 2 * hk)
    new_pad = 8
    tpad = lambda a: jnp.pad(a, ((0, 0), (0, new_pad - ss_len), (0, 0)))
    to_heads = lambda a, last: (a.reshape(nbs, ss_len, NSA_KV_HEADS, NSA_GROUP, last).transpose(0, 2, 3, 1, 4)
                                .reshape(nbs, NSA_KV_HEADS, NSA_GROUP * ss_len, last))
    os_ = _nsa_dec(to_heads(qs, NSA_HD), to_heads(gs, 3), jnp.cos(ang_t), jnp.sin(ang_t), pool_v, page_table,
                   tpad(rows_s3), win_t, tpad(wrows_s3), past_len, ss_len)
    os_ = (os_.reshape(nbs, NSA_KV_HEADS, NSA_GROUP, ss_len, NSA_HD).transpose(0, 3, 1, 2, 4).reshape(ms, hq))

    xp = _outproj_ln(op.reshape(nbp * sp_len, hq), wo, xp, ln_g, ln_b, name="nsa_out")
    xs = _outproj_ln(os_, wo, xs, ln_g, ln_b, tm=ms, name="nsa_out_s")
    kv_shape = (4, NSA_KV_HEADS, NSA_HD)
    win_shape = (2, NSA_KV_HEADS, NSA_HD)
    from_t = lambda a, shape: a.reshape((a.shape[0],) + shape + (a.shape[2],)).transpose(0, 4, 1, 2, 3)
    rp = from_t(rows_t, kv_shape)
    rs = rows_s.reshape((nbs, ss_len) + kv_shape)
    wp = from_t(wrows_t[:, :, sp_len - min(WINDOW, sp_len):], win_shape)
    wsn_t = jnp.concatenate([win_t, wrows_s3.transpose(0, 2, 1)], axis=2)
    wsn = from_t(wsn_t[:, :, wsn_t.shape[2] - min(WINDOW, wsn_t.shape[2]):], win_shape)
    return xp, xs, (rp, rs, wp, wsn)


FFN_TF = 256
MOE_TF = 512
ROW_TILE = 1024


def kernel(x_prompt, x_sample, state_l0_lru_conv, state_l0_lru_h, cache_l1_nsa_kv, cache_l1_nsa_win, page_table,
           state_l2_ssd_conv, state_l2_ssd_ssm, state_l3_lru_conv, state_l3_lru_h, ln_g, ln_b, lru_w_in,
           lru_conv_w, lru_conv_b, lru_w_a, lru_b_a, lru_w_x, lru_b_x, lru_lam, lru_w_out, nsa_w_in, nsa_w_out,
           ssd_w_in, ssd_conv_w, ssd_conv_b, ssd_dt_bias, ssd_a_log, ssd_d, ssd_norm_g, ssd_w_out, ffn_w_in,
           ffn_w_out, moe_router_w, moe_router_b, moe_w_in, moe_w_out):
    nbp, sp_len, d = x_prompt.shape
    nbs, ss_len, _ = x_sample.shape
    xp = x_prompt.reshape(nbp * sp_len, d)
    xs = x_sample.reshape(nbs * ss_len, d)
    ms = xs.shape[0]
    lru_state = {0: (state_l0_lru_conv, state_l0_lru_h), 3: (state_l3_lru_conv, state_l3_lru_h)}
    new = {}
    for i in range(DEPTH):
        kind, j = i % 3, i // 3
        g0, b0 = ln_g[i, 0], ln_b[i, 0]
        if kind == 0:
            conv_s, h_s = lru_state[i]
            xp, xs, new[i] = _lru_layer(xp, xs, nbp, nbs, conv_s, h_s, g0, b0, lru_w_in[j], lru_conv_w[j],
                                        lru_conv_b[j], lru_w_a[j], lru_b_a[j], lru_w_x[j], lru_b_x[j],
                                        lru_lam[j], lru_w_out[j])
        elif kind == 1:
            xp, xs, new[i] = _nsa_layer(xp, xs, nbp, nbs, cache_l1_nsa_kv, cache_l1_nsa_win, page_table, g0, b0,
                                        nsa_w_in[j], nsa_w_out[j])
        else:
            xp, xs, new[i] = _ssd_layer(xp, xs, nbp, nbs, state_l2_ssd_conv, state_l2_ssd_ssm, g0, b0,
                                        ssd_w_in[j], ssd_conv_w[j], ssd_conv_b[j], ssd_dt_bias[j], ssd_a_log[j],
                                        ssd_d[j], ssd_norm_g[j], ssd_w_out[j])
        g1, b1 = ln_g[i, 1], ln_b[i, 1]
        k = i // 2
        if i % 2 == 0:
            xp = _ffn_ln(xp, ffn_w_in, ffn_w_out, k, g1, b1, ROW_TILE, FFN_TF, name="ffn")
            xs = _ffn_ln(xs, ffn_w_in, ffn_w_out, k, g1, b1, ms, FFN_TF, name="ffn_s")
        else:
            xp = _moe_sparse_ln(xp, moe_router_w[k], moe_router_b[k], moe_w_in, moe_w_out, k, g1, b1, MOE_TF)
            cs = _router(xs, moe_router_w[k], moe_router_b[k], tm=ms)
            xs = _moe_ln(xs, cs, moe_w_in, moe_w_out, k, g1, b1, ms, MOE_TF)
    out = [xp.reshape(nbp, sp_len, d), xs.reshape(nbs, ss_len, d)]
    for i in range(DEPTH):
        out.extend(new[i])
    return tuple(out)
```

```python
import functools
import math

import jax
import jax.numpy as jnp
from jax import lax
from jax.experimental import pallas as pl
from jax.experimental.pallas import tpu as pltpu

f32 = jnp.float32
bf16 = jnp.bfloat16

D_MODEL = 1024
DEPTH = 4
ALPHA = (2 * DEPTH) ** 0.25
LN_EPS = 1e-5
RMS_EPS = 1e-5
CONV_W = 4
NEG_INF = -1e30

D_RNN = 1344
LRU_BLOCKS = 16
LRU_BS = D_RNN // LRU_BLOCKS
LRU_C = 8.0

NSA_HEADS = 16
NSA_KV_HEADS = 4
NSA_HD = 64
NSA_GROUP = NSA_HEADS // NSA_KV_HEADS
CMP_BLOCK = 32
SEL_BLOCK = 64
SEL_TOPK = 16
WINDOW = 512
FORCE_SCORE = 1e4
ROPE_THETA = 10000.0
NSA_HQ = NSA_HEADS * NSA_HD
NSA_HK = NSA_KV_HEADS * NSA_HD

SSD_INNER = 2 * D_MODEL
SSD_HEAD_DIM = 64
SSD_HEADS = SSD_INNER // SSD_HEAD_DIM
SSD_GROUPS = 4
SSD_STATE = 128
SSD_CHUNK = 128
SSD_CONV_DIM = SSD_INNER + 2 * SSD_GROUPS * SSD_STATE

N_EXPERTS = 8
TOP_K = 2

VMEM_LIMIT_V7X = 56 * 1024 * 1024
PROJ_TM = 512


def _cparams(sem):
    return pltpu.CompilerParams(dimension_semantics=sem, vmem_limit_bytes=VMEM_LIMIT_V7X)


def _const_spec(shape):
    nd = len(shape)
    return pl.BlockSpec(shape, lambda *_: (0,) * nd, pipeline_mode=pl.Buffered(1))


def _layer_norm(v, g, b):
    mu = jnp.mean(v, -1, keepdims=True)
    d = v - mu
    var = jnp.mean(d * d, -1, keepdims=True)
    return d * lax.rsqrt(var + LN_EPS) * g + b


def _proj_kernel(*refs, epilogues, n_aux):
    n_out = len(epilogues)
    x_ref = refs[0]
    w_refs = refs[1:1 + n_out]
    aux_refs = refs[1 + n_out:1 + n_out + n_aux]
    o_refs = refs[1 + n_out + n_aux:]
    x = x_ref[...].astype(bf16)
    aux = [a[...] for a in aux_refs]
    for w_ref, o_ref, ep in zip(w_refs, o_refs, epilogues):
        y = jnp.dot(x, w_ref[...], preferred_element_type=f32)
        if ep is not None:
            y = ep(y, *aux)
        o_ref[...] = y.astype(o_ref.dtype)


def _proj(x, ws, epilogues, aux=(), aux_period=(), tm=PROJ_TM, name="proj"):
    m, k = x.shape
    assert m % tm == 0
    in_specs = [pl.BlockSpec((tm, k), lambda i: (i, 0))]
    in_specs += [_const_spec(w.shape) for w in ws]
    for a, p in zip(aux, aux_period):
        if p:
            assert p % tm == 0 and a.shape[0] == p
            in_specs.append(pl.BlockSpec((tm, a.shape[1]), functools.partial(lambda i, n: (i % n, 0), n=p // tm)))
        else:
            in_specs.append(_const_spec(a.shape))
    out_shape = [jax.ShapeDtypeStruct((m, w.shape[1]), f32) for w in ws]
    out_specs = [pl.BlockSpec((tm, w.shape[1]), lambda i: (i, 0)) for w in ws]
    return pl.pallas_call(
        functools.partial(_proj_kernel, epilogues=tuple(epilogues), n_aux=len(aux)),
        grid=(m // tm,), in_specs=in_specs, out_specs=out_specs, out_shape=out_shape,
        compiler_params=_cparams(("parallel",)), name=name,
    )(x, *ws, *aux)


def _outproj_ln_kernel(a_ref, w_ref, x_ref, g_ref, b_ref, o_ref):
    y = jnp.dot(a_ref[...].astype(bf16), w_ref[...], preferred_element_type=f32)
    o_ref[...] = _layer_norm(ALPHA * x_ref[...] + y, g_ref[...], b_ref[...])


def _outproj_ln(a, w, x, g, b, tm=PROJ_TM, name="outproj_ln"):
    m, k = a.shape
    d = w.shape[1]
    assert m % tm == 0
    return pl.pallas_call(
        _outproj_ln_kernel, grid=(m // tm,),
        in_specs=[pl.BlockSpec((tm, k), lambda i: (i, 0)), _const_spec(w.shape),
                  pl.BlockSpec((tm, d), lambda i: (i, 0)), _const_spec((1, d)), _const_spec((1, d))],
        out_specs=pl.BlockSpec((tm, d), lambda i: (i, 0)),
        out_shape=jax.ShapeDtypeStruct((m, d), f32),
        compiler_params=_cparams(("parallel",)), name=name,
    )(a, w, x, g.reshape(1, d), b.reshape(1, d))


def _ffn_ln_kernel(x_ref, wg_ref, wu_ref, wo_ref, g_ref, b_ref, o_ref, h_ref):
    j = pl.program_id(1)
    tf = wg_ref.shape[1]
    xb = x_ref[...].astype(bf16)
    hg = jnp.dot(xb, wg_ref[...].astype(bf16), preferred_element_type=f32)
    hu = jnp.dot(xb, wu_ref[...].astype(bf16), preferred_element_type=f32)
    h_ref[:, pl.ds(pl.multiple_of(j * tf, tf), tf)] = (hg * jax.nn.sigmoid(hg) * hu).astype(bf16)

    @pl.when(j == pl.num_programs(1) - 1)
    def _():
        y = jnp.dot(h_ref[...], wo_ref[...].astype(bf16), preferred_element_type=f32)
        o_ref[...] = _layer_norm(ALPHA * x_ref[...] + y, g_ref[...], b_ref[...])


def _ffn_ln(x, w_in, w_out, layer, g, b, tm, tf, name="ffn_ln"):
    m, d = x.shape
    f = w_out.shape[1]
    assert m % tm == 0 and f % tf == 0 and tf % 128 == 0
    nf = f // tf
    return pl.pallas_call(
        _ffn_ln_kernel, grid=(m // tm, nf),
        in_specs=[pl.BlockSpec((tm, d), lambda i, j: (i, 0)),
                  pl.BlockSpec((None, d, tf), lambda i, j: (layer, 0, j)),
                  pl.BlockSpec((None, d, tf), lambda i, j: (layer, 0, j + nf)),
                  pl.BlockSpec((None, f, d), lambda i, j: (layer, 0, 0), pipeline_mode=pl.Buffered(1)),
                  _const_spec((1, d)), _const_spec((1, d))],
        out_specs=pl.BlockSpec((tm, d), lambda i, j: (i, 0)),
        out_shape=jax.ShapeDtypeStruct((m, d), f32),
        scratch_shapes=[pltpu.VMEM((tm, f), bf16)],
        compiler_params=_cparams(("parallel", "arbitrary")), name=name,
    )(x, w_in, w_in, w_out, g.reshape(1, d), b.reshape(1, d))


def _router_kernel(x_ref, w_ref, b_ref, c_ref):
    logits = jnp.dot(x_ref[...], w_ref[...], precision=lax.Precision.HIGHEST,
                     preferred_element_type=f32) + b_ref[...]
    e_idx, i1, i2, w1, w2 = _top2(logits)
    c_ref[...] = jnp.where(e_idx == i1, w1, 0.0) + jnp.where(e_idx == i2, w2, 0.0)


def _router(x, w, b, tm=512):
    m, d = x.shape
    assert m % tm == 0
    return pl.pallas_call(
        _router_kernel, grid=(m // tm,),
        in_specs=[pl.BlockSpec((tm, d), lambda i: (i, 0)), _const_spec(w.shape),
                  _const_spec((1, N_EXPERTS))],
        out_specs=pl.BlockSpec((tm, N_EXPERTS), lambda i: (i, 0)),
        out_shape=jax.ShapeDtypeStruct((m, N_EXPERTS), f32),
        compiler_params=_cparams(("parallel",)), name="router",
    )(x, w, b.reshape(1, N_EXPERTS))


def _moe_ln_kernel(x_ref, c_ref, wg_ref, wu_ref, wo_ref, g_ref, b_ref, o_ref, acc_ref):
    e = pl.program_id(1)
    j = pl.program_id(2)
    x = x_ref[...]
    xb = x.astype(bf16)
    c = c_ref[...]
    ce = jnp.sum(jnp.where(lax.broadcasted_iota(jnp.int32, c.shape, 1) == e, c, 0.0), -1, keepdims=True)
    hg = jnp.dot(xb, wg_ref[...].astype(bf16), preferred_element_type=f32)
    hu = jnp.dot(xb, wu_ref[...].astype(bf16), preferred_element_type=f32)
    act = (hg * jax.nn.sigmoid(hg) * hu).astype(bf16)
    part = ce * jnp.dot(act, wo_ref[...].astype(bf16), preferred_element_type=f32)
    first = jnp.logical_and(e == 0, j == 0)

    @pl.when(first)
    def _():
        acc_ref[...] = part

    @pl.when(jnp.logical_not(first))
    def _():
        acc_ref[...] += part

    @pl.when(jnp.logical_and(e == pl.num_programs(1) - 1, j == pl.num_programs(2) - 1))
    def _():
        o_ref[...] = _layer_norm(ALPHA * x + acc_ref[...], g_ref[...], b_ref[...])


def _moe_ln(x, comb, w_in, w_out, layer, g, b, tm, tf):
    m, d = x.shape
    _, ne, f, _ = w_out.shape
    assert m % tm == 0 and f % tf == 0
    nf = f // tf
    return pl.pallas_call(
        _moe_ln_kernel, grid=(m // tm, ne, nf),
        in_specs=[pl.BlockSpec((tm, d), lambda i, e, j: (i, 0)),
                  pl.BlockSpec((tm, ne), lambda i, e, j: (i, 0)),
                  pl.BlockSpec((None, None, d, tf), lambda i, e, j: (layer, e, 0, j)),
                  pl.BlockSpec((None, None, d, tf), lambda i, e, j: (layer, e, 0, j + nf)),
                  pl.BlockSpec((None, None, tf, d), lambda i, e, j: (layer, e, j, 0)),
                  _const_spec((1, d)), _const_spec((1, d))],
        out_specs=pl.BlockSpec((tm, d), lambda i, e, j: (i, 0)),
        out_shape=jax.ShapeDtypeStruct((m, d), f32),
        scratch_shapes=[pltpu.VMEM((tm, d), f32)],
        compiler_params=_cparams(("parallel", "arbitrary", "arbitrary")), name="moe_ln",
    )(x, comb, w_in, w_in, w_out, g.reshape(1, d), b.reshape(1, d))


MOE_TM = 1024
MOE_ROUTE_TM = 512
MOE_DISPATCH_TM = 512
MOE_COMBINE_TM = 256
ROUTE_COLS = 8
ROW_DMA_UNROLL = 8


def _top2(logits):
    e_idx = lax.broadcasted_iota(jnp.int32, logits.shape, 1)
    v1 = jnp.max(logits, -1, keepdims=True)
    i1 = jnp.min(jnp.where(logits == v1, e_idx, N_EXPERTS), -1, keepdims=True)
    rest = jnp.where(e_idx == i1, -jnp.inf, logits)
    v2 = jnp.max(rest, -1, keepdims=True)
    i2 = jnp.min(jnp.where(rest == v2, e_idx, N_EXPERTS), -1, keepdims=True)
    e2 = jnp.exp(v2 - v1)
    den = 1.0 + e2
    return e_idx, i1, i2, 1.0 / den, e2 / den


def _route_kernel(x_ref, wt_ref, b_ref, r_ref, cnt_ref, carry_ref):
    i = pl.program_id(0)
    tm = x_ref.shape[0]

    @pl.when(i == 0)
    def _():
        carry_ref[...] = jnp.zeros_like(carry_ref)

    logits = lax.dot_general(wt_ref[...], x_ref[...], (((1,), (1,)), ((), ())), precision=lax.Precision.HIGHEST,
                             preferred_element_type=f32) + b_ref[...]
    e_idx = lax.broadcasted_iota(jnp.int32, logits.shape, 0)
    v1 = jnp.max(logits, 0, keepdims=True)
    i1 = jnp.min(jnp.where(logits == v1, e_idx, N_EXPERTS), 0, keepdims=True)
    rest = jnp.where(e_idx == i1, -jnp.inf, logits)
    v2 = jnp.max(rest, 0, keepdims=True)
    i2 = jnp.min(jnp.where(rest == v2, e_idx, N_EXPERTS), 0, keepdims=True)
    e2 = jnp.exp(v2 - v1)
    den = 1.0 + e2
    hit1 = e_idx == i1
    hit2 = e_idx == i2
    a = jnp.where(hit1, 1.0, 0.0) + jnp.where(hit2, 1.0, 0.0)
    earlier = lax.broadcasted_iota(jnp.int32, (tm, tm), 0) < lax.broadcasted_iota(jnp.int32, (tm, tm), 1)
    c = carry_ref[...] + jnp.dot(a.astype(bf16), jnp.where(earlier, 1.0, 0.0).astype(bf16),
                                 preferred_element_type=f32)
    pos1 = jnp.sum(jnp.where(hit1, c, 0.0), 0, keepdims=True)
    pos2 = jnp.sum(jnp.where(hit2, c, 0.0), 0, keepdims=True)
    carry_ref[...] += jnp.sum(a, 1, keepdims=True)
    rec = jnp.zeros((ROUTE_COLS, tm), f32)
    for n, v in enumerate((i1.astype(f32), i2.astype(f32), pos1, pos2, 1.0 / den, e2 / den)):
        rec = jnp.where(e_idx == n, v, rec)
    r_ref[...] = rec

    @pl.when(i == pl.num_programs(0) - 1)
    def _():
        cnt_ref[...] = carry_ref[...]


def _route(x, w, b):
    m, d = x.shape
    tm = MOE_ROUTE_TM
    assert m % tm == 0 and ROUTE_COLS == N_EXPERTS
    return pl.pallas_call(
        _route_kernel, grid=(m // tm,),
        in_specs=[pl.BlockSpec((tm, d), lambda i: (i, 0)), _const_spec((N_EXPERTS, d)), _const_spec((N_EXPERTS, 1))],
        out_specs=[pl.BlockSpec((ROUTE_COLS, tm), lambda i: (0, i)), pl.BlockSpec((N_EXPERTS, 1), lambda i: (0, 0))],
        out_shape=[jax.ShapeDtypeStruct((ROUTE_COLS, m), f32), jax.ShapeDtypeStruct((N_EXPERTS, 1), f32)],
        scratch_shapes=[pltpu.VMEM((N_EXPERTS, 1), f32)],
        compiler_params=_cparams(("arbitrary",)), name="moe_route",
    )(x, w.T, b.reshape(N_EXPERTS, 1))


def _row_copy(src, src_row, dst, dst_row, sem):
    return pltpu.make_async_copy(src.at[pl.ds(src_row, 1)], dst.at[pl.ds(dst_row, 1)], sem)


def _dispatch_kernel(d1_ref, d2_ref, x_ref, init_hbm, o_hbm, sem):
    del init_hbm
    base = pl.program_id(0) * MOE_DISPATCH_TM

    def copies(t):
        tok = base + t
        return (_row_copy(x_ref, t, o_hbm, d1_ref[tok], sem), _row_copy(x_ref, t, o_hbm, d2_ref[tok], sem))

    def issue(t, carry):
        for cp in copies(t):
            cp.start()
        return carry

    def drain(t, carry):
        for cp in copies(t):
            cp.wait()
        return carry

    lax.fori_loop(0, MOE_DISPATCH_TM, issue, 0, unroll=ROW_DMA_UNROLL)
    lax.fori_loop(0, MOE_DISPATCH_TM, drain, 0, unroll=ROW_DMA_UNROLL)


def _dispatch(x, d1, d2, n_rows):
    m, d = x.shape
    assert m % MOE_DISPATCH_TM == 0
    any_spec = pl.BlockSpec(memory_space=pl.ANY)
    grid_spec = pltpu.PrefetchScalarGridSpec(
        num_scalar_prefetch=2, grid=(m // MOE_DISPATCH_TM,),
        in_specs=[pl.BlockSpec((MOE_DISPATCH_TM, d), lambda i, *_: (i, 0)), any_spec], out_specs=any_spec,
        scratch_shapes=[pltpu.SemaphoreType.DMA(())])
    return pl.pallas_call(
        _dispatch_kernel, grid_spec=grid_spec, out_shape=jax.ShapeDtypeStruct((n_rows, d), x.dtype),
        input_output_aliases={3: 0},
        compiler_params=_cparams(("arbitrary",)), name="moe_dispatch",
    )(d1, d2, x, jnp.zeros((n_rows, d), x.dtype))


def _moe_group_kernel(te_ref, nu_ref, x_ref, wg_ref, wu_ref, wo_ref, o_ref, acc_ref):
    del te_ref
    t = pl.program_id(0)
    j = pl.program_id(1)
    used = t < nu_ref[0]
    last = j == pl.num_programs(1) - 1

    @pl.when(used)
    def _():
        xb = x_ref[...].astype(bf16)
        hg = jnp.dot(xb, wg_ref[...].astype(bf16), preferred_element_type=f32)
        hu = jnp.dot(xb, wu_ref[...].astype(bf16), preferred_element_type=f32)
        act = (hg * jax.nn.sigmoid(hg) * hu).astype(bf16)
        part = jnp.dot(act, wo_ref[...].astype(bf16), preferred_element_type=f32)

        @pl.when(j == 0)
        def _():
            acc_ref[...] = part

        @pl.when(j > 0)
        def _():
            acc_ref[...] += part

        @pl.when(last)
        def _():
            o_ref[...] = acc_ref[...]

    @pl.when(jnp.logical_and(jnp.logical_not(used), last))
    def _():
        o_ref[...] = jnp.zeros_like(o_ref)


def _moe_group(xs, tile_expert, n_used, w_in, w_out, layer, tf):
    r, d = xs.shape
    _, ne, f, _ = w_out.shape
    assert r % MOE_TM == 0 and f % tf == 0
    nf = f // tf
    n_tiles = r // MOE_TM

    def jj(t, j, nu):
        return jnp.where(t < nu[0], j, nf - 1)

    grid_spec = pltpu.PrefetchScalarGridSpec(
        num_scalar_prefetch=2, grid=(n_tiles, nf),
        in_specs=[pl.BlockSpec((MOE_TM, d), lambda t, j, te, nu: (jnp.maximum(jnp.minimum(t, nu[0] - 1), 0), 0)),
                  pl.BlockSpec((None, None, d, tf), lambda t, j, te, nu: (layer, te[t], 0, jj(t, j, nu))),
                  pl.BlockSpec((None, None, d, tf), lambda t, j, te, nu: (layer, te[t], 0, jj(t, j, nu) + nf)),
                  pl.BlockSpec((None, None, tf, d), lambda t, j, te, nu: (layer, te[t], jj(t, j, nu), 0))],
        out_specs=pl.BlockSpec((MOE_TM, d), lambda t, j, te, nu: (t, 0)),
        scratch_shapes=[pltpu.VMEM((MOE_TM, d), f32)])
    return pl.pallas_call(
        _moe_group_kernel, grid_spec=grid_spec, out_shape=jax.ShapeDtypeStruct((r, d), f32),
        compiler_params=_cparams(("arbitrary", "arbitrary")), name="moe_group",
    )(tile_expert, n_used, xs, w_in, w_in, w_out)


def _moe_combine_kernel(d1_ref, d2_ref, x_ref, r_ref, y_hbm, g_ref, b_ref, o_ref, ya_ref, yb_ref, sem):
    tm = x_ref.shape[0]
    base = pl.program_id(0) * tm

    def copies(t):
        tok = base + t
        return (_row_copy(y_hbm, d1_ref[tok], ya_ref, t, sem), _row_copy(y_hbm, d2_ref[tok], yb_ref, t, sem))

    def issue(t, carry):
        for cp in copies(t):
            cp.start()
        return carry

    def drain(t, carry):
        for cp in copies(t):
            cp.wait()
        return carry

    lax.fori_loop(0, tm, issue, 0, unroll=ROW_DMA_UNROLL)
    lax.fori_loop(0, tm, drain, 0, unroll=ROW_DMA_UNROLL)
    r = r_ref[...]
    mix = r[:, 4:5] * ya_ref[...] + r[:, 5:6] * yb_ref[...]
    o_ref[...] = _layer_norm(ALPHA * x_ref[...] + mix, g_ref[...], b_ref[...])


def _moe_combine(x, route, y, d1, d2, g, b):
    m, d = x.shape
    tm = MOE_COMBINE_TM
    assert m % tm == 0
    grid_spec = pltpu.PrefetchScalarGridSpec(
        num_scalar_prefetch=2, grid=(m // tm,),
        in_specs=[pl.BlockSpec((tm, d), lambda i, *_: (i, 0)),
                  pl.BlockSpec((tm, ROUTE_COLS), lambda i, *_: (i, 0)),
                  pl.BlockSpec(memory_space=pl.ANY),
                  pl.BlockSpec((1, d), lambda i, *_: (0, 0)), pl.BlockSpec((1, d), lambda i, *_: (0, 0))],
        out_specs=pl.BlockSpec((tm, d), lambda i, *_: (i, 0)),
        scratch_shapes=[pltpu.VMEM((tm, d), f32), pltpu.VMEM((tm, d), f32), pltpu.SemaphoreType.DMA(())])
    return pl.pallas_call(
        _moe_combine_kernel, grid_spec=grid_spec, out_shape=jax.ShapeDtypeStruct((m, d), f32),
        compiler_params=_cparams(("arbitrary",)), name="moe_combine",
    )(d1, d2, x, route, y, g.reshape(1, d), b.reshape(1, d))


def _moe_sparse_ln(x, router_w, router_b, w_in, w_out, layer, g, b, tf):
    m, d = x.shape
    route_t, counts = _route(x, router_w, router_b)
    route = route_t.T
    counts = counts[:, 0].astype(jnp.int32)
    padded = (counts + MOE_TM - 1) // MOE_TM * MOE_TM
    ends = jnp.cumsum(padded)
    offs = ends - padded
    n_tiles = -(-TOP_K * m // MOE_TM) + N_EXPERTS
    n_used = (ends[-1] // MOE_TM).reshape(1)
    tile_start = jnp.minimum(jnp.arange(n_tiles, dtype=jnp.int32), n_used[0] - 1) * MOE_TM
    tile_expert = jnp.minimum(jnp.sum((tile_start[:, None] >= ends[None, :]).astype(jnp.int32), axis=1), N_EXPERTS - 1)
    e1 = route_t[0].astype(jnp.int32)
    e2 = route_t[1].astype(jnp.int32)
    d1 = offs[e1] + route_t[2].astype(jnp.int32)
    d2 = offs[e2] + route_t[3].astype(jnp.int32)
    xs = _dispatch(x, d1, d2, n_tiles * MOE_TM)
    y = _moe_group(xs, tile_expert, n_used, w_in, w_out, layer, tf)
    return _moe_combine(x, route, y, d1, d2, g, b)


def _softplus(x):
    return jnp.maximum(x, 0.0) + jnp.log(1.0 + jnp.exp(-jnp.abs(x)))


LRU_BAND_OUT = 256


def _lru_bands():
    bands = []
    for out_lo in range(0, D_RNN, LRU_BAND_OUT):
        n = min(LRU_BAND_OUT, D_RNN - out_lo)
        in_lo = out_lo // LRU_BS * LRU_BS // 128 * 128
        in_hi = min(-(-(((out_lo + n - 1) // LRU_BS + 1) * LRU_BS) // 128) * 128, D_RNN)
        bands.append((out_lo, n, in_lo, in_hi - in_lo))
    return bands


def _band_weights(w):
    dense = _block_diag(w)
    bands = _lru_bands()
    kmax = max(b[3] for b in bands)
    out = []
    for out_lo, n, in_lo, k in bands:
        out.append(jnp.pad(dense[in_lo:in_lo + k, out_lo:out_lo + n], ((0, kmax - k), (0, LRU_BAND_OUT - n))))
    return jnp.stack(out).astype(bf16)


def _sigmoid_tanh(x):
    return 0.5 * jnp.tanh(0.5 * x) + 0.5


def _lru_gates(xc, wa_ref, ba, wx_ref, bx, sp):
    xcb = xc.astype(bf16)
    ra, rx = [], []
    for g, (_, n, in_lo, k) in enumerate(_lru_bands()):
        xin = xcb[:, in_lo:in_lo + k]
        ra.append(jnp.dot(xin, wa_ref[g, 0:k, 0:n], preferred_element_type=f32))
        rx.append(jnp.dot(xin, wx_ref[g, 0:k, 0:n], preferred_element_type=f32))
    r = _sigmoid_tanh(jnp.concatenate(ra, axis=1) + ba)
    gi = _sigmoid_tanh(jnp.concatenate(rx, axis=1) + bx)
    log_a = sp * r
    th = jnp.tanh(log_a)
    one_minus_a2 = -2.0 * th / (1.0 - th)
    return jnp.exp(log_a), jnp.sqrt(one_minus_a2) * gi * xc


def _lru_core_kernel(xb_ref, g_ref, cw_ref, cb_ref, wa_ref, ba_ref, wx_ref, bx_ref, lam_ref,
                     y_ref, hlast_ref, ext_ref, a_ref, b_ref, h_ref, *, ts, nb):
    i = pl.program_id(0)
    hist = 8

    @pl.when(i == 0)
    def _():
        ext_ref[:, 0:hist, :] = jnp.zeros((nb, hist, ext_ref.shape[2]), f32)
        h_ref[...] = jnp.zeros_like(h_ref)

    ext_ref[:, hist:hist + ts, :] = xb_ref[...]
    sp = -LRU_C * _softplus(-lam_ref[...])
    cw = cw_ref[...]
    for b in range(nb):
        xc = cb_ref[...]
        for k in range(CONV_W):
            off = hist - (CONV_W - 1) + k
            xc = xc + cw[k:k + 1, :] * ext_ref[b, off:off + ts, :]
        a, bt = _lru_gates(xc, wa_ref, ba_ref[...], wx_ref, bx_ref[...], sp)
        a_ref[b] = a
        b_ref[b] = bt
    ext_ref[:, 0:hist, :] = ext_ref[:, ts:ts + hist, :]

    def step(t, h):
        h = a_ref[:, t, :] * h + b_ref[:, t, :]
        b_ref[:, t, :] = h
        return h

    h = lax.fori_loop(0, ts, step, h_ref[...], unroll=8)
    h_ref[...] = h
    y_ref[...] = b_ref[...] * g_ref[...]

    @pl.when(i == pl.num_programs(0) - 1)
    def _():
        hlast_ref[...] = h


def _lru_core(xb, g, cw, cb, wa, ba, wx, bx, lam, ts=64):
    nb, s, c = xb.shape
    assert s % ts == 0 and ts % 8 == 0
    blk = pl.BlockSpec((nb, ts, c), lambda i: (0, i, 0))
    row = _const_spec((1, c))
    return pl.pallas_call(
        functools.partial(_lru_core_kernel, ts=ts, nb=nb), grid=(s // ts,),
        in_specs=[blk, blk, _const_spec((CONV_W, c)), row, _const_spec(wa.shape), row,
                  _const_spec(wx.shape), row, row],
        out_specs=[blk, pl.BlockSpec((nb, c), lambda i: (0, 0))],
        out_shape=[jax.ShapeDtypeStruct((nb, s, c), f32), jax.ShapeDtypeStruct((nb, c), f32)],
        scratch_shapes=[pltpu.VMEM((nb, ts + 8, c), f32), pltpu.VMEM((nb, ts, c), f32),
                        pltpu.VMEM((nb, ts, c), f32), pltpu.VMEM((nb, c), f32)],
        compiler_params=_cparams(("arbitrary",)), name="lru_core",
    )(xb, g, cw, cb.reshape(1, c), wa, ba.reshape(1, c), wx, bx.reshape(1, c), lam.reshape(1, c))


def _lru_short_kernel(xb_ref, g_ref, cs_ref, h0_ref, cw_ref, cb_ref, wa_ref, ba_ref, wx_ref, bx_ref,
                      lam_ref, y_ref, hlast_ref, *, s):
    rows = [cs_ref[k] for k in range(CONV_W - 1)] + [xb_ref[t] for t in range(s)]
    sp = -LRU_C * _softplus(-lam_ref[...])
    cw = cw_ref[...]
    h = h0_ref[...]
    for t in range(s):
        xc = cb_ref[...]
        for k in range(CONV_W):
            xc = xc + cw[k:k + 1, :] * rows[t + k]
        a, bt = _lru_gates(xc, wa_ref, ba_ref[...], wx_ref, bx_ref[...], sp)
        h = a * h + bt
        y_ref[t] = h * g_ref[t]
    hlast_ref[...] = h


def _lru_short(xb, g, cs, h0, cw, cb, wa, ba, wx, bx, lam):
    s, nb, c = xb.shape
    return pl.pallas_call(
        functools.partial(_lru_short_kernel, s=s),
        out_shape=[jax.ShapeDtypeStruct((s, nb, c), f32), jax.ShapeDtypeStruct((nb, c), f32)],
        compiler_params=pltpu.CompilerParams(vmem_limit_bytes=VMEM_LIMIT_V7X), name="lru_short",
    )(xb, g, cs, h0, cw, cb.reshape(1, c), wa, ba.reshape(1, c), wx, bx.reshape(1, c), lam.reshape(1, c))


def _block_diag(w):
    n, k, _ = w.shape
    eye = jnp.eye(n, dtype=w.dtype)
    return (eye[:, None, :, None] * w[:, :, None, :]).reshape(n * k, n * k)


def _gelu_ep(y):
    return jax.nn.gelu(y)


def _lru_layer(xp, xs, nbp, nbs, conv_s, h_s, ln_g, ln_b, w_in, cw, cb, w_a, b_a, w_x, b_x, lam, w_out):
    c = D_RNN
    w_gate = w_in[:, :c].astype(bf16)
    w_xb = w_in[:, c:].astype(bf16)
    wa = _band_weights(w_a)
    wx = _band_weights(w_x)
    wo = w_out.astype(bf16)
    sp_len = xp.shape[0] // nbp
    ss_len = xs.shape[0] // nbs

    gp, xbp = _proj(xp, [w_gate, w_xb], [_gelu_ep, None], tm=PROJ_TM, name="lru_in")
    gs, xbs = _proj(xs, [w_gate, w_xb], [_gelu_ep, None], tm=xs.shape[0], name="lru_in_s")
    xbp3 = xbp.reshape(nbp, sp_len, c)
    yp, hp = _lru_core(xbp3, gp.reshape(nbp, sp_len, c), cw, cb, wa, b_a, wx, b_x, lam)
    xbs3 = xbs.reshape(nbs, ss_len, c)
    ys_t, hs = _lru_short(xbs3.transpose(1, 0, 2), gs.reshape(nbs, ss_len, c).transpose(1, 0, 2),
                          conv_s.transpose(1, 0, 2), h_s, cw, cb, wa, b_a, wx, b_x, lam)
    ys = ys_t.transpose(1, 0, 2).reshape(nbs * ss_len, c)
    xp = _outproj_ln(yp.reshape(nbp * sp_len, c), wo, xp, ln_g, ln_b, name="lru_out")
    xs = _outproj_ln(ys, wo, xs, ln_g, ln_b, tm=xs.shape[0], name="lru_out_s")
    conv_p = xbp3[:, sp_len - (CONV_W - 1):]
    conv_sn = jnp.concatenate([conv_s, xbs3], axis=1)[:, ss_len:]
    return xp, xs, (conv_p, conv_sn, hp, hs)


SSD_GN = SSD_GROUPS * SSD_STATE
SSD_HPG = SSD_HEADS // SSD_GROUPS
SSD_DT_PAD = 128


def _ssd_core_kernel(*refs, q, valid, zero_init):
    if zero_init:
        (xbc_ref, dt_ref, zs_ref, cw_ref, cb_ref, alog_ref, dsk_ref, ng_ref, ex_ref,
         y_ref, sfin_ref, ext_ref, st_ref, yacc_ref) = refs
    else:
        (xbc_ref, dt_ref, zs_ref, hist_ref, s0_ref, cw_ref, cb_ref, alog_ref, dsk_ref, ng_ref, ex_ref,
         y_ref, sfin_ref, ext_ref, st_ref, yacc_ref) = refs
    ci = pl.program_id(1)
    hist = 8
    hp = SSD_HPG * SSD_HEAD_DIM
    pairs = [(g, j) for g in range(SSD_GROUPS) for j in range(SSD_HPG // 2)]

    @pl.when(ci == 0)
    def _():
        if zero_init:
            ext_ref[0:hist, :] = jnp.zeros((hist, ext_ref.shape[1]), f32)
            st_ref[...] = jnp.zeros_like(st_ref)
        else:
            ext_ref[0:hist, :] = hist_ref[...]
            for g, j in pairs:
                h = g * SSD_HPG + 2 * j
                two = jnp.concatenate([s0_ref[h], s0_ref[h + 1]], axis=0)
                st_ref[g, :, 2 * SSD_HEAD_DIM * j:2 * SSD_HEAD_DIM * (j + 1)] = two.T

    ext_ref[hist:hist + q, :] = xbc_ref[...]
    cw = cw_ref[...]
    xc = cb_ref[...]
    for k in range(CONV_W):
        off = hist - (CONV_W - 1) + k
        xc = xc + cw[k:k + 1, :] * ext_ref[off:off + q, :]
    xc = xc * jax.nn.sigmoid(xc)
    if q >= hist:
        ext_ref[0:hist, :] = ext_ref[q:q + hist, :]

    row = lax.broadcasted_iota(jnp.int32, (q, q), 0)
    col = lax.broadcasted_iota(jnp.int32, (q, q), 1)
    causal = col <= row
    dt = dt_ref[...]
    if valid < q:
        dt = jnp.where(lax.broadcasted_iota(jnp.int32, dt.shape, 0) < valid, dt, 0.0)
    a_neg = -jnp.exp(alog_ref[...])
    tril = jnp.where(causal, 1.0, 0.0)
    acum = jnp.dot(tril, dt * a_neg, precision=lax.Precision.HIGHEST, preferred_element_type=f32)
    acum_t = acum.T
    last = acum[q - 1:q, :]

    def per_channel(v):
        hi = v.astype(bf16)
        r1 = v - hi.astype(f32)
        mid = r1.astype(bf16)
        lo = (r1 - mid.astype(f32)).astype(bf16)
        ex = ex_ref[...]
        return (jnp.dot(hi, ex, preferred_element_type=f32) + jnp.dot(mid, ex, preferred_element_type=f32)
                + jnp.dot(lo, ex, preferred_element_type=f32))

    e_acum_x = per_channel(jnp.exp(acum))
    xdt_all = xc[:, :SSD_INNER] * per_channel(dt)
    xdt_b = xdt_all.astype(bf16)
    xdd_b = (xdt_all * per_channel(jnp.exp(last - acum))).astype(bf16)
    dec_chunk_x = e_acum_x[q - 1:q, :]
    two = 2 * SSD_HEAD_DIM
    first_head = lax.broadcasted_iota(jnp.int32, (q, two), 1) < SSD_HEAD_DIM
    nt = (((1,), (1,)), ((), ()))

    for g in range(SSD_GROUPS):
        bm = xc[:, SSD_INNER + g * SSD_STATE:SSD_INNER + (g + 1) * SSD_STATE]
        cm = xc[:, SSD_INNER + SSD_GN + g * SSD_STATE:SSD_INNER + SSD_GN + (g + 1) * SSD_STATE].astype(bf16)
        cb = lax.dot_general(cm, bm.astype(bf16), nt, preferred_element_type=f32)
        ch = slice(g * hp, (g + 1) * hp)
        st = st_ref[g]
        y_off = jnp.dot(cm, st.astype(bf16), preferred_element_type=f32) * e_acum_x[:, ch]
        st_ref[g] = dec_chunk_x[:, ch] * st + jnp.dot(bm.T.astype(bf16), xdd_b[:, ch], preferred_element_type=f32)
        for j in range(SSD_HPG // 2):
            h = g * SSD_HPG + 2 * j
            lanes = slice(h * SSD_HEAD_DIM, (h + 2) * SSD_HEAD_DIM)
            m = []
            for hh in (h, h + 1):
                seg = acum[:, hh:hh + 1] - acum_t[hh:hh + 1, :]
                m.append((cb * jnp.exp(jnp.where(causal, seg, NEG_INF))).astype(bf16))
            if q % 128 == 0:
                x2 = xdt_b[:, lanes]
                zero = jnp.zeros_like(x2)
                rhs = jnp.concatenate([jnp.where(first_head, x2, zero), jnp.where(first_head, zero, x2)], axis=0)
                y_diag = jnp.dot(jnp.concatenate(m, axis=1), rhs, preferred_element_type=f32)
            else:
                y_diag = jnp.concatenate(
                    [jnp.dot(m[n], xdt_b[:, (h + n) * SSD_HEAD_DIM:(h + n + 1) * SSD_HEAD_DIM],
                             preferred_element_type=f32) for n in range(2)], axis=1)
            yacc_ref[:, lanes] = y_diag + y_off[:, 2 * j * SSD_HEAD_DIM:(2 * j + 2) * SSD_HEAD_DIM]

    y = yacc_ref[...] + dsk_ref[...] * xc[:, :SSD_INNER]
    yg = y * zs_ref[...]
    gw = SSD_INNER // SSD_GROUPS
    for g in range(SSD_GROUPS):
        v = yg[:, g * gw:(g + 1) * gw]
        v = v * lax.rsqrt(jnp.mean(v * v, -1, keepdims=True) + RMS_EPS)
        y_ref[:, g * gw:(g + 1) * gw] = v * ng_ref[:, g * gw:(g + 1) * gw]

    @pl.when(ci == pl.num_programs(1) - 1)
    def _():
        for g, j in pairs:
            h = g * SSD_HPG + 2 * j
            both = st_ref[g, :, two * j:two * (j + 1)].T
            sfin_ref[h] = both[:SSD_HEAD_DIM]
            sfin_ref[h + 1] = both[SSD_HEAD_DIM:]


def _ssd_core(xbc, dt, zs, hist, s0, cw, cb, a_log, d_skip, norm_g, q, valid):
    nb, s, cd = xbc.shape
    assert s % q == 0
    zero_init = s0 is None
    blk = lambda w: pl.BlockSpec((None, q, w), lambda b, c: (b, c, 0))
    st_spec = pl.BlockSpec((None, SSD_HEADS, SSD_HEAD_DIM, SSD_STATE), lambda b, c: (b, 0, 0, 0))
    in_specs = [blk(cd), blk(SSD_DT_PAD), blk(SSD_INNER)]
    args = [xbc, dt, zs]
    if not zero_init:
        in_specs += [pl.BlockSpec((None, 8, cd), lambda b, c: (b, 0, 0)), st_spec]
        args += [hist, s0]
    in_specs += [_const_spec((CONV_W, cd)), _const_spec((1, cd)), _const_spec((1, SSD_DT_PAD)),
                 _const_spec((1, SSD_INNER)), _const_spec((1, SSD_INNER)), _const_spec((SSD_DT_PAD, SSD_INNER))]
    pad_h = SSD_DT_PAD - SSD_HEADS
    head_to_channels = (jnp.arange(SSD_DT_PAD, dtype=jnp.int32)[:, None]
                        == jnp.arange(SSD_INNER, dtype=jnp.int32)[None, :] // SSD_HEAD_DIM).astype(bf16)
    args += [cw, cb.reshape(1, cd), jnp.pad(a_log, (0, pad_h)).reshape(1, SSD_DT_PAD),
             jnp.repeat(d_skip, SSD_HEAD_DIM).reshape(1, SSD_INNER), norm_g.reshape(1, SSD_INNER),
             head_to_channels]
    return pl.pallas_call(
        functools.partial(_ssd_core_kernel, q=q, valid=valid, zero_init=zero_init),
        grid=(nb, s // q), in_specs=in_specs,
        out_specs=[blk(SSD_INNER), st_spec],
        out_shape=[jax.ShapeDtypeStruct((nb, s, SSD_INNER), f32),
                   jax.ShapeDtypeStruct((nb, SSD_HEADS, SSD_HEAD_DIM, SSD_STATE), f32)],
        scratch_shapes=[pltpu.VMEM((q + 8, cd), f32),
                        pltpu.VMEM((SSD_GROUPS, SSD_STATE, SSD_HPG * SSD_HEAD_DIM), f32),
                        pltpu.VMEM((q, SSD_INNER), f32)],
        compiler_params=_cparams(("parallel", "arbitrary")), name="ssd_core",
    )(*args)


def _silu_ep(y, *_):
    return y * jax.nn.sigmoid(y)


def _dt_ep(y, bias):
    return _softplus(y + bias)


def _pass_ep(y, *_):
    return y


def _ssd_layer(xp, xs, nbp, nbs, conv_s, st_s, ln_g, ln_b, w_in, cw, cb, dt_bias, a_log, d_skip, norm_g, w_out):
    cd = SSD_CONV_DIM
    pad_h = SSD_DT_PAD - SSD_HEADS
    w_z = w_in[:, :SSD_INNER].astype(bf16)
    w_xbc = w_in[:, SSD_INNER:SSD_INNER + cd].astype(bf16)
    w_dt = jnp.pad(w_in[:, SSD_INNER + cd:], ((0, 0), (0, pad_h))).astype(bf16)
    bias = jnp.pad(dt_bias, (0, pad_h)).reshape(1, SSD_DT_PAD)
    wo = w_out.astype(bf16)
    sp_len = xp.shape[0] // nbp
    ss_len = xs.shape[0] // nbs
    eps = [_silu_ep, _pass_ep, _dt_ep]

    zsp, xbcp, dtp = _proj(xp, [w_z, w_xbc, w_dt], eps, aux=(bias,), aux_period=(0,), tm=PROJ_TM, name="ssd_in")
    zss, xbcs, dts = _proj(xs, [w_z, w_xbc, w_dt], eps, aux=(bias,), aux_period=(0,), tm=xs.shape[0], name="ssd_in_s")
    xbcp3 = xbcp.reshape(nbp, sp_len, cd)
    yp, sp_fin = _ssd_core(xbcp3, dtp.reshape(nbp, sp_len, SSD_DT_PAD), zsp.reshape(nbp, sp_len, SSD_INNER),
                           None, None, cw, cb, a_log, d_skip, norm_g, q=math.gcd(sp_len, SSD_CHUNK), valid=SSD_CHUNK)
    qs = 8
    assert ss_len <= qs
    tpad = lambda a: jnp.pad(a, ((0, 0), (0, qs - ss_len), (0, 0)))
    xbcs3 = xbcs.reshape(nbs, ss_len, cd)
    hist = jnp.pad(conv_s, ((0, 0), (8 - (CONV_W - 1), 0), (0, 0)))
    ys, ss_fin = _ssd_core(tpad(xbcs3), tpad(dts.reshape(nbs, ss_len, SSD_DT_PAD)),
                           tpad(zss.reshape(nbs, ss_len, SSD_INNER)), hist, st_s,
                           cw, cb, a_log, d_skip, norm_g, q=qs, valid=ss_len)
    ys = ys[:, :ss_len].reshape(nbs * ss_len, SSD_INNER)
    xp = _outproj_ln(yp.reshape(nbp * sp_len, SSD_INNER), wo, xp, ln_g, ln_b, name="ssd_out")
    xs = _outproj_ln(ys, wo, xs, ln_g, ln_b, tm=xs.shape[0], name="ssd_out_s")
    conv_p = xbcp3[:, sp_len - (CONV_W - 1):]
    conv_sn = jnp.concatenate([conv_s, xbcs3], axis=1)[:, ss_len:]
    return xp, xs, (conv_p, conv_sn, sp_fin, ss_fin)


NSA_TILE = 128
NSA_SCALE = NSA_HD ** -0.5


def _rope_tables(pos):
    inv = ROPE_THETA ** (-jnp.arange(0, NSA_HD, 2, dtype=f32) / NSA_HD)
    ang = pos[:, None] * inv[None, :]
    cos, sin = jnp.cos(ang), jnp.sin(ang)
    return jnp.tile(cos, (1, 4)), jnp.tile(jnp.concatenate([-sin, sin], axis=1), (1, 2))


def _rope_cols(y, cos, sgn_sin, blocks):
    lane = lax.broadcasted_iota(jnp.int32, (y.shape[0], 128), 1)
    first_half = (lane % NSA_HD) < NSA_HD // 2
    out = []
    for c in range(y.shape[1] // 128):
        blk = y[:, 128 * c:128 * (c + 1)]
        if c in blocks:
            partner = jnp.where(first_half, pltpu.roll(blk, 128 - NSA_HD // 2, 1), pltpu.roll(blk, NSA_HD // 2, 1))
            blk = blk * cos + partner * sgn_sin
        out.append(blk)
    return jnp.concatenate(out, axis=1)


def _q_ep(y, cos, sin):
    return _rope_cols(y, cos, sin, range(NSA_HQ // 128)) * NSA_SCALE


def _rows_ep(y, cos, sin):
    return _rope_cols(y, cos, sin, (4, 5))


def _wrows_ep(y, cos, sin):
    return _rope_cols(y, cos, sin, (0, 1))


def _sigmoid_ep(y, *_):
    return jax.nn.sigmoid(y)


def _nsa_cmp_kernel(pt_ref, pa_ref, pb_ref, cos_ref, sin_ref, o_ref):
    del pt_ref
    per_page = NSA_TILE // CMP_BLOCK
    for n, ref in enumerate((pa_ref, pb_ref)):
        m = ref[...].reshape(per_page, CMP_BLOCK, ref.shape[1]).sum(1) * (1.0 / CMP_BLOCK)
        lo, hi = n * per_page, (n + 1) * per_page
        o_ref[lo:hi, :] = _rope_cols(m, cos_ref[lo:hi, :], sin_ref[lo:hi, :], (0, 1))


def _nsa_cmp(pool, page_table, cos, sin):
    nb, n_pages = page_table.shape
    assert n_pages % 2 == 0
    w = 2 * NSA_HK
    per_step = 2 * NSA_TILE // CMP_BLOCK
    grid_spec = pltpu.PrefetchScalarGridSpec(
        num_scalar_prefetch=1, grid=(nb, n_pages // 2),
        in_specs=[pl.BlockSpec((None, NSA_TILE, w), lambda b, p, pt: (pt[b, 2 * p], 0, 0)),
                  pl.BlockSpec((None, NSA_TILE, w), lambda b, p, pt: (pt[b, 2 * p + 1], 0, 0)),
                  pl.BlockSpec((per_step, 128), lambda b, p, pt: (p, 0)),
                  pl.BlockSpec((per_step, 128), lambda b, p, pt: (p, 0))],
        out_specs=pl.BlockSpec((None, per_step, w), lambda b, p, pt: (b, p, 0)))
    return pl.pallas_call(
        _nsa_cmp_kernel, grid_spec=grid_spec,
        out_shape=jax.ShapeDtypeStruct((nb, n_pages * NSA_TILE // CMP_BLOCK, w), f32),
        compiler_params=_cparams(("parallel", "arbitrary")), name="nsa_cmp",
    )(page_table, pool, pool, cos, sin)


def _even_odd(cmp):
    nb, nc, w = cmp.shape
    return cmp.reshape(nb, nc // 2, 2, w).transpose(0, 2, 1, 3).reshape(nb, nc, w)


def _topk_mask_rows(score_t, k):
    n = score_t.shape[0]
    assert n % 8 == 0
    pieces = [score_t[8 * v:8 * v + 8] for v in range(n // 8)]
    ridx = lax.broadcasted_iota(jnp.int32, pieces[0].shape, 0)
    cnts = [jnp.zeros(pieces[0].shape, f32) for _ in pieces]
    for i in range(n):
        si = score_t[i:i + 1, :]
        for v, pc in enumerate(pieces):
            if 8 * v > i:
                cnts[v] = cnts[v] + jnp.where(si >= pc, 1.0, 0.0)
            elif 8 * v + 7 < i:
                cnts[v] = cnts[v] + jnp.where(si > pc, 1.0, 0.0)
            else:
                tie_wins = jnp.where(ridx + 8 * v > i, 1.0, 0.0)
                cnts[v] = cnts[v] + jnp.where(si > pc, 1.0, 0.0) + jnp.where(si == pc, tie_wins, 0.0)
    return jnp.concatenate([jnp.where(c < k, 1.0, 0.0) for c in cnts], axis=0)


def _topk_mask_lanes(score, k, n_real):
    lane = lax.broadcasted_iota(jnp.int32, score.shape, 1)
    cnt = jnp.zeros(score.shape, f32)
    for i in range(n_real):
        si = score[:, i:i + 1]
        cnt = cnt + jnp.where(si > score, 1.0, 0.0) + jnp.where(si == score, jnp.where(lane > i, 1.0, 0.0), 0.0)
    return jnp.where(cnt < k, 1.0, 0.0)


def _softmax_rows(s):
    e = jnp.exp(s - jnp.max(s, -1, keepdims=True))
    return e / jnp.sum(e, -1, keepdims=True)


def _rope_rows(y, cos_t, sin_t, heads):
    half = NSA_HD // 2
    out = []
    for h in range(heads):
        top = y[NSA_HD * h:NSA_HD * h + half]
        bot = y[NSA_HD * h + half:NSA_HD * (h + 1)]
        out += [top * cos_t - bot * sin_t, bot * cos_t + top * sin_t]
    return jnp.concatenate(out, axis=0)


def _nsa_in_kernel(x_ref, wn_ref, wq_ref, wr_ref, ww_ref, wg_ref, cos_ref, sin_ref, cost_ref, sint_ref,
                   nat_ref, qt_ref, rt_ref, wt_ref, gt_ref):
    x = x_ref[...].astype(bf16)
    nat = jnp.dot(x, wn_ref[...], preferred_element_type=f32)
    nat_ref[...] = _rope_cols(nat, cos_ref[...], sin_ref[...], (4, 5, 6, 7))
    nt = (((1,), (1,)), ((), ()))
    cos_t, sin_t = cost_ref[...], sint_ref[...]
    q_t = lax.dot_general(wq_ref[...], x, nt, preferred_element_type=f32)
    qt_ref[...] = _rope_rows(q_t, cos_t, sin_t, NSA_HEADS) * NSA_SCALE
    r_t = lax.dot_general(wr_ref[...], x, nt, preferred_element_type=f32)
    rt_ref[0:2 * NSA_HK, :] = r_t[0:2 * NSA_HK]
    rt_ref[2 * NSA_HK:3 * NSA_HK, :] = _rope_rows(r_t[2 * NSA_HK:3 * NSA_HK], cos_t, sin_t, NSA_KV_HEADS)
    rt_ref[3 * NSA_HK:4 * NSA_HK, :] = r_t[3 * NSA_HK:4 * NSA_HK]
    w_t = lax.dot_general(ww_ref[...], x, nt, preferred_element_type=f32)
    wt_ref[0:NSA_HK, :] = _rope_rows(w_t[0:NSA_HK], cos_t, sin_t, NSA_KV_HEADS)
    wt_ref[NSA_HK:2 * NSA_HK, :] = w_t[NSA_HK:2 * NSA_HK]
    gt_ref[...] = jax.nn.sigmoid(lax.dot_general(wg_ref[...], x, nt, preferred_element_type=f32))


def _nsa_in(x, nb, w_in, tm=PROJ_TM):
    m, d = x.shape
    s_len = m // nb
    assert s_len % tm == 0
    hq, hk = NSA_HQ, NSA_HK
    w_q = w_in[:, :hq]
    w_kv = w_in[:, hq:hq + 6 * hk].reshape(d, 6, hk)
    w_g = w_in[:, hq + 6 * hk:]
    w_nat = w_kv[:, jnp.array([0, 1, 2, 4])].reshape(d, 4 * hk).astype(bf16)
    w_q_t = w_q.T.astype(bf16)
    w_rows_t = w_kv[:, 0:4].reshape(d, 4 * hk).T.astype(bf16)
    w_win_t = w_kv[:, 4:6].reshape(d, 2 * hk).T.astype(bf16)
    w_g_t = w_g.T.astype(bf16)
    pos = jnp.arange(s_len, dtype=f32)
    cos, sin = _rope_tables(pos)
    inv = ROPE_THETA ** (-jnp.arange(0, NSA_HD, 2, dtype=f32) / NSA_HD)
    ang_t = inv[:, None] * pos[None, :]
    nt = s_len // tm
    ng = 3 * NSA_HEADS
    t_spec = lambda r: pl.BlockSpec((None, r, tm), lambda b, i: (b, 0, i))
    tab = pl.BlockSpec((tm, 128), lambda b, i: (i, 0))
    tab_t = pl.BlockSpec((NSA_HD // 2, tm), lambda b, i: (0, i))
    return pl.pallas_call(
        _nsa_in_kernel, grid=(nb, nt),
        in_specs=[pl.BlockSpec((tm, d), lambda b, i: (b * nt + i, 0)),
                  _const_spec(w_nat.shape), _const_spec(w_q_t.shape), _const_spec(w_rows_t.shape),
                  _const_spec(w_win_t.shape), _const_spec(w_g_t.shape), tab, tab, tab_t, tab_t],
        out_specs=[pl.BlockSpec((tm, 4 * hk), lambda b, i: (b * nt + i, 0)), t_spec(hq), t_spec(4 * hk),
                   t_spec(2 * hk), t_spec(ng)],
        out_shape=[jax.ShapeDtypeStruct((m, 4 * hk), f32), jax.ShapeDtypeStruct((nb, hq, s_len), f32),
                   jax.ShapeDtypeStruct((nb, 4 * hk, s_len), f32), jax.ShapeDtypeStruct((nb, 2 * hk, s_len), f32),
                   jax.ShapeDtypeStruct((nb, ng, s_len), f32)],
        compiler_params=_cparams(("parallel", "parallel")), name="nsa_in",
    )(x, w_nat, w_q_t, w_rows_t, w_win_t, w_g_t, cos, sin, jnp.cos(ang_t), jnp.sin(ang_t))


def _flash_cols(k_tile, v_t, q_ref, bias, m_ref, l_ref, acc_ref):
    qw = NSA_TILE
    for g in range(q_ref.shape[1] // qw):
        c = slice(g * qw, (g + 1) * qw)
        s = jnp.dot(k_tile, q_ref[:, c], preferred_element_type=f32)
        if bias is not None:
            s = s + bias
        m_old = m_ref[:, c]
        m_new = jnp.maximum(m_old, jnp.max(s, 0, keepdims=True))
        alpha = jnp.exp(m_old - m_new)
        p = jnp.exp(s - m_new)
        l_ref[:, c] = alpha * l_ref[:, c] + jnp.sum(p, 0, keepdims=True)
        m_ref[:, c] = m_new
        acc_ref[:, c] = alpha * acc_ref[:, c] + jnp.dot(v_t, p.astype(bf16), preferred_element_type=f32)


def _nsa_seq_kernel(qt_ref, gt_ref, kcb_ref, vcbt_ref, knat_ref, vst_ref, vwt_ref, o_ref,
                    q_sc, oc_sc, os_sc, m_sc, l_sc, acc_sc, *, s_len):
    i = pl.program_id(1)
    qb = NSA_TILE
    nc = s_len // CMP_BLOCK
    nsb = s_len // SEL_BLOCK
    cols = NSA_GROUP * qb
    tile4 = lambda a: jnp.concatenate([a] * NSA_GROUP, axis=1)
    t_q = i * qb + lax.broadcasted_iota(jnp.int32, (1, qb), 1)
    t_cols = tile4(t_q)
    cl = lax.broadcasted_iota(jnp.int32, (nc, 1), 0)
    cblk = jnp.where(cl < nc // 2, 2 * cl, 2 * (cl - nc // 2) + 1)
    cmask = (cblk + 1) * CMP_BLOCK - 1 <= t_cols
    has_cmp = jnp.where(t_cols >= CMP_BLOCK - 1, 1.0, 0.0)
    blk = lax.broadcasted_iota(jnp.int32, (nsb, 1), 0)
    valid = blk * SEL_BLOCK <= t_q
    forced = (blk == 0) | (blk == t_q // SEL_BLOCK)
    key_in = lax.broadcasted_iota(jnp.int32, (NSA_TILE, 1), 0)
    blk_of_key = lax.broadcasted_iota(jnp.int32, (NSA_TILE, 2 * NSA_HD), 0) // SEL_BLOCK
    lane = lax.broadcasted_iota(jnp.int32, (NSA_TILE, 2 * NSA_HD), 1)
    own_lanes = [lane < NSA_HD, lane >= NSA_HD]
    gates = gt_ref[...]
    kcols = [slice(NSA_HD * k, NSA_HD * (k + 1)) for k in range(NSA_KV_HEADS)]
    wcols = [slice(NSA_HK + NSA_HD * k, NSA_HK + NSA_HD * (k + 1)) for k in range(NSA_KV_HEADS)]

    def reset():
        m_sc[...] = jnp.full(m_sc.shape, NEG_INF, f32)
        l_sc[...] = jnp.zeros_like(l_sc)
        acc_sc[...] = jnp.zeros_like(acc_sc)

    for k in range(NSA_KV_HEADS):
        kcol = kcols[k]
        q_t = jnp.concatenate([qt_ref[NSA_HD * (NSA_GROUP * k + g):NSA_HD * (NSA_GROUP * k + g + 1), :]
                               for g in range(NSA_GROUP)], axis=1).astype(bf16)

        s_c = jnp.dot(kcb_ref[:, kcol].astype(bf16), q_t, preferred_element_type=f32)
        s_c = jnp.where(cmask, s_c, NEG_INF)
        e = jnp.exp(s_c - jnp.max(s_c, 0, keepdims=True))
        p_c = e / jnp.sum(e, 0, keepdims=True) * has_cmp
        o_c = jnp.dot(vcbt_ref[kcol, :].astype(bf16), p_c.astype(bf16), preferred_element_type=f32)

        imp = p_c[:, 0:qb]
        for g in range(1, NSA_GROUP):
            imp = imp + p_c[:, g * qb:(g + 1) * qb]
        imp = imp[:nc // 2] + imp[nc // 2:]
        score = jnp.where(forced, FORCE_SCORE, jnp.where(valid, imp, -1.0))
        sel = _topk_mask_rows(score, SEL_TOPK)
        sel_bias = tile4(jnp.where(sel > 0.5, 0.0, NEG_INF)).astype(bf16)
        fill = jnp.zeros((NSA_HD - nsb, cols), bf16)
        q_sc[k] = jnp.concatenate([q_t, sel_bias, fill] if k % 2 == 0 else [sel_bias, fill, q_t], axis=0)
        oc_sc[k] = o_c

    def sel_tile(j, bias):
        off = pl.multiple_of(j * NSA_TILE, NSA_TILE)
        keys = pl.ds(off, NSA_TILE)
        blk_key = blk_of_key + j * (NSA_TILE // SEL_BLOCK)
        onehot = [jnp.where(lane - NSA_HD == blk_key, 1.0, 0.0).astype(bf16),
                  jnp.where(lane == blk_key, 1.0, 0.0).astype(bf16)]
        for pair in range(NSA_KV_HEADS // 2):
            k_both = knat_ref[keys, 2 * NSA_HD * pair:2 * NSA_HD * (pair + 1)].astype(bf16)
            for own in range(2):
                k = 2 * pair + own
                k_aug = jnp.where(own_lanes[own], k_both, onehot[own])
                _flash_cols(k_aug, vst_ref[kcols[k], keys].astype(bf16), q_sc.at[k], bias,
                            m_sc.at[k], l_sc.at[k], acc_sc.at[k])

    def win_tile(j, bias):
        off = pl.multiple_of(j * NSA_TILE, NSA_TILE)
        keys = pl.ds(off, NSA_TILE)
        for k in range(NSA_KV_HEADS):
            _flash_cols(knat_ref[keys, wcols[k]].astype(bf16), vwt_ref[kcols[k], keys].astype(bf16),
                        q_sc.at[k, pl.ds(NSA_HD * (k % 2), NSA_HD)], bias, m_sc.at[k], l_sc.at[k], acc_sc.at[k])

    def causal_bias(j):
        return jnp.where(j * NSA_TILE + key_in <= t_q, 0.0, NEG_INF)

    def window_bias(j):
        kpos = j * NSA_TILE + key_in
        return jnp.where((kpos <= t_q) & (kpos > t_q - WINDOW), 0.0, NEG_INF)

    def loop(tile_fn, lo, hi):
        def body(j, carry):
            tile_fn(j, None)
            return carry
        lax.fori_loop(lo, hi, body, 0)

    reset()
    loop(sel_tile, 0, i)
    sel_tile(i, causal_bias(i))
    os_sc[...] = acc_sc[...] / l_sc[...]

    reset()
    n_back = WINDOW // NSA_TILE

    @pl.when(i >= n_back)
    def _():
        win_tile(i - n_back, window_bias(i - n_back))

    loop(win_tile, jnp.maximum(i - n_back + 1, 0), i)
    win_tile(i, window_bias(i))

    for k in range(NSA_KV_HEADS):
        o_c = oc_sc[k]
        o_s = os_sc[k]
        o_w = acc_sc[k] / l_sc[k]
        for pair in range(NSA_GROUP // 2):
            o_t = []
            for g in (2 * pair, 2 * pair + 1):
                h = NSA_GROUP * k + g
                c = slice(g * qb, (g + 1) * qb)
                o_t.append(gates[3 * h:3 * h + 1, :] * o_c[:, c] + gates[3 * h + 1:3 * h + 2, :] * o_s[:, c]
                           + gates[3 * h + 2:3 * h + 3, :] * o_w[:, c])
            lane0 = NSA_HD * (NSA_GROUP * k + 2 * pair)
            o_ref[:, lane0:lane0 + 2 * NSA_HD] = jnp.concatenate(o_t, axis=0).T


def _nsa_seq(q_t, gates_t, kcb, vcb_t, nat, rows_t, wrows_t):
    nb, _, s_len = q_t.shape
    assert s_len % NSA_TILE == 0 and (s_len // CMP_BLOCK) % 2 == 0 and (s_len // SEL_BLOCK) % 8 == 0
    assert s_len // SEL_BLOCK <= NSA_HD and 2 * NSA_HD == NSA_TILE
    nc = s_len // CMP_BLOCK
    hk = NSA_HK
    cols = NSA_GROUP * NSA_TILE
    per_b = lambda shape, idx: pl.BlockSpec((None,) + shape, lambda b, i: (b,) + idx)
    return pl.pallas_call(
        functools.partial(_nsa_seq_kernel, s_len=s_len), grid=(nb, s_len // NSA_TILE),
        in_specs=[pl.BlockSpec((None, NSA_HQ, NSA_TILE), lambda b, i: (b, 0, i)),
                  pl.BlockSpec((None, 3 * NSA_HEADS, NSA_TILE), lambda b, i: (b, 0, i)),
                  per_b((nc, hk), (0, 0)), per_b((hk, nc), (0, 0)),
                  per_b((s_len, 2 * hk), (0, 1)),
                  per_b((hk, s_len), (3, 0)),
                  per_b((hk, s_len), (1, 0))],
        out_specs=pl.BlockSpec((None, NSA_TILE, NSA_HQ), lambda b, i: (b, i, 0)),
        out_shape=jax.ShapeDtypeStruct((nb, s_len, NSA_HQ), f32),
        scratch_shapes=[pltpu.VMEM((NSA_KV_HEADS, 2 * NSA_HD, cols), bf16),
                        pltpu.VMEM((NSA_KV_HEADS, NSA_HD, cols), f32), pltpu.VMEM((NSA_KV_HEADS, NSA_HD, cols), f32),
                        pltpu.VMEM((NSA_KV_HEADS, 1, cols), f32), pltpu.VMEM((NSA_KV_HEADS, 1, cols), f32),
                        pltpu.VMEM((NSA_KV_HEADS, NSA_HD, cols), f32)],
        compiler_params=_cparams(("parallel", "arbitrary")), name="nsa_seq",
    )(q_t, gates_t, kcb, vcb_t, nat, rows_t, wrows_t)


DEC_CHUNK_PAGES = 8


def _nsa_dec_kernel(pt_ref, q_ref, g_ref, cos_ref, sin_ref, avg_ref, pool_hbm, new_ref, win_ref, wnew_ref, o_ref,
                    buf, sems, cmp_sc, bias_sc, m_sc, l_sc, acc_sc, *, past_len, n_new):
    b = pl.program_id(0)
    rows = NSA_GROUP * n_new
    t_len = past_len + n_new
    nc = t_len // CMP_BLOCK
    nsb = -(-t_len // SEL_BLOCK)
    n_pages = past_len // NSA_TILE
    cpg = DEC_CHUNK_PAGES
    n_chunks = n_pages // cpg
    ckeys = cpg * NSA_TILE
    new_pad = new_ref.shape[0]

    def page_copy(c, i):
        half = c // n_chunks
        page = pt_ref[b, (c % n_chunks) * cpg + i]
        return pltpu.make_async_copy(pool_hbm.at[page, half], buf.at[c % 2, :, pl.ds(i * NSA_TILE, NSA_TILE)],
                                     sems.at[c % 2])

    def start_chunk(c):
        for i in range(cpg):
            page_copy(c, i).start()

    def wait_chunk(c):
        for i in range(cpg):
            page_copy(c, i).wait()

    start_chunk(0)
    cmp_sc[...] = jnp.zeros_like(cmp_sc)

    def pass1(c, carry):
        start_chunk(c + 1)
        wait_chunk(c)
        x = buf[c % 2]
        avg = avg_ref[c]
        hi = x.astype(bf16)
        r1 = x - hi.astype(f32)
        mid = r1.astype(bf16)
        lo = (r1 - mid.astype(f32)).astype(bf16)
        cmp_sc[...] += (jnp.dot(hi, avg, preferred_element_type=f32) + jnp.dot(mid, avg, preferred_element_type=f32)
                        + jnp.dot(lo, avg, preferred_element_type=f32))
        return carry

    lax.fori_loop(0, n_chunks, pass1, 0)

    nr = NSA_KV_HEADS * rows
    rid = lax.broadcasted_iota(jnp.int32, (nr, 1), 0)
    t_all = past_len + rid % n_new
    zero_q = jnp.zeros((rows, NSA_HD), bf16)
    q_bd = jnp.concatenate(
        [jnp.concatenate([q_ref[k].astype(bf16) if kk == k else zero_q for kk in range(NSA_KV_HEADS)], axis=1)
         for k in range(NSA_KV_HEADS)], axis=0)
    nt = (((1,), (1,)), ((), ()))

    cl = lax.broadcasted_iota(jnp.int32, (1, nc), 1)
    cblk = jnp.where(cl < nc // 2, 2 * cl, 2 * (cl - nc // 2) + 1)
    cmask = (cblk + 1) * CMP_BLOCK - 1 <= t_all
    nd = NSA_KV_HEADS * n_new
    same = lambda d, r: (d // n_new == r // rows) & (d % n_new == r % n_new)
    group_sum = jnp.where(same(lax.broadcasted_iota(jnp.int32, (nd, nr), 0),
                               lax.broadcasted_iota(jnp.int32, (nd, nr), 1)), 1.0, 0.0)
    spread = jnp.where(same(lax.broadcasted_iota(jnp.int32, (nr, nd), 1),
                            lax.broadcasted_iota(jnp.int32, (nr, nd), 0)), 1.0, 0.0).astype(bf16)
    t_d = past_len + lax.broadcasted_iota(jnp.int32, (nd, 1), 0) % n_new
    blk = lax.broadcasted_iota(jnp.int32, (nd, nc), 1)
    valid = blk * SEL_BLOCK <= t_d
    forced = (blk == 0) | (blk == t_d // SEL_BLOCK)
    cos, sin = cos_ref[...], sin_ref[...]
    kcb_t = _rope_rows(cmp_sc[0:NSA_HK, :], cos, sin, NSA_KV_HEADS).astype(bf16)
    vcb_t = cmp_sc[NSA_HK:2 * NSA_HK, :].astype(bf16)
    s_c = jnp.dot(q_bd, kcb_t, preferred_element_type=f32)
    p_c = _softmax_rows(jnp.where(cmask, s_c, NEG_INF))
    p_c = p_c * jnp.where(t_all >= CMP_BLOCK - 1, 1.0, 0.0)
    o_c = lax.dot_general(p_c.astype(bf16), vcb_t, nt, preferred_element_type=f32)
    imp = jnp.dot(group_sum, p_c, precision=lax.Precision.HIGHEST, preferred_element_type=f32)
    imp = imp[:, :nc // 2] + imp[:, nc // 2:]
    imp = jnp.concatenate([imp, jnp.zeros((nd, nc - nc // 2), f32)], axis=1)
    score = jnp.where(forced, FORCE_SCORE, jnp.where(valid, imp, -1.0))
    score = jnp.where(blk < nsb, score, -2.0)
    sel = jnp.dot(spread, _topk_mask_lanes(score, SEL_TOPK, nsb).astype(bf16), preferred_element_type=f32)
    bias_sc[...] = jnp.where(sel > 0.5, 0.0, NEG_INF).astype(bf16)
    in_last = sel[:, nsb - 1:nsb] > 0.5
    m_sc[...] = jnp.full(m_sc.shape, NEG_INF, f32)
    l_sc[...] = jnp.zeros_like(l_sc)
    acc_sc[...] = jnp.zeros_like(acc_sc)

    def flash(s, v, v_transposed):
        m_new = jnp.maximum(m_sc[...], jnp.max(s, -1, keepdims=True))
        alpha = jnp.exp(m_sc[...] - m_new)
        p = jnp.exp(s - m_new)
        l_sc[...] = alpha * l_sc[...] + jnp.sum(p, -1, keepdims=True)
        m_sc[...] = m_new
        if v_transposed:
            pv = lax.dot_general(p.astype(bf16), v, nt, preferred_element_type=f32)
        else:
            pv = jnp.dot(p.astype(bf16), v, preferred_element_type=f32)
        acc_sc[...] = alpha * acc_sc[...] + pv

    def pass2(c, carry):
        @pl.when(c + 1 < 2 * n_chunks)
        def _():
            start_chunk(c + 1)

        wait_chunk(c)
        key0 = (c - n_chunks) * ckeys
        eb = lax.broadcasted_iota(jnp.int32, (nc, ckeys), 0)
        ek = (key0 + lax.broadcasted_iota(jnp.int32, (nc, ckeys), 1)) // SEL_BLOCK
        onehot = jnp.where(eb == ek, 1.0, 0.0).astype(bf16)
        s = (jnp.dot(q_bd, buf[c % 2, 0:NSA_HK, :].astype(bf16), preferred_element_type=f32)
             + jnp.dot(bias_sc[...], onehot, preferred_element_type=f32))
        flash(s, buf[c % 2, NSA_HK:2 * NSA_HK, :].astype(bf16), True)
        return carry

    lax.fori_loop(n_chunks, 2 * n_chunks, pass2, 0)

    jn = lax.broadcasted_iota(jnp.int32, (1, new_pad), 1)
    npos = past_len + jn
    wpos = past_len - WINDOW + lax.broadcasted_iota(jnp.int32, (1, WINDOW), 1)
    ok = in_last & (npos <= t_all) & (jn < n_new)
    s = lax.dot_general(q_bd, new_ref[:, 0:NSA_HK].astype(bf16), nt, preferred_element_type=f32)
    flash(jnp.where(ok, s, NEG_INF), new_ref[:, NSA_HK:2 * NSA_HK].astype(bf16), False)
    o_s = acc_sc[...] / l_sc[...]

    m_sc[...] = jnp.full(m_sc.shape, NEG_INF, f32)
    l_sc[...] = jnp.zeros_like(l_sc)
    acc_sc[...] = jnp.zeros_like(acc_sc)
    ok = (wpos <= t_all) & (wpos > t_all - WINDOW)
    s = jnp.dot(q_bd, win_ref[0:NSA_HK, :].astype(bf16), preferred_element_type=f32)
    flash(jnp.where(ok, s, NEG_INF), win_ref[NSA_HK:2 * NSA_HK, :].astype(bf16), True)
    ok = (npos <= t_all) & (npos > t_all - WINDOW) & (jn < n_new)
    s = lax.dot_general(q_bd, wnew_ref[:, 0:NSA_HK].astype(bf16), nt, preferred_element_type=f32)
    flash(jnp.where(ok, s, NEG_INF), wnew_ref[:, NSA_HK:2 * NSA_HK].astype(bf16), False)
    o_w = acc_sc[...] / l_sc[...]

    g = jnp.concatenate([g_ref[k] for k in range(NSA_KV_HEADS)], axis=0)
    o = g[:, 0:1] * o_c + g[:, 1:2] * o_s + g[:, 2:3] * o_w
    for k in range(NSA_KV_HEADS):
        o_ref[k] = o[rows * k:rows * (k + 1), NSA_HD * k:NSA_HD * (k + 1)]


def _nsa_dec(q, gates, cos_t, sin_t, pool_v, page_table, new_rows, win_t, new_wrows, past_len, n_new):
    nb, n_pages = page_table.shape
    rows = NSA_GROUP * n_new
    w = 2 * NSA_HK
    nc = (past_len + n_new) // CMP_BLOCK
    assert past_len == n_pages * NSA_TILE and past_len % SEL_BLOCK == 0 and n_new <= CMP_BLOCK
    assert win_t.shape[2] == WINDOW and past_len >= WINDOW and n_pages % DEC_CHUNK_PAGES == 0
    assert nc == past_len // CMP_BLOCK and nc % 2 == 0 and -(-(past_len + n_new) // SEL_BLOCK) <= nc
    assert pool_v.shape[1:] == (2, w, NSA_TILE) and cos_t.shape == (NSA_HD // 2, nc)
    ckeys = DEC_CHUNK_PAGES * NSA_TILE
    blk_of_key = jnp.arange(n_pages * NSA_TILE, dtype=jnp.int32).reshape(-1, ckeys, 1) // CMP_BLOCK
    lane_of_blk = jnp.where(blk_of_key % 2 == 0, blk_of_key // 2, nc // 2 + blk_of_key // 2)
    avg = jnp.where(jnp.arange(nc, dtype=jnp.int32) == lane_of_blk, 1.0 / CMP_BLOCK, 0.0).astype(bf16)
    nr = NSA_KV_HEADS * rows
    per_b = lambda shape: pl.BlockSpec((None,) + shape, lambda b, pt: (b,) + (0,) * len(shape))
    const = lambda shape: pl.BlockSpec(shape, lambda b, pt: (0,) * len(shape), pipeline_mode=pl.Buffered(1))
    grid_spec = pltpu.PrefetchScalarGridSpec(
        num_scalar_prefetch=1, grid=(nb,),
        in_specs=[per_b((NSA_KV_HEADS, rows, NSA_HD)), per_b((NSA_KV_HEADS, rows, 3)),
                  const(cos_t.shape), const(sin_t.shape), const(avg.shape), pl.BlockSpec(memory_space=pl.ANY),
                  pl.BlockSpec((None, new_rows.shape[1], w), lambda b, pt: (b, 0, 1)),
                  per_b((w, WINDOW)), per_b((new_wrows.shape[1], w))],
        out_specs=per_b((NSA_KV_HEADS, rows, NSA_HD)),
        scratch_shapes=[pltpu.VMEM((2, w, ckeys), f32), pltpu.SemaphoreType.DMA((2,)),
                        pltpu.VMEM((w, nc), f32), pltpu.VMEM((nr, nc), bf16),
                        pltpu.VMEM((nr, 1), f32), pltpu.VMEM((nr, 1), f32), pltpu.VMEM((nr, NSA_HK), f32)])
    return pl.pallas_call(
        functools.partial(_nsa_dec_kernel, past_len=past_len, n_new=n_new), grid_spec=grid_spec,
        out_shape=jax.ShapeDtypeStruct((nb, NSA_KV_HEADS, rows, NSA_HD), f32),
        compiler_params=_cparams(("arbitrary",)), name="nsa_dec",
    )(page_table, q, gates, cos_t, sin_t, avg, pool_v, new_rows, win_t, new_wrows)


def _nsa_layer(xp, xs, nbp, nbs, pool, win, page_table, ln_g, ln_b, w_in, w_out):
    hq, hk = NSA_HQ, NSA_HK
    ws = [w_in[:, :hq].astype(bf16), w_in[:, hq:hq + 4 * hk].astype(bf16),
          w_in[:, hq + 4 * hk:hq + 6 * hk].astype(bf16), w_in[:, hq + 6 * hk:].astype(bf16)]
    wo = w_out.astype(bf16)
    eps = [_q_ep, _rows_ep, _wrows_ep, _sigmoid_ep]
    sp_len = xp.shape[0] // nbp
    ss_len = xs.shape[0] // nbs
    n_pages = page_table.shape[1]
    page = pool.shape[1]
    assert page == NSA_TILE
    past_len = n_pages * page
    ms = xs.shape[0]

    cos_s, sin_s = _rope_tables(jnp.tile(past_len + jnp.arange(ss_len, dtype=f32), nbs))
    nat_p, q_t, rows_t, wrows_t, g_t = _nsa_in(xp, nbp, w_in)
    qs, rows_s, wrows_s, gs = _proj(xs, ws, eps, aux=(cos_s, sin_s), aux_period=(ms, ms), tm=ms, name="nsa_in_s")

    def cmp_tables(nc):
        return _rope_tables(jnp.arange(nc, dtype=f32) * CMP_BLOCK + (CMP_BLOCK - 1) / 2.0)

    pages_p = sp_len // page
    pt_p = jnp.arange(nbp * pages_p, dtype=jnp.int32).reshape(nbp, pages_p)
    cmp_p = _even_odd(_nsa_cmp(nat_p.reshape(nbp * pages_p, page, 4 * hk), pt_p, *cmp_tables(sp_len // CMP_BLOCK)))
    op = _nsa_seq(q_t, g_t, cmp_p[:, :, :hk], cmp_p[:, :, hk:].transpose(0, 2, 1),
                  nat_p.reshape(nbp, sp_len, 4 * hk), rows_t, wrows_t)

    pool_v = pool.transpose(0, 2, 3, 4, 1).reshape(pool.shape[0], 2, 2 * hk, page)
    win_t = win.transpose(0, 2, 3, 4, 1).reshape(nbs, 2 * hk, win.shape[1])
    nc_s = past_len // CMP_BLOCK
    cpos = jnp.arange(nc_s, dtype=f32) * CMP_BLOCK + (CMP_BLOCK - 1) / 2.0
    cpos = jnp.concatenate([cpos[0::2], cpos[1::2]])
    inv = ROPE_THETA ** (-jnp.arange(0, NSA_HD, 2, dtype=f32) / NSA_HD)
    ang_t = inv[:, None] * cpos[None, :]
    rows_s3 = rows_s.reshape(nbs, ss_len, 4 * hk)
    wrows_s3 = wrows_s.reshape(nbs, ss_len, 2 * hk)
    new_pad = 8
    tpad = lambda a: jnp.pad(a, ((0, 0), (0, new_pad - ss_len), (0, 0)))
    to_heads = lambda a, last: (a.reshape(nbs, ss_len, NSA_KV_HEADS, NSA_GROUP, last).transpose(0, 2, 3, 1, 4)
                                .reshape(nbs, NSA_KV_HEADS, NSA_GROUP * ss_len, last))
    os_ = _nsa_dec(to_heads(qs, NSA_HD), to_heads(gs, 3), jnp.cos(ang_t), jnp.sin(ang_t), pool_v, page_table,
                   tpad(rows_s3), win_t, tpad(wrows_s3), past_len, ss_len)
    os_ = (os_.reshape(nbs, NSA_KV_HEADS, NSA_GROUP, ss_len, NSA_HD).transpose(0, 3, 1, 2, 4).reshape(ms, hq))

    xp = _outproj_ln(op.reshape(nbp * sp_len, hq), wo, xp, ln_g, ln_b, name="nsa_out")
    xs = _outproj_ln(os_, wo, xs, ln_g, ln_b, tm=ms, name="nsa_out_s")
    kv_shape = (4, NSA_KV_HEADS, NSA_HD)
    win_shape = (2, NSA_KV_HEADS, NSA_HD)
    from_t = lambda a, shape: a.reshape((a.shape[0],) + shape + (a.shape[2],)).transpose(0, 4, 1, 2, 3)
    rp = from_t(rows_t, kv_shape)
    rs = rows_s.reshape((nbs, ss_len) + kv_shape)
    wp = from_t(wrows_t[:, :, sp_len - min(WINDOW, sp_len):], win_shape)
    wsn_t = jnp.concatenate([win_t, wrows_s3.transpose(0, 2, 1)], axis=2)
    wsn = from_t(wsn_t[:, :, wsn_t.shape[2] - min(WINDOW, wsn_t.shape[2]):], win_shape)
    return xp, xs, (rp, rs, wp, wsn)


FFN_TF = 256
MOE_TF = 512
ROW_TILE = 1024


def kernel(x_prompt, x_sample, state_l0_lru_conv, state_l0_lru_h, cache_l1_nsa_kv, cache_l1_nsa_win, page_table,
           state_l2_ssd_conv, state_l2_ssd_ssm, state_l3_lru_conv, state_l3_lru_h, ln_g, ln_b, lru_w_in,
           lru_conv_w, lru_conv_b, lru_w_a, lru_b_a, lru_w_x, lru_b_x, lru_lam, lru_w_out, nsa_w_in, nsa_w_out,
           ssd_w_in, ssd_conv_w, ssd_conv_b, ssd_dt_bias, ssd_a_log, ssd_d, ssd_norm_g, ssd_w_out, ffn_w_in,
           ffn_w_out, moe_router_w, moe_router_b, moe_w_in, moe_w_out):
    nbp, sp_len, d = x_prompt.shape
    nbs, ss_len, _ = x_sample.shape
    xp = x_prompt.reshape(nbp * sp_len, d)
    xs = x_sample.reshape(nbs * ss_len, d)
    ms = xs.shape[0]
    lru_state = {0: (state_l0_lru_conv, state_l0_lru_h), 3: (state_l3_lru_conv, state_l3_lru_h)}
    new = {}
    for i in range(DEPTH):
        kind, j = i % 3, i // 3
        g0, b0 = ln_g[i, 0], ln_b[i, 0]
        if kind == 0:
            conv_s, h_s = lru_state[i]
            xp, xs, new[i] = _lru_layer(xp, xs, nbp, nbs, conv_s, h_s, g0, b0, lru_w_in[j], lru_conv_w[j],
                                        lru_conv_b[j], lru_w_a[j], lru_b_a[j], lru_w_x[j], lru_b_x[j],
                                        lru_lam[j], lru_w_out[j])
        elif kind == 1:
            xp, xs, new[i] = _nsa_layer(xp, xs, nbp, nbs, cache_l1_nsa_kv, cache_l1_nsa_win, page_table, g0, b0,
                                        nsa_w_in[j], nsa_w_out[j])
        else:
            xp, xs, new[i] = _ssd_layer(xp, xs, nbp, nbs, state_l2_ssd_conv, state_l2_ssd_ssm, g0, b0,
                                        ssd_w_in[j], ssd_conv_w[j], ssd_conv_b[j], ssd_dt_bias[j], ssd_a_log[j],
                                        ssd_d[j], ssd_norm_g[j], ssd_w_out[j])
        g1, b1 = ln_g[i, 1], ln_b[i, 1]
        k = i // 2
        if i % 2 == 0:
            xp = _ffn_ln(xp, ffn_w_in, ffn_w_out, k, g1, b1, ROW_TILE, FFN_TF, name="ffn")
            xs = _ffn_ln(xs, ffn_w_in, ffn_w_out, k, g1, b1, ms, FFN_TF, name="ffn_s")
        else:
            xp = _moe_sparse_ln(xp, moe_router_w[k], moe_router_b[k], moe_w_in, moe_w_out, k, g1, b1, MOE_TF)
            cs = _router(xs, moe_router_w[k], moe_router_b[k], tm=ms)
            xs = _moe_ln(xs, cs, moe_w_in, moe_w_out, k, g1, b1, ms, MOE_TF)
    out = [xp.reshape(nbp, sp_len, d), xs.reshape(nbs, ss_len, d)]
    for i in range(DEPTH):
        out.extend(new[i])
    return tuple(out)
```
